```python
import math
import jax, jax.numpy as jnp
from jax import lax
import numpy as np

D_MODEL = 2048
BATCH = 8
SEQ = 2048
DEPTH = 2
DEC_BATCH = 32
DEC_SEQ = 8
PAST_LEN = 8192
PAGE_SIZE = 128

D_MIX = D_MODEL
GROUP_W = D_MIX // 4
GLA_HEADS = 4
GLA_DV = GROUP_W // GLA_HEADS
GLA_DK = GLA_DV // 2
GLA_GATE_RANK = 16
GLA_TAU = 16.0
DIFF_HEADS = 4
DIFF_DV = GROUP_W // DIFF_HEADS
DIFF_DQK = DIFF_DV // 2
RET_HEADS = 4
RET_DV = GROUP_W // RET_HEADS
RET_DK = RET_DV // 2
ROPE_BASE = 10000.0
RWKV_N = 64
RWKV_HEADS = GROUP_W // RWKV_N
RWKV_W_RANK = 64
RWKV_A_RANK = 64
RWKV_SHIFT_W = 3 * GROUP_W + RWKV_W_RANK + RWKV_A_RANK
REL_BUCKETS = 32
REL_MAX_DIST = 128
CHUNK = 16
Q_BLOCK = 128
EPS = 1e-6

GLA_COLS = (GLA_HEADS * GLA_DK, GLA_HEADS * GLA_DK, GROUP_W, GLA_GATE_RANK, GROUP_W)
DIFF_COLS = (DIFF_HEADS * 2 * DIFF_DQK, DIFF_HEADS * 2 * DIFF_DQK, GROUP_W, GROUP_W)
RET_COLS = (RET_HEADS * RET_DK, RET_HEADS * RET_DK, GROUP_W, GROUP_W)
RWKV_COLS = (RWKV_SHIFT_W, GROUP_W)
MIXER_COLS = (sum(GLA_COLS), sum(DIFF_COLS), sum(RET_COLS), sum(RWKV_COLS))
IN_W = sum(MIXER_COLS)

kernel_name = "hymba_style_gla_diff_ret_rwkv7_adaln_step"


def split_cols(p, sizes):
    out, o = [], 0
    for s in sizes:
        out.append(p[..., o:o + s])
        o += s
    return out


def rms_norm(x, w):
    xf = x.astype(jnp.float32)
    y = xf * lax.rsqrt(jnp.mean(xf * xf, axis=-1, keepdims=True) + EPS)
    return (y * w.astype(jnp.float32)).astype(x.dtype)


def head_rms(o, w):
    of = o.astype(jnp.float32)
    y = of * lax.rsqrt(jnp.mean(of * of, axis=-1, keepdims=True) + EPS)
    return (y * w.astype(jnp.float32)).astype(o.dtype)


def t5_bucket(rel):
    n = jnp.maximum(rel, 0)
    max_exact = REL_BUCKETS // 2
    nf = jnp.maximum(n, 1).astype(jnp.float32)
    large = max_exact + (jnp.log(nf / max_exact) / math.log(REL_MAX_DIST / max_exact)
                         * (REL_BUCKETS - max_exact)).astype(jnp.int32)
    large = jnp.minimum(large, REL_BUCKETS - 1)
    return jnp.where(n < max_exact, n, large)


def rope(x, pos):
    half = x.shape[-1] // 2
    inv = ROPE_BASE ** (-jnp.arange(half, dtype=jnp.float32) / half)
    ang = pos.astype(jnp.float32)[:, None] * inv[None, :]
    cos = jnp.cos(ang)[None, :, None, :]
    sin = jnp.sin(ang)[None, :, None, :]
    xf = x.astype(jnp.float32)
    x1, x2 = xf[..., :half], xf[..., half:]
    return jnp.concatenate([x1 * cos - x2 * sin, x1 * sin + x2 * cos], axis=-1).astype(x.dtype)


def chunked_decay_recurrence(q, k, v, log_a, s0):
    f32 = jnp.float32
    out_dtype = v.dtype
    B, T, H, K = q.shape
    V = v.shape[-1]
    C = min(CHUNK, T)
    pad = (-T) % C
    q, k, v, log_a = (a.astype(f32) for a in (q, k, v, log_a))
    if pad:
        padw = ((0, 0), (0, pad), (0, 0), (0, 0))
        q, k, v, log_a = (jnp.pad(a, padw) for a in (q, k, v, log_a))
    N = (T + pad) // C
    q = q.reshape(B, N, C, H, K)
    k = k.reshape(B, N, C, H, K)
    v = v.reshape(B, N, C, H, V)
    b = jnp.cumsum(log_a.reshape(B, N, C, H, K), axis=2)
    b_last = b[:, :, -1:]
    causal = jnp.tril(jnp.ones((C, C), dtype=bool))
    expo = jnp.where(causal[None, None, :, :, None, None],
                     b[:, :, :, None] - b[:, :, None, :], -jnp.inf)
    scores = jnp.einsum('bnthk,bnshk,bntshk->bnhts', q, k, jnp.exp(expo))
    intra = jnp.einsum('bnhts,bnshv->bnthv', scores, v)
    d_state = jnp.einsum('bnshk,bnshv->bnhkv', k * jnp.exp(b_last - b), v)
    g = jnp.exp(b_last[:, :, 0])

    def step(S, inp):
        g_n, ds_n = inp
        return g_n[..., None] * S + ds_n, S

    s_fin, s_prev = lax.scan(step, s0.astype(f32), (jnp.moveaxis(g, 1, 0), jnp.moveaxis(d_state, 1, 0)))
    s_prev = jnp.moveaxis(s_prev, 0, 1)
    inter = jnp.einsum('bnthk,bnhkv->bnthv', q * jnp.exp(b), s_prev)
    o = (intra + inter).reshape(B, N * C, H, V)[:, :T]
    return o.astype(out_dtype), s_fin.astype(s0.dtype)


def gla_branch(cols, lp, s0):
    q, k, v, a_lo, g = split_cols(cols, GLA_COLS)
    B, T, _ = q.shape
    q = q.reshape(B, T, GLA_HEADS, GLA_DK) * GLA_DK ** -0.5
    k = k.reshape(B, T, GLA_HEADS, GLA_DK)
    v = v.reshape(B, T, GLA_HEADS, GLA_DV)
    log_a = jax.nn.log_sigmoid((a_lo @ lp['gla_w_a2'] + lp['gla_b_a']).astype(jnp.float32)) / GLA_TAU
    log_a = log_a.reshape(B, T, GLA_HEADS, GLA_DK)
    o, s = chunked_decay_recurrence(q, k, v, log_a, s0)
    o = head_rms(o, lp['gla_norm']).reshape(B, T, GROUP_W)
    return o * jax.nn.silu(g), s


def diff_attend(q, k, v, q_pos, k_pos, rel_bias, lam, lam_init, subln_w):
    f32 = jnp.float32
    B, Tq, H, _ = q.shape
    Tk = k.shape[1]
    qf = q.astype(f32).reshape(B, Tq, H, 2, DIFF_DQK)
    kf = k.astype(f32).reshape(B, Tk, H, 2, DIFF_DQK)
    s = jnp.einsum('bqhjd,bkhjd->bhjqk', qf, kf) * DIFF_DQK ** -0.5
    rel = q_pos[:, None] - k_pos[None, :]
    bias = jnp.moveaxis(rel_bias.astype(f32)[t5_bucket(rel)], -1, 0)
    s = jnp.where(rel >= 0, s + bias[:, None], -jnp.inf)
    p = jax.nn.softmax(s, axis=-1)
    w = p[:, :, 0] - lam * p[:, :, 1]
    o = jnp.einsum('bhqk,bkhe->bqhe', w, v.astype(f32))
    return (head_rms(o, subln_w) * (1.0 - lam_init)).astype(v.dtype)


def diff_attention_blocks(q, k, v, pos, lam, lam_init, rel_bias, subln_w):
    B, S, H, Dq = q.shape
    blk = min(Q_BLOCK, S)
    nb = S // blk
    qb = jnp.moveaxis(q.reshape(B, nb, blk, H, Dq), 1, 0)
    pb = pos.reshape(nb, blk)
    out = lax.map(lambda a: diff_attend(a[0], k, v, a[1], pos, rel_bias, lam, lam_init, subln_w), (qb, pb))
    return jnp.moveaxis(out, 0, 1).reshape(B, S, H, DIFF_DV)


def diff_branch(cols, lp, layer_idx, pos, kv_past, rel_bias):
    f32 = jnp.float32
    q, k, v, g = split_cols(cols, DIFF_COLS)
    B, T, _ = q.shape
    q = q.reshape(B, T, DIFF_HEADS, 2 * DIFF_DQK)
    k = k.reshape(B, T, DIFF_HEADS, 2 * DIFF_DQK)
    v = v.reshape(B, T, DIFF_HEADS, DIFF_DV)
    lam_init = 0.8 - 0.6 * math.exp(-0.3 * layer_idx)
    lam = (jnp.exp(jnp.sum(lp['diff_lam_q1'].astype(f32) * lp['diff_lam_k1'].astype(f32)))
           - jnp.exp(jnp.sum(lp['diff_lam_q2'].astype(f32) * lp['diff_lam_k2'].astype(f32))) + lam_init)
    if kv_past is None:
        o = diff_attention_blocks(q, k, v, pos, lam, lam_init, rel_bias, lp['diff_subln'])
    else:
        k_all = jnp.concatenate([kv_past[0].astype(k.dtype), k], axis=1)
        v_all = jnp.concatenate([kv_past[1].astype(v.dtype), v], axis=1)
        k_pos = jnp.arange(k_all.shape[1], dtype=jnp.int32)
        o = diff_attend(q, k_all, v_all, pos, k_pos, rel_bias, lam, lam_init, lp['diff_subln'])
    return o.reshape(B, T, GROUP_W) * jax.nn.silu(g), k, v


def retention_branch(cols, lp, s0, pos):
    q, k, v, g = split_cols(cols, RET_COLS)
    B, T, _ = q.shape
    q = rope(q.reshape(B, T, RET_HEADS, RET_DK), pos)
    k = rope(k.reshape(B, T, RET_HEADS, RET_DK), pos) * RET_DK ** -0.5
    v = v.reshape(B, T, RET_HEADS, RET_DV)
    log_gamma = jnp.log(1.0 - 2.0 ** (-5.0 - jnp.arange(RET_HEADS, dtype=jnp.float32)))
    log_a = jnp.broadcast_to(log_gamma[None, None, :, None], (B, T, RET_HEADS, RET_DK))
    o, s = chunked_decay_recurrence(q, k, v, log_a, s0)
    o = head_rms(o, lp['ret_norm']).reshape(B, T, GROUP_W)
    return o * jax.nn.silu(g), s


def rwkv7_branch(cols, lp, s0, shift0):
    f32 = jnp.float32
    p, g = split_cols(cols, RWKV_COLS)
    B, T, _ = p.shape
    prev = jnp.concatenate([shift0[:, None].astype(p.dtype), p[:, :-1]], axis=1)
    pm = p + lp['rwkv_mu'] * (prev - p)
    r, k, v, w_lo, a_lo = split_cols(pm, (GROUP_W, GROUP_W, GROUP_W, RWKV_W_RANK, RWKV_A_RANK))
    w_pre = (lp['rwkv_w0'] + jnp.tanh(w_lo) @ lp['rwkv_w_w2']).astype(f32)
    log_w = -jnp.exp(-jax.nn.softplus(-w_pre) - 0.5)
    a = jax.nn.sigmoid((lp['rwkv_a0'] + a_lo @ lp['rwkv_a_w2']).astype(f32))

    def heads(t):
        return t.astype(f32).reshape(B, T, RWKV_HEADS, RWKV_N)

    r, k, v, log_w, a = heads(r), heads(k), heads(v), heads(log_w), heads(a)
    k_k = lp['rwkv_k_k'].astype(f32).reshape(RWKV_HEADS, RWKV_N)
    k_a = lp['rwkv_k_a'].astype(f32).reshape(RWKV_HEADS, RWKV_N)
    kk = k * k_k
    kk = kk / jnp.maximum(jnp.sqrt(jnp.sum(kk * kk, axis=-1, keepdims=True)), 1e-12)
    k = k * (1.0 + (a - 1.0) * k_a)
    seq = tuple(jnp.moveaxis(t, 1, 0) for t in (r, jnp.exp(log_w), k, v, kk, kk * a))

    def step(S, inp):
        r_t, w_t, k_t, v_t, kk_t, b_t = inp
        sa = jnp.einsum('bhvk,bhk->bhv', S, kk_t)
        S = S * w_t[:, :, None, :] - sa[..., None] * b_t[:, :, None, :] + v_t[..., None] * k_t[:, :, None, :]
        return S, jnp.einsum('bhvk,bhk->bhv', S, r_t)

    s_fin, o = lax.scan(step, s0.astype(f32), seq)
    o = jnp.moveaxis(o, 0, 1)
    o = head_rms(o, lp['rwkv_norm']) + jnp.sum(r * k * lp['rwkv_r_k'].astype(f32), axis=-1, keepdims=True) * v
    o = o.reshape(B, T, GROUP_W).astype(cols.dtype) * jax.nn.silu(g)
    return o, s_fin.astype(s0.dtype), p[:, -1]


def trunk_layer(x, c, lp, layer_idx, pos, s_gla, s_ret, s_rwkv, s_shift, kv_past, rel_bias):
    mod = jax.nn.silu(c) @ lp['w_ada'] + lp['b_ada']
    shift, scale, gate = split_cols(mod, (D_MODEL, D_MODEL, D_MODEL))
    h = rms_norm(x, lp['norm_w']) * (1.0 + scale[:, None]) + shift[:, None]
    cols = h @ lp['w_in']
    ca, cb, cc, cd = split_cols(cols, MIXER_COLS)
    oa, s_gla = gla_branch(ca, lp, s_gla)
    ob, k_new, v_new = diff_branch(cb, lp, layer_idx, pos, kv_past, rel_bias)
    oc, s_ret = retention_branch(cc, lp, s_ret, pos)
    od, s_rwkv, s_shift = rwkv7_branch(cd, lp, s_rwkv, s_shift)
    mix = jnp.concatenate([oa, ob, oc, od], axis=-1)
    y = x + gate[:, None] * (mix @ lp['w_o'])
    return y, (k_new, v_new, s_gla, s_ret, s_rwkv, s_shift)


def setup_inputs(seed: int = 0) -> dict:
    key = jax.random.key(seed)
    ks = jax.random.split(key, 48)
    ctr = [0]

    def nxt():
        ctr[0] += 1
        return ks[ctr[0] - 1]

    def nrm(shape, scale):
        return scale * jax.random.normal(nxt(), shape, jnp.float32)

    def uni(shape, lo, hi):
        return jax.random.uniform(nxt(), shape, jnp.float32, lo, hi)

    n_pages = PAST_LEN // PAGE_SIZE
    n_pool = (5 * DEC_BATCH * n_pages + 3) // 4
    perm = jax.random.permutation(nxt(), n_pool)
    page_table = perm[: DEC_BATCH * n_pages].reshape(DEC_BATCH, n_pages).astype(jnp.int32)
    return {
        'x_prompt': nrm((BATCH, SEQ, D_MODEL), 1.0),
        'x_sample': nrm((DEC_BATCH, DEC_SEQ, D_MODEL), 1.0),
        'cache_k': nrm((DEPTH, n_pool, PAGE_SIZE, DIFF_HEADS, 2 * DIFF_DQK), 1.0),
        'cache_v': nrm((DEPTH, n_pool, PAGE_SIZE, DIFF_HEADS, DIFF_DV), 1.0),
        'state_gla': nrm((DEPTH, DEC_BATCH, GLA_HEADS, GLA_DK, GLA_DV), 0.3),
        'state_ret': nrm((DEPTH, DEC_BATCH, RET_HEADS, RET_DK, RET_DV), 0.3),
        'state_rwkv': nrm((DEPTH, DEC_BATCH, RWKV_HEADS, RWKV_N, RWKV_N), 0.3),
        'state_shift': nrm((DEPTH, DEC_BATCH, RWKV_SHIFT_W), 1.0),
        'page_table': page_table,
        'c_prompt': nrm((BATCH, D_MODEL), 1.0),
        'c_sample': nrm((DEC_BATCH, D_MODEL), 1.0),
        'rel_bias': nrm((REL_BUCKETS, DIFF_HEADS), 0.5),
        'w_ada': nrm((DEPTH, D_MODEL, 3 * D_MODEL), 0.5 * D_MODEL ** -0.5),
        'b_ada': nrm((DEPTH, 3 * D_MODEL), 0.01),
        'norm_w': 1.0 + nrm((DEPTH, D_MODEL), 0.02),
        'w_in': nrm((DEPTH, D_MODEL, IN_W), D_MODEL ** -0.5),
        'w_o': nrm((DEPTH, D_MIX, D_MODEL), D_MIX ** -0.5),
        'gla_w_a2': nrm((DEPTH, GLA_GATE_RANK, GLA_HEADS * GLA_DK), GLA_GATE_RANK ** -0.5),
        'gla_b_a': nrm((DEPTH, GLA_HEADS * GLA_DK), 0.1),
        'gla_norm': 1.0 + nrm((DEPTH, GLA_DV), 0.02),
        'diff_lam_q1': nrm((DEPTH, DIFF_DQK), 0.1),
        'diff_lam_k1': nrm((DEPTH, DIFF_DQK), 0.1),
        'diff_lam_q2': nrm((DEPTH, DIFF_DQK), 0.1),
        'diff_lam_k2': nrm((DEPTH, DIFF_DQK), 0.1),
        'diff_subln': 1.0 + nrm((DEPTH, DIFF_DV), 0.02),
        'ret_norm': 1.0 + nrm((DEPTH, RET_DV), 0.02),
        'rwkv_mu': uni((DEPTH, RWKV_SHIFT_W), 0.0, 1.0),
        'rwkv_w0': uni((DEPTH, GROUP_W), -3.0, 0.0),
        'rwkv_w_w2': nrm((DEPTH, RWKV_W_RANK, GROUP_W), 0.1),
        'rwkv_a0': nrm((DEPTH, GROUP_W), 0.1),
        'rwkv_a_w2': nrm((DEPTH, RWKV_A_RANK, GROUP_W), 0.1),
        'rwkv_k_k': 1.0 + nrm((DEPTH, GROUP_W), 0.1),
        'rwkv_k_a': 1.0 + nrm((DEPTH, GROUP_W), 0.1),
        'rwkv_r_k': nrm((DEPTH, RWKV_HEADS, RWKV_N), 0.1),
        'rwkv_norm': 1.0 + nrm((DEPTH, RWKV_N), 0.02),
        'final_norm': 1.0 + nrm((D_MODEL,), 0.02),
    }


def reference(x_prompt, x_sample, cache_k, cache_v, state_gla, state_ret, state_rwkv, state_shift,
              page_table, c_prompt, c_sample, rel_bias, w_ada, b_ada, norm_w, w_in, w_o,
              gla_w_a2, gla_b_a, gla_norm, diff_lam_q1, diff_lam_k1, diff_lam_q2, diff_lam_k2, diff_subln,
              ret_norm, rwkv_mu, rwkv_w0, rwkv_w_w2, rwkv_a0, rwkv_a_w2, rwkv_k_k, rwkv_k_a, rwkv_r_k,
              rwkv_norm, final_norm):
    Bp, Sp = x_prompt.shape[0], x_prompt.shape[1]
    Bs, Ts = x_sample.shape[0], x_sample.shape[1]
    past_len = page_table.shape[1] * cache_k.shape[2]
    pos_p = jnp.arange(Sp, dtype=jnp.int32)
    pos_s = past_len + jnp.arange(Ts, dtype=jnp.int32)
    dt = x_prompt.dtype
    hp, hs = x_prompt, x_sample
    new_p, new_s = [], []
    for l in range(DEPTH):
        lp = dict(w_ada=w_ada[l], b_ada=b_ada[l], norm_w=norm_w[l], w_in=w_in[l], w_o=w_o[l],
                  gla_w_a2=gla_w_a2[l], gla_b_a=gla_b_a[l], gla_norm=gla_norm[l],
                  diff_lam_q1=diff_lam_q1[l], diff_lam_k1=diff_lam_k1[l],
                  diff_lam_q2=diff_lam_q2[l], diff_lam_k2=diff_lam_k2[l], diff_subln=diff_subln[l],
                  ret_norm=ret_norm[l], rwkv_mu=rwkv_mu[l], rwkv_w0=rwkv_w0[l], rwkv_w_w2=rwkv_w_w2[l],
                  rwkv_a0=rwkv_a0[l], rwkv_a_w2=rwkv_a_w2[l], rwkv_k_k=rwkv_k_k[l], rwkv_k_a=rwkv_k_a[l],
                  rwkv_r_k=rwkv_r_k[l], rwkv_norm=rwkv_norm[l])
        hp, st_p = trunk_layer(
            hp, c_prompt, lp, l, pos_p,
            jnp.zeros((Bp, GLA_HEADS, GLA_DK, GLA_DV), dt),
            jnp.zeros((Bp, RET_HEADS, RET_DK, RET_DV), dt),
            jnp.zeros((Bp, RWKV_HEADS, RWKV_N, RWKV_N), dt),
            jnp.zeros((Bp, RWKV_SHIFT_W), dt),
            None, rel_bias)
        k_past = cache_k[l, page_table].reshape(Bs, past_len, DIFF_HEADS, 2 * DIFF_DQK)
        v_past = cache_v[l, page_table].reshape(Bs, past_len, DIFF_HEADS, DIFF_DV)
        hs, st_s = trunk_layer(
            hs, c_sample, lp, l, pos_s,
            state_gla[l], state_ret[l], state_rwkv[l], state_shift[l],
            (k_past, v_past), rel_bias)
        new_p.append(st_p)
        new_s.append(st_s)
    y_prompt = rms_norm(hp, final_norm)
    y_sample = rms_norm(hs, final_norm)
    k_prompt = jnp.stack([s[0] for s in new_p])
    v_prompt = jnp.stack([s[1] for s in new_p])
    k_sample = jnp.stack([s[0] for s in new_s])
    v_sample = jnp.stack([s[1] for s in new_s])
    gla_prompt = jnp.stack([s[2] for s in new_p])
    gla_sample = jnp.stack([s[2] for s in new_s])
    ret_prompt = jnp.stack([s[3] for s in new_p])
    ret_sample = jnp.stack([s[3] for s in new_s])
    rwkv_prompt = jnp.stack([s[4] for s in new_p])
    rwkv_sample = jnp.stack([s[4] for s in new_s])
    shift_prompt = jnp.stack([s[5] for s in new_p])
    shift_sample = jnp.stack([s[5] for s in new_s])
    return (y_prompt, y_sample, k_prompt, v_prompt, k_sample, v_sample, gla_prompt, gla_sample,
            ret_prompt, ret_sample, rwkv_prompt, rwkv_sample, shift_prompt, shift_sample)
```

```python
import functools
import math

import jax
import jax.numpy as jnp
from jax import lax
from jax.experimental import pallas as pl
from jax.experimental.pallas import tpu as pltpu

F32 = jnp.float32
BF16 = jnp.bfloat16
HIGHEST = lax.Precision.HIGHEST

D_MODEL = 2048
GROUP_W = 512
N_HEADS = 4
DK = 64
DV = 128
GLA_RANK = 16
GLA_TAU = 16.0
RWKV_N = 64
RWKV_HEADS = 8
RWKV_LORA = 64
ROPE_BASE = 10000.0
REL_BUCKETS = 32
REL_MAX_DIST = 128
CHUNK = 16
EPS = 1e-6
NEG_BIG = -1e30

LANE = 128
VMEM_LIMIT = 48 * 1024 * 1024

COLS_W = 15 * GROUP_W
BLK_GLA_QK, BLK_GLA_V, BLK_GLA_G = 0, 1, 2
BLK_DIFF_Q, BLK_DIFF_K, BLK_DIFF_V, BLK_DIFF_G = 3, 4, 5, 6
BLK_RET_QK, BLK_RET_V, BLK_RET_G = 7, 8, 9
BLK_RWKV_R, BLK_RWKV_K, BLK_RWKV_V, BLK_RWKV_G = 10, 11, 12, 13
BLK_MISC = 14


def _mm(a, b):
    return jnp.dot(a.astype(BF16), b.astype(BF16), preferred_element_type=F32)


def _mm_nt(a, b):
    return lax.dot_general(a.astype(BF16), b.astype(BF16), (((1,), (1,)), ((), ())), preferred_element_type=F32)


def _mm_tn(a, b):
    return lax.dot_general(a.astype(BF16), b.astype(BF16), (((0,), (0,)), ((), ())), preferred_element_type=F32)


def _mm_exact(a, b):
    return jnp.dot(a, b, precision=HIGHEST, preferred_element_type=F32)


def _mm_tn_exact(a, b):
    return lax.dot_general(a, b, (((0,), (0,)), ((), ())), precision=HIGHEST, preferred_element_type=F32)


def _sigmoid(x):
    return 1.0 / (1.0 + jnp.exp(-x))


def _softplus(x):
    return jnp.maximum(x, 0.0) + jnp.log1p(jnp.exp(-jnp.abs(x)))


def _iota(shape, dim):
    return lax.broadcasted_iota(jnp.int32, shape, dim)


def _params(sem):
    return pltpu.CompilerParams(dimension_semantics=sem, vmem_limit_bytes=VMEM_LIMIT)


def _ada_kernel(c_ref, w_ref, b_ref, o_ref):
    c = c_ref[...]
    o_ref[...] = _mm(c * _sigmoid(c), w_ref[...]) + b_ref[...]


def ada_mod(c, w_ada, b_ada, tn=512):
    rows, d = c.shape
    n = w_ada.shape[1]
    return pl.pallas_call(
        _ada_kernel,
        grid=(n // tn,),
        in_specs=[pl.BlockSpec((rows, d), lambda j: (0, 0)),
                  pl.BlockSpec((d, tn), lambda j: (0, j)),
                  pl.BlockSpec((1, tn), lambda j: (0, j))],
        out_specs=pl.BlockSpec((rows, tn), lambda j: (0, j)),
        out_shape=jax.ShapeDtypeStruct((rows, n), F32),
        compiler_params=_params(("arbitrary",)),
        name="ada_mod",
    )(c, w_ada, b_ada.reshape(1, n))


def _inproj_kernel(x_ref, scale_ref, shift_ref, nw_ref, w_ref, o_ref, h_ref):
    @pl.when(pl.program_id(1) == 0)
    def _():
        x = x_ref[...]
        y = x * lax.rsqrt(jnp.mean(x * x, axis=-1, keepdims=True) + EPS) * nw_ref[...]
        h_ref[...] = (y * (1.0 + scale_ref[...]) + shift_ref[...]).astype(BF16)

    o_ref[...] = jnp.dot(h_ref[...], w_ref[...], preferred_element_type=F32)


def in_proj(x, scale, shift, norm_w, w_pad, rows_per_mod, tm, tn):
    m, d = x.shape
    n = w_pad.shape[1]
    if rows_per_mod == 1:
        mod_spec = pl.BlockSpec((tm, d), lambda i, j: (i, 0))
    else:
        per = rows_per_mod // tm
        mod_spec = pl.BlockSpec((None, 1, d), lambda i, j: (i // per, 0, 0))
    return pl.pallas_call(
        _inproj_kernel,
        grid=(m // tm, n // tn),
        in_specs=[pl.BlockSpec((tm, d), lambda i, j: (i, 0)),
                  mod_spec, mod_spec,
                  pl.BlockSpec((1, d), lambda i, j: (0, 0)),
                  pl.BlockSpec((d, tn), lambda i, j: (0, j))],
        out_specs=pl.BlockSpec((tm, tn), lambda i, j: (i, j)),
        out_shape=jax.ShapeDtypeStruct((m, n), F32),
        scratch_shapes=[pltpu.VMEM((tm, d), BF16)],
        compiler_params=_params(("parallel", "arbitrary")),
        name="in_proj",
    )(x, scale, shift, norm_w.reshape(1, d), w_pad)


def _outproj_kernel(final, x_ref, gate_ref, oa_ref, ob_ref, oc_ref, od_ref, w_ref, fw_ref, y_ref):
    acc = jnp.dot(oa_ref[...], w_ref[0 * GROUP_W:1 * GROUP_W, :], preferred_element_type=F32)
    acc += jnp.dot(ob_ref[...], w_ref[1 * GROUP_W:2 * GROUP_W, :], preferred_element_type=F32)
    acc += jnp.dot(oc_ref[...], w_ref[2 * GROUP_W:3 * GROUP_W, :], preferred_element_type=F32)
    acc += jnp.dot(od_ref[...], w_ref[3 * GROUP_W:4 * GROUP_W, :], preferred_element_type=F32)
    y = x_ref[...] + gate_ref[...] * acc
    if final:
        y = y * lax.rsqrt(jnp.mean(y * y, axis=-1, keepdims=True) + EPS) * fw_ref[...]
    y_ref[...] = y


def out_proj(x, gate, oa, ob, oc, od, w_o, final_w, rows_per_mod, tm, final):
    m, d = x.shape
    if rows_per_mod == 1:
        mod_spec = pl.BlockSpec((tm, d), lambda i: (i, 0))
    else:
        per = rows_per_mod // tm
        mod_spec = pl.BlockSpec((None, 1, d), lambda i: (i // per, 0, 0))
    o_spec = pl.BlockSpec((tm, GROUP_W), lambda i: (i, 0))
    return pl.pallas_call(
        functools.partial(_outproj_kernel, final),
        grid=(m // tm,),
        in_specs=[pl.BlockSpec((tm, d), lambda i: (i, 0)), mod_spec,
                  o_spec, o_spec, o_spec, o_spec,
                  pl.BlockSpec((d, d), lambda i: (0, 0)),
                  pl.BlockSpec((1, d), lambda i: (0, 0))],
        out_specs=pl.BlockSpec((tm, d), lambda i: (i, 0)),
        out_shape=jax.ShapeDtypeStruct((m, d), F32),
        compiler_params=_params(("parallel",)),
        name="out_proj",
    )(x, gate, oa, ob, oc, od, w_o, final_w.reshape(1, d))


def _head_rms_gate(o, g, nw):
    y = o * lax.rsqrt(jnp.mean(o * o, axis=-1, keepdims=True) + EPS) * nw
    return y * (g * _sigmoid(g))


def _rope128(x, cos, sin_signed):
    half = DK // 2
    up = pltpu.roll(x, LANE - half, axis=1)
    down = pltpu.roll(x, half, axis=1)
    first = (_iota(x.shape, 1) % DK) < half
    return x * cos + jnp.where(first, up, down) * sin_signed


def _decay_kernel(mode, C, tb, qk_ref, v_ref, g_ref, aux_ref, p1_ref, p2_ref, nw_ref, s0_ref,
                  o_ref, sout_ref, S_scr, q_scr, k_scr, b_scr, qe_scr, ke_scr, la_scr, oi_scr):
    HK = N_HEADS * DK
    tblk = pl.program_id(1)

    @pl.when(tblk == 0)
    def _():
        S_scr[...] = s0_ref[...]

    q = qk_ref[:, :HK]
    k = qk_ref[:, HK:]
    if mode == "gla":
        x = jnp.dot(aux_ref[:, LANE:2 * LANE].astype(BF16), p1_ref[...], preferred_element_type=F32) + p2_ref[...]
        la = -_softplus(-x) * (1.0 / GLA_TAU)
        q = q * (DK ** -0.5)
    else:
        cos = aux_ref[:, :HK]
        sin = aux_ref[:, HK:]
        q = jnp.concatenate([_rope128(q[:, i * LANE:(i + 1) * LANE], cos[:, i * LANE:(i + 1) * LANE],
                                      sin[:, i * LANE:(i + 1) * LANE]) for i in range(HK // LANE)], axis=1)
        k = jnp.concatenate([_rope128(k[:, i * LANE:(i + 1) * LANE], cos[:, i * LANE:(i + 1) * LANE],
                                      sin[:, i * LANE:(i + 1) * LANE]) for i in range(HK // LANE)], axis=1)
        k = k * (DK ** -0.5)
        la = jnp.broadcast_to(p2_ref[...], (tb, HK))

    row = _iota((tb, tb), 0)
    col = _iota((tb, tb), 1)
    same = (row // C) == (col // C)
    b = _mm_exact(jnp.where(same & (col <= row), 1.0, 0.0), la)
    btot = _mm_exact(jnp.where(same, 1.0, 0.0), la)
    q_scr[...] = q
    k_scr[...] = k
    b_scr[...] = b
    la_scr[...] = la
    qe_scr[...] = q * jnp.exp(b)
    ke_scr[...] = k * jnp.exp(btot - b)

    ind = jnp.where((_iota((HK, N_HEADS * DV), 0) // DK) == (_iota((HK, N_HEADS * DV), 1) // DV), 1.0, 0.0).astype(BF16)
    lane_head = _iota((C, HK), 1) // DK
    trow = _iota((C, HK), 0)
    ones_cv = jnp.ones((C, DV), F32)

    def chunk(c, carry):
        r0 = pl.multiple_of(c * C, C)
        qc = q_scr[pl.ds(r0, C), :]
        bc = b_scr[pl.ds(r0, C), :]
        xs = []
        for s in range(C):
            krow = k_scr[pl.ds(r0 + s, 1), :]
            brow = b_scr[pl.ds(r0 + s, 1), :]
            e = jnp.where(trow >= s, jnp.exp(jnp.minimum(bc - brow, 0.0)), 0.0)
            xs.append((qc * krow * e).astype(BF16))
        R = jnp.dot(jnp.concatenate(xs, axis=0), ind, preferred_element_type=F32)
        o_c = jnp.zeros((C, N_HEADS * DV), F32)
        for s in range(C):
            o_c = o_c + R[s * C:(s + 1) * C, :] * v_ref[pl.ds(r0 + s, 1), :]
        S = S_scr[...]
        qe = qe_scr[pl.ds(r0, C), :]
        L = jnp.concatenate([jnp.where(lane_head == h, qe, 0.0) for h in range(N_HEADS)], axis=0)
        inter = _mm(L, S)
        o_c = o_c + jnp.concatenate([inter[h * C:(h + 1) * C, :] for h in range(N_HEADS)], axis=1)
        oi_scr[pl.ds(r0, C), :] = o_c
        dS = _mm_tn(ke_scr[pl.ds(r0, C), :], v_ref[pl.ds(r0, C), :])
        gcol = jnp.exp(_mm_tn_exact(la_scr[pl.ds(r0, C), :], ones_cv))
        S_scr[...] = S * gcol + jnp.concatenate(
            [dS[h * DK:(h + 1) * DK, h * DV:(h + 1) * DV] for h in range(N_HEADS)], axis=0)
        return carry

    lax.fori_loop(0, tb // C, chunk, 0)

    for h in range(N_HEADS):
        sl = slice(h * DV, (h + 1) * DV)
        o_ref[:, sl] = _head_rms_gate(oi_scr[:, sl], g_ref[:, sl], nw_ref[...]).astype(o_ref.dtype)

    @pl.when(tblk == pl.num_programs(1) - 1)
    def _():
        sout_ref[...] = S_scr[...]


def decay_mixer(mode, cols, batch, seq, blk_qk, blk_v, blk_g, aux, aux_is_cols, p1, p2, norm_w, s0):
    C = min(CHUNK, seq)
    tb = min(256, seq)
    nt = seq // tb
    HK = N_HEADS * DK
    row_map = lambda bl: (lambda b, t: (b * nt + t, bl))
    if aux_is_cols:
        aux_spec = pl.BlockSpec((tb, GROUP_W), row_map(BLK_MISC))
    else:
        aux_spec = pl.BlockSpec((tb, aux.shape[1]), lambda b, t: (t, 0))
    full = lambda a: pl.BlockSpec(a.shape, lambda b, t: (0,) * a.ndim)
    return pl.pallas_call(
        functools.partial(_decay_kernel, mode, C, tb),
        grid=(batch, nt),
        in_specs=[pl.BlockSpec((tb, GROUP_W), row_map(blk_qk)),
                  pl.BlockSpec((tb, GROUP_W), row_map(blk_v)),
                  pl.BlockSpec((tb, GROUP_W), row_map(blk_g)),
                  aux_spec, full(p1), full(p2), full(norm_w),
                  pl.BlockSpec((None, HK, DV), lambda b, t: (b, 0, 0))],
        out_specs=[pl.BlockSpec((tb, GROUP_W), lambda b, t: (b * nt + t, 0)),
                   pl.BlockSpec((None, HK, DV), lambda b, t: (b, 0, 0))],
        out_shape=[jax.ShapeDtypeStruct((batch * seq, GROUP_W), BF16),
                   jax.ShapeDtypeStruct((batch, HK, DV), F32)],
        scratch_shapes=[pltpu.VMEM((HK, DV), F32)] + [pltpu.VMEM((tb, HK), F32)] * 6
                       + [pltpu.VMEM((tb, N_HEADS * DV), F32)],
        compiler_params=_params(("parallel", "arbitrary")),
        name="decay_" + mode,
    )(cols, cols, cols, aux, p1, p2, norm_w, s0)


def pad_in_cols(w):
    z = jnp.zeros(w.shape[:-1] + (COLS_W - 7312,), w.dtype)
    return jnp.concatenate([w[..., 0:1024], w[..., 1040:1552], w[..., 1552:5136], w[..., 5136:6672],
                            w[..., 6800:7312], w[..., 6672:6800], w[..., 1024:1040], z], axis=-1)


def gla_mixer(cols, batch, seq, lp, s0):
    HK = N_HEADS * DK
    w_a2 = jnp.zeros((LANE, HK), F32).at[:GLA_RANK].set(lp['gla_w_a2']).astype(BF16)
    b_a = lp['gla_b_a'].reshape(1, HK)
    return decay_mixer("gla", cols, batch, seq, BLK_GLA_QK, BLK_GLA_V, BLK_GLA_G, cols, True,
                       w_a2, b_a, lp['gla_norm'].reshape(1, DV), s0)


def ret_mixer(cols, batch, seq, lp, s0, pos):
    HK = N_HEADS * DK
    half = DK // 2
    inv = ROPE_BASE ** (-jnp.arange(half, dtype=F32) / half)
    ang = pos.astype(F32)[:, None] * inv[None, :]
    cos = jnp.tile(jnp.cos(ang), (1, 2 * N_HEADS))
    sin = jnp.tile(jnp.concatenate([-jnp.sin(ang), jnp.sin(ang)], axis=1), (1, N_HEADS))
    tables = jnp.concatenate([cos, sin], axis=1)
    log_gamma = jnp.log(1.0 - 2.0 ** (-5.0 - jnp.arange(N_HEADS, dtype=F32)))
    la = jnp.repeat(log_gamma, DK).reshape(1, HK)
    dummy = jnp.zeros((8, LANE), BF16)
    return decay_mixer("ret", cols, batch, seq, BLK_RET_QK, BLK_RET_V, BLK_RET_G, tables, False,
                       dummy, la, lp['ret_norm'].reshape(1, DV), s0)


def _t5_bucket(rel):
    n = jnp.maximum(rel, 0)
    max_exact = REL_BUCKETS // 2
    nf = jnp.maximum(n, 1).astype(F32)
    large = max_exact + (jnp.log(nf / max_exact) / math.log(REL_MAX_DIST / max_exact)
                         * (REL_BUCKETS - max_exact)).astype(jnp.int32)
    large = jnp.minimum(large, REL_BUCKETS - 1)
    return jnp.where(n < max_exact, n, large)


def _rel_bias_tile(rel_bias, rel):
    b = jnp.moveaxis(rel_bias.astype(F32)[_t5_bucket(rel)], -1, 0)
    return jnp.where(rel[None] >= 0, b, NEG_BIG)


def _lambda(lam_ref, lam_init):
    s1 = jnp.sum(lam_ref[0:1, :] * lam_ref[1:2, :], axis=-1, keepdims=True)
    s2 = jnp.sum(lam_ref[2:3, :] * lam_ref[3:4, :], axis=-1, keepdims=True)
    return jnp.exp(s1) - jnp.exp(s2) + lam_init


def _diff_finish(o1, o2, lam, lam_init, g, sw):
    o = o1 - lam * o2
    y = o * lax.rsqrt(jnp.mean(o * o, axis=-1, keepdims=True) + EPS) * sw * (1.0 - lam_init)
    return y * (g * _sigmoid(g))


def _diffattn_kernel(lam_init, tq, tk, q_ref, k_ref, v_ref, g_ref, bd_ref, bp_ref, far_ref, lam_ref, sw_ref,
                     o_ref, q_scr, m_scr, l_scr, acc_scr):
    qi = pl.program_id(2)
    q = q_ref[...] * (DK ** -0.5)
    lane = _iota(q.shape, 1)
    q_scr[0] = jnp.where(lane < DK, q, 0.0).astype(BF16)
    q_scr[1] = jnp.where(lane >= DK, q, 0.0).astype(BF16)
    m_scr[...] = jnp.full(m_scr.shape, NEG_BIG, F32)
    l_scr[...] = jnp.zeros(l_scr.shape, F32)
    acc_scr[...] = jnp.zeros(acc_scr.shape, F32)

    def step(j, bias):
        r0 = pl.multiple_of(j * tk, tk)
        kb = k_ref[pl.ds(r0, tk), :].astype(BF16)
        vb = v_ref[pl.ds(r0, tk), :].astype(BF16)
        for mp in range(2):
            s = lax.dot_general(q_scr[mp], kb, (((1,), (1,)), ((), ())), preferred_element_type=F32) + bias
            m_old = m_scr[mp]
            m_new = jnp.maximum(m_old, jnp.max(s, axis=-1, keepdims=True))
            p = jnp.exp(s - m_new)
            alpha = jnp.exp(m_old - m_new)
            l_scr[mp] = alpha * l_scr[mp] + jnp.sum(p, axis=-1, keepdims=True)
            acc_scr[mp] = alpha * acc_scr[mp] + jnp.dot(p.astype(BF16), vb, preferred_element_type=F32)
            m_scr[mp] = m_new

    def far_body(j, carry):
        step(j, far_ref[:, 0:1])
        return carry

    lax.fori_loop(0, jnp.maximum(qi - 1, 0), far_body, 0)

    @pl.when(qi >= 1)
    def _():
        step(qi - 1, bp_ref[...])

    step(qi, bd_ref[...])
    lam = _lambda(lam_ref, lam_init)
    o_ref[...] = _diff_finish(acc_scr[0] / l_scr[0], acc_scr[1] / l_scr[1], lam, lam_init,
                              g_ref[...], sw_ref[...]).astype(o_ref.dtype)


def diff_attn_prompt(cols, batch, seq, lp, rel_bias, lam_init):
    tq = tk = min(256, seq)
    nq = seq // tq
    d = _iota((tq, tk), 0) - _iota((tq, tk), 1)
    bias_diag = _rel_bias_tile(rel_bias, d)
    bias_prev = _rel_bias_tile(rel_bias, d + tk)
    far = jnp.broadcast_to(rel_bias.astype(F32)[REL_BUCKETS - 1][:, None, None], (N_HEADS, 1, LANE))
    lam4 = jnp.stack([lp['diff_lam_q1'], lp['diff_lam_k1'], lp['diff_lam_q2'], lp['diff_lam_k2']]).astype(F32)
    hb = GROUP_W // LANE
    head_tile = lambda shape: pl.BlockSpec((None,) + shape, lambda b, h, i: (h, 0, 0))
    return pl.pallas_call(
        functools.partial(_diffattn_kernel, lam_init, tq, tk),
        grid=(batch, N_HEADS, nq),
        in_specs=[pl.BlockSpec((tq, LANE), lambda b, h, i: (b * nq + i, BLK_DIFF_Q * hb + h)),
                  pl.BlockSpec((seq, LANE), lambda b, h, i: (b, BLK_DIFF_K * hb + h)),
                  pl.BlockSpec((seq, LANE), lambda b, h, i: (b, BLK_DIFF_V * hb + h)),
                  pl.BlockSpec((tq, LANE), lambda b, h, i: (b * nq + i, BLK_DIFF_G * hb + h)),
                  head_tile((tq, tk)), head_tile((tq, tk)), head_tile((1, LANE)),
                  pl.BlockSpec((4, DK), lambda b, h, i: (0, 0)),
                  pl.BlockSpec((1, DV), lambda b, h, i: (0, 0))],
        out_specs=pl.BlockSpec((tq, LANE), lambda b, h, i: (b * nq + i, h)),
        out_shape=jax.ShapeDtypeStruct((batch * seq, GROUP_W), BF16),
        scratch_shapes=[pltpu.VMEM((2, tq, LANE), BF16), pltpu.VMEM((2, tq, 1), F32),
                        pltpu.VMEM((2, tq, 1), F32), pltpu.VMEM((2, tq, DV), F32)],
        compiler_params=_params(("parallel", "parallel", "arbitrary")),
        name="diff_attn_prompt",
    )(cols, cols, cols, cols, bias_diag, bias_prev, far, lam4, lp['diff_subln'].reshape(1, DV))


def _diffdec_kernel(lam_init, PP, ts, page, pt_ref, q_ref, kn_ref, vn_ref, g_ref, *rest):
    k_refs = rest[:PP]
    v_refs = rest[PP:2 * PP]
    bias_ref, biasn_ref, lam_ref, sw_ref, o_ref, q_scr, m_scr, l_scr, acc_scr = rest[2 * PP:]
    step_id = pl.program_id(1)
    W = N_HEADS * DV

    @pl.when(step_id == 0)
    def _():
        q = q_ref[...] * (DK ** -0.5)
        lane = _iota(q.shape, 1)
        pieces = []
        for h in range(N_HEADS):
            for mp in range(2):
                lo = h * DV + mp * DK
                pieces.append(jnp.where((lane >= lo) & (lane < lo + DK), q, 0.0))
        q_scr[...] = jnp.concatenate(pieces, axis=0).astype(BF16)
        m_scr[...] = jnp.full(m_scr.shape, NEG_BIG, F32)
        l_scr[...] = jnp.zeros(l_scr.shape, F32)
        acc_scr[...] = jnp.zeros(acc_scr.shape, F32)

    def update(kb, vb, bias):
        s = lax.dot_general(q_scr[...], kb, (((1,), (1,)), ((), ())), preferred_element_type=F32) + bias
        m_old = m_scr[...]
        m_new = jnp.maximum(m_old, jnp.max(s, axis=-1, keepdims=True))
        p = jnp.exp(s - m_new)
        alpha = jnp.exp(m_old - m_new)
        l_scr[...] = alpha * l_scr[...] + jnp.sum(p, axis=-1, keepdims=True)
        acc_scr[...] = alpha * acc_scr[...] + jnp.dot(p.astype(BF16), vb, preferred_element_type=F32)
        m_scr[...] = m_new

    update(jnp.concatenate([r[...].astype(BF16) for r in k_refs], axis=0),
           jnp.concatenate([r[...].astype(BF16) for r in v_refs], axis=0), bias_ref[...])

    @pl.when(step_id == pl.num_programs(1) - 1)
    def _():
        pad = jnp.zeros((page - ts, W), BF16)
        update(jnp.concatenate([kn_ref[...].astype(BF16), pad], axis=0),
               jnp.concatenate([vn_ref[...].astype(BF16), pad], axis=0), biasn_ref[...])
        lam = _lambda(lam_ref, lam_init)
        rows = 2 * ts
        for h in range(N_HEADS):
            a = acc_scr[h * rows:(h + 1) * rows, h * DV:(h + 1) * DV]
            ln = l_scr[h * rows:(h + 1) * rows, :]
            o_ref[:, h * DV:(h + 1) * DV] = _diff_finish(
                a[:ts] / ln[:ts], a[ts:] / ln[ts:], lam, lam_init,
                g_ref[:, h * DV:(h + 1) * DV], sw_ref[...]).astype(o_ref.dtype)


def diff_attn_sample(cols, batch, ts, cache_k, cache_v, layer, page_table, lp, rel_bias, lam_init, pages_per_step=8):
    _, n_pool, page, W = cache_k.shape
    n_pages = page_table.shape[1]
    PP = pages_per_step
    n_steps = n_pages // PP
    past = n_pages * page
    nk = PP * page
    rows = N_HEADS * 2 * ts
    t_of_row = jnp.tile(jnp.arange(ts, dtype=jnp.int32), N_HEADS * 2)
    h_of_row = jnp.repeat(jnp.arange(N_HEADS, dtype=jnp.int32), 2 * ts)

    def rows_bias(k_pos):
        rel = past + t_of_row[:, None] - k_pos[None, :]
        tile = _rel_bias_tile(rel_bias, rel)
        return jnp.take_along_axis(tile, h_of_row[None, :, None], axis=0)[0]

    kk = jnp.arange(nk, dtype=jnp.int32)
    bias_steps = jnp.stack([rows_bias(kk), rows_bias((n_steps - 1) * nk + kk)])
    kn = jnp.arange(page, dtype=jnp.int32)
    bias_new = jnp.where(kn[None, :] < ts, rows_bias(past + kn), NEG_BIG)
    lam4 = jnp.stack([lp['diff_lam_q1'], lp['diff_lam_k1'], lp['diff_lam_q2'], lp['diff_lam_k2']]).astype(F32)

    def page_spec(i):
        return pl.BlockSpec((None, None, page, W), lambda b, s, pt: (layer, pt[b * n_pages + s * PP + i], 0, 0))

    col_spec = lambda blk: pl.BlockSpec((ts, GROUP_W), lambda b, s, pt: (b, blk))
    const = lambda a: pl.BlockSpec(a.shape, lambda b, s, pt: (0,) * a.ndim)
    grid_spec = pltpu.PrefetchScalarGridSpec(
        num_scalar_prefetch=1,
        grid=(batch, n_steps),
        in_specs=[col_spec(BLK_DIFF_Q), col_spec(BLK_DIFF_K), col_spec(BLK_DIFF_V), col_spec(BLK_DIFF_G)]
                 + [page_spec(i) for i in range(PP)] + [page_spec(i) for i in range(PP)]
                 + [pl.BlockSpec((None, rows, nk), lambda b, s, pt: (jnp.where(s == n_steps - 1, 1, 0), 0, 0)),
                    const(bias_new), const(lam4), pl.BlockSpec((1, DV), lambda b, s, pt: (0, 0))],
        out_specs=pl.BlockSpec((ts, GROUP_W), lambda b, s, pt: (b, 0)),
        scratch_shapes=[pltpu.VMEM((rows, W), BF16), pltpu.VMEM((rows, 1), F32),
                        pltpu.VMEM((rows, 1), F32), pltpu.VMEM((rows, W), F32)])
    return pl.pallas_call(
        functools.partial(_diffdec_kernel, lam_init, PP, ts, page),
        grid_spec=grid_spec,
        out_shape=jax.ShapeDtypeStruct((batch * ts, GROUP_W), BF16),
        compiler_params=_params(("parallel", "arbitrary")),
        name="diff_attn_sample",
    )(page_table.reshape(-1), cols, cols, cols, cols, *([cache_k] * PP), *([cache_v] * PP),
      bias_steps, bias_new, lam4, lp['diff_subln'].reshape(1, DV))


RWKV_PAIRS = RWKV_HEADS // 2
SHIFT_PAD = 4 * GROUP_W


def _rwkv_kernel(C, tb, r_ref, k_ref, v_ref, misc_ref, g_ref, shift_ref, s0_ref,
                 mu_ref, w0_ref, ww2_ref, a0_ref, aw2_ref, kk_ref, ka_ref, rk_ref, nw_ref,
                 o_ref, sout_ref, S_scr, carry_scr, rt_scr, kt_scr, kd_scr, bd_scr, kh_scr, bh_scr, v_scr,
                 lw_scr, oi_scr):
    G = GROUP_W
    tblk = pl.program_id(1)

    @pl.when(tblk == 0)
    def _():
        S_scr[...] = s0_ref[...]
        carry_scr[...] = shift_ref[...]

    first_row = _iota((tb, 1), 0) == 0

    def token_shift(p, lo, width):
        prev = jnp.where(first_row, carry_scr[:, lo:lo + width], pltpu.roll(p, 1, axis=0))
        carry_scr[:, lo:lo + width] = p[tb - 1:tb, :]
        return p + mu_ref[:, lo:lo + width] * (prev - p)

    r = token_shift(r_ref[...], 0, G)
    k = token_shift(k_ref[...], G, G)
    v = token_shift(v_ref[...], 2 * G, G)
    lora = token_shift(misc_ref[:, :LANE], 3 * G, LANE)

    w_pre = w0_ref[...] + jnp.dot(jnp.tanh(lora).astype(BF16), ww2_ref[...], preferred_element_type=F32)
    lw = -jnp.exp(-_softplus(-w_pre) - 0.5)
    a = _sigmoid(a0_ref[...] + jnp.dot(lora.astype(BF16), aw2_ref[...], preferred_element_type=F32))

    head_sum = jnp.where((_iota((G, G), 0) // RWKV_N) == (_iota((G, G), 1) // RWKV_N), 1.0, 0.0)
    kk = k * kk_ref[...]
    kk = kk / jnp.maximum(jnp.sqrt(_mm_exact(kk * kk, head_sum)), 1e-12)
    k2 = k * (1.0 + (a - 1.0) * ka_ref[...])
    bb = kk * a
    bonus = _mm_exact(r * k2 * rk_ref[...], head_sum) * v

    row = _iota((tb, tb), 0)
    col = _iota((tb, tb), 1)
    same = (row // C) == (col // C)
    cum = _mm_exact(jnp.where(same & (col <= row), 1.0, 0.0), lw)
    tot = _mm_exact(jnp.where(same, 1.0, 0.0), lw)
    inv = jnp.exp(-cum)
    rest = jnp.exp(tot - cum)
    rt_scr[...] = r * jnp.exp(cum)
    kt_scr[...] = kk * jnp.exp(cum - lw)
    kd_scr[...] = k2 * inv
    bd_scr[...] = bb * inv
    kh_scr[...] = k2 * rest
    bh_scr[...] = bb * rest
    v_scr[...] = v
    lw_scr[...] = lw

    C2 = 2 * C
    lane0 = _iota((C, LANE), 1) < RWKV_N
    rr = _iota((C2, C2), 0)
    cc = _iota((C2, C2), 1)
    strict = (rr % C) > (cc % C)
    incl = (rr % C) >= (cc % C)
    eye = jnp.where(rr == cc, 1.0, 0.0)
    blockdiag = (_iota((LANE, LANE), 0) // RWKV_N) == (_iota((LANE, LANE), 1) // RWKV_N)
    ones_c = jnp.ones((C, LANE), F32)
    n_double = max(int(math.log2(C)) - 1, 0)

    def split(x):
        return jnp.concatenate([jnp.where(lane0, x, 0.0), jnp.where(lane0, 0.0, x)], axis=0)

    def chunk(c, carry):
        r0 = pl.multiple_of(c * C, C)
        rows = pl.ds(r0, C)
        for p in range(RWKV_PAIRS):
            sl = slice(p * LANE, (p + 1) * LANE)
            Lk = split(kt_scr[rows, sl])
            Lr = split(rt_scr[rows, sl])
            Rk = split(kd_scr[rows, sl])
            Rb = split(bd_scr[rows, sl])
            vc = v_scr[rows, sl]
            Vb = split(vc)
            A_kk = jnp.where(strict, _mm_nt(Lk, Rk), 0.0)
            A_kb = jnp.where(strict, _mm_nt(Lk, Rb), 0.0)
            A_rk = jnp.where(incl, _mm_nt(Lr, Rk), 0.0)
            A_rb = jnp.where(incl, _mm_nt(Lr, Rb), 0.0)
            X = -A_kb
            T = eye + X
            for _ in range(n_double):
                X = _mm(X, X)
                T = T + _mm(T, X)
            S = S_scr[p]
            U = _mm(T, _mm(Lk, S) + _mm(A_kk, Vb))
            O = _mm(Lr, S) + _mm(A_rk, Vb) - _mm(A_rb, U)
            oi_scr[rows, sl] = O[:C] + O[C:]
            Up = U[:C] + U[C:]
            dS = _mm_tn(kh_scr[rows, sl], vc) - _mm_tn(bh_scr[rows, sl], Up)
            gcol = jnp.exp(_mm_tn_exact(lw_scr[rows, sl], ones_c))
            S_scr[p] = S * gcol + jnp.where(blockdiag, dS, 0.0)
        return carry

    lax.fori_loop(0, tb // C, chunk, 0)

    o = oi_scr[...]
    ms = _mm_exact(o * o, head_sum) * (1.0 / RWKV_N)
    y = o * lax.rsqrt(ms + EPS) * nw_ref[...] + bonus
    g = g_ref[...]
    o_ref[...] = (y * (g * _sigmoid(g))).astype(o_ref.dtype)

    @pl.when(tblk == pl.num_programs(1) - 1)
    def _():
        sout_ref[...] = S_scr[...]


def rwkv_mixer(cols, batch, seq, lp, s0_pairs, shift_pad):
    C = min(64, seq)
    tb = min(256, seq)
    nt = seq // tb
    G = GROUP_W
    mu = lp['rwkv_mu']
    mu_pad = jnp.concatenate([mu, jnp.zeros((SHIFT_PAD - mu.shape[0],), F32)]).reshape(1, SHIFT_PAD)
    ww2 = jnp.zeros((LANE, G), F32).at[:RWKV_LORA].set(lp['rwkv_w_w2']).astype(BF16)
    aw2 = jnp.zeros((LANE, G), F32).at[RWKV_LORA:].set(lp['rwkv_a_w2']).astype(BF16)
    vec = lambda a: a.reshape(1, G).astype(F32)
    params = [mu_pad, vec(lp['rwkv_w0']), ww2, vec(lp['rwkv_a0']), aw2, vec(lp['rwkv_k_k']), vec(lp['rwkv_k_a']),
              vec(lp['rwkv_r_k']), jnp.tile(lp['rwkv_norm'], RWKV_HEADS).reshape(1, G)]
    row_map = lambda bl: (lambda b, t: (b * nt + t, bl))
    full = lambda a: pl.BlockSpec(a.shape, lambda b, t: (0,) * a.ndim)
    slab = pltpu.VMEM((tb, G), F32)
    return pl.pallas_call(
        functools.partial(_rwkv_kernel, C, tb),
        grid=(batch, nt),
        in_specs=[pl.BlockSpec((tb, G), row_map(BLK_RWKV_R)), pl.BlockSpec((tb, G), row_map(BLK_RWKV_K)),
                  pl.BlockSpec((tb, G), row_map(BLK_RWKV_V)), pl.BlockSpec((tb, G), row_map(BLK_MISC)),
                  pl.BlockSpec((tb, G), row_map(BLK_RWKV_G)),
                  pl.BlockSpec((None, 1, SHIFT_PAD), lambda b, t: (b, 0, 0)),
                  pl.BlockSpec((None, RWKV_PAIRS, LANE, LANE), lambda b, t: (b, 0, 0, 0))]
                 + [full(a) for a in params],
        out_specs=[pl.BlockSpec((tb, G), lambda b, t: (b * nt + t, 0)),
                   pl.BlockSpec((None, RWKV_PAIRS, LANE, LANE), lambda b, t: (b, 0, 0, 0))],
        out_shape=[jax.ShapeDtypeStruct((batch * seq, G), BF16),
                   jax.ShapeDtypeStruct((batch, RWKV_PAIRS, LANE, LANE), F32)],
        scratch_shapes=[pltpu.VMEM((RWKV_PAIRS, LANE, LANE), F32), pltpu.VMEM((1, SHIFT_PAD), F32)] + [slab] * 9,
        compiler_params=_params(("parallel", "arbitrary")),
        name="rwkv7",
    )(cols, cols, cols, cols, cols, shift_pad, s0_pairs, *params)


def rwkv_state_to_pairs(s):
    b = s.shape[0]
    st = jnp.swapaxes(s, -1, -2).reshape(b, RWKV_PAIRS, 2, RWKV_N, RWKV_N)
    z = jnp.zeros_like(st[:, :, 0])
    top = jnp.concatenate([st[:, :, 0], z], axis=-1)
    bot = jnp.concatenate([z, st[:, :, 1]], axis=-1)
    return jnp.concatenate([top, bot], axis=-2)


def rwkv_pairs_to_state(sp):
    b = sp.shape[0]
    h0 = sp[:, :, :RWKV_N, :RWKV_N]
    h1 = sp[:, :, RWKV_N:, RWKV_N:]
    st = jnp.stack([h0, h1], axis=2).reshape(b, RWKV_HEADS, RWKV_N, RWKV_N)
    return jnp.swapaxes(st, -1, -2)


def _group_layer(x, scale, shift, gate, rows_per_mod, batch, seq, lp, w_pad, w_o_bf, final_w, final,
                 s_gla, s_ret, s_rwkv, s_shift, pos, diff_fn, tm_in, tm_out):
    cols = in_proj(x, scale, shift, lp['norm_w'], w_pad, rows_per_mod, tm_in, 768)
    oa, n_gla = gla_mixer(cols, batch, seq, lp, s_gla)
    ob = diff_fn(cols)
    oc, n_ret = ret_mixer(cols, batch, seq, lp, s_ret, pos)
    od, n_rwkv = rwkv_mixer(cols, batch, seq, lp, s_rwkv, s_shift)
    y = out_proj(x, gate, oa, ob, oc, od, w_o_bf, final_w, rows_per_mod, tm_out, final)
    c3 = cols.reshape(batch, seq, COLS_W)
    k_new = c3[:, :, BLK_DIFF_K * GROUP_W:(BLK_DIFF_K + 1) * GROUP_W].reshape(batch, seq, N_HEADS, DV)
    v_new = c3[:, :, BLK_DIFF_V * GROUP_W:(BLK_DIFF_V + 1) * GROUP_W].reshape(batch, seq, N_HEADS, DV)
    last = c3[:, -1]
    n_shift = jnp.concatenate([last[:, BLK_RWKV_R * GROUP_W:(BLK_RWKV_V + 1) * GROUP_W],
                               last[:, BLK_MISC * GROUP_W:BLK_MISC * GROUP_W + 2 * RWKV_LORA]], axis=-1)
    return y, (k_new, v_new, n_gla.reshape(batch, N_HEADS, DK, DV), n_ret.reshape(batch, N_HEADS, DK, DV),
               rwkv_pairs_to_state(n_rwkv), n_shift)


def kernel(x_prompt, x_sample, cache_k, cache_v, state_gla, state_ret, state_rwkv, state_shift, page_table,
           c_prompt, c_sample, rel_bias, w_ada, b_ada, norm_w, w_in, w_o, gla_w_a2, gla_b_a, gla_norm,
           diff_lam_q1, diff_lam_k1, diff_lam_q2, diff_lam_k2, diff_subln, ret_norm, rwkv_mu, rwkv_w0, rwkv_w_w2,
           rwkv_a0, rwkv_a_w2, rwkv_k_k, rwkv_k_a, rwkv_r_k, rwkv_norm, final_norm):
    Bp, Sp, D = x_prompt.shape
    Bs, Ts, _ = x_sample.shape
    depth = w_in.shape[0]
    n_pool, page = cache_k.shape[1], cache_k.shape[2]
    past_len = page_table.shape[1] * page
    ck = cache_k.reshape(depth, n_pool, page, N_HEADS * DV)
    cv = cache_v.reshape(depth, n_pool, page, N_HEADS * DV)
    pos_p = jnp.arange(Sp, dtype=jnp.int32)
    pos_s = past_len + jnp.arange(Ts, dtype=jnp.int32)
    c_all = jnp.concatenate([c_prompt, c_sample], axis=0)
    xp = x_prompt.reshape(Bp * Sp, D)
    xs = x_sample.reshape(Bs * Ts, D)
    zeros_dec = jnp.zeros((Bp, N_HEADS * DK, DV), F32)
    zeros_rwkv = jnp.zeros((Bp, RWKV_PAIRS, LANE, LANE), F32)
    zeros_shift = jnp.zeros((Bp, 1, SHIFT_PAD), F32)
    new_p, new_s = [], []
    for l in range(depth):
        lp = dict(norm_w=norm_w[l], gla_w_a2=gla_w_a2[l], gla_b_a=gla_b_a[l], gla_norm=gla_norm[l],
                  diff_lam_q1=diff_lam_q1[l], diff_lam_k1=diff_lam_k1[l], diff_lam_q2=diff_lam_q2[l],
                  diff_lam_k2=diff_lam_k2[l], diff_subln=diff_subln[l], ret_norm=ret_norm[l],
                  rwkv_mu=rwkv_mu[l], rwkv_w0=rwkv_w0[l], rwkv_w_w2=rwkv_w_w2[l], rwkv_a0=rwkv_a0[l],
                  rwkv_a_w2=rwkv_a_w2[l], rwkv_k_k=rwkv_k_k[l], rwkv_k_a=rwkv_k_a[l], rwkv_r_k=rwkv_r_k[l],
                  rwkv_norm=rwkv_norm[l])
        final = l == depth - 1
        lam_init = 0.8 - 0.6 * math.exp(-0.3 * l)
        mod = ada_mod(c_all, w_ada[l], b_ada[l])
        shift, scale, gate = mod[:, :D], mod[:, D:2 * D], mod[:, 2 * D:]
        w_pad = pad_in_cols(w_in[l]).astype(BF16)
        w_o_bf = w_o[l].astype(BF16)

        per_batch = lambda a: a[:Bp].reshape(Bp, 1, D)
        xp, st_p = _group_layer(
            xp, per_batch(scale), per_batch(shift), per_batch(gate), Sp, Bp, Sp, lp, w_pad, w_o_bf, final_norm, final,
            zeros_dec, zeros_dec, zeros_rwkv, zeros_shift, pos_p,
            lambda cols: diff_attn_prompt(cols, Bp, Sp, lp, rel_bias, lam_init), min(1024, Sp), min(256, Sp))

        per_row = lambda a: jnp.repeat(a[Bp:], Ts, axis=0)
        sh = state_shift[l]
        sh_pad = jnp.concatenate([sh, jnp.zeros((Bs, SHIFT_PAD - sh.shape[1]), F32)], axis=1).reshape(Bs, 1, SHIFT_PAD)
        xs, st_s = _group_layer(
            xs, per_row(scale), per_row(shift), per_row(gate), 1, Bs, Ts, lp, w_pad, w_o_bf, final_norm, final,
            state_gla[l].reshape(Bs, N_HEADS * DK, DV), state_ret[l].reshape(Bs, N_HEADS * DK, DV),
            rwkv_state_to_pairs(state_rwkv[l]), sh_pad, pos_s,
            lambda cols: diff_attn_sample(cols, Bs, Ts, ck, cv, l, page_table, lp, rel_bias, lam_init),
            Bs * Ts, Bs * Ts)
        new_p.append(st_p)
        new_s.append(st_s)
    stack = lambda states, i: jnp.stack([s[i] for s in states])
    return (xp.reshape(Bp, Sp, D), xs.reshape(Bs, Ts, D),
            stack(new_p, 0), stack(new_p, 1), stack(new_s, 0), stack(new_s, 1),
            stack(new_p, 2), stack(new_s, 2), stack(new_p, 3), stack(new_s, 3),
            stack(new_p, 4), stack(new_s, 4), stack(new_p, 5), stack(new_s, 5))
```

```python
import functools
import math

import jax
import jax.numpy as jnp
from jax import lax
from jax.experimental import pallas as pl
from jax.experimental.pallas import tpu as pltpu

F32 = jnp.float32
BF16 = jnp.bfloat16
HIGHEST = lax.Precision.HIGHEST

D_MODEL = 2048
GROUP_W = 512
N_HEADS = 4
DK = 64
DV = 128
GLA_RANK = 16
GLA_TAU = 16.0
RWKV_N = 64
RWKV_HEADS = 8
RWKV_LORA = 64
ROPE_BASE = 10000.0
REL_BUCKETS = 32
REL_MAX_DIST = 128
CHUNK = 16
EPS = 1e-6
NEG_BIG = -1e30

LANE = 128
VMEM_LIMIT = 48 * 1024 * 1024
ATTN_BLOCK = 256
DECODE_PAGES = 16
DECODE_CHAINS = 2
SEQ_BLOCK = 256
IN_TM, IN_TN = 1024, 768
OUT_TM = 256

COLS_W = 15 * GROUP_W
BLK_GLA_QK, BLK_GLA_V, BLK_GLA_G = 0, 1, 2
BLK_DIFF_Q, BLK_DIFF_K, BLK_DIFF_V, BLK_DIFF_G = 3, 4, 5, 6
BLK_RET_QK, BLK_RET_V, BLK_RET_G = 7, 8, 9
BLK_RWKV_R, BLK_RWKV_K, BLK_RWKV_V, BLK_RWKV_G = 10, 11, 12, 13
BLK_MISC = 14


def _mm(a, b):
    return jnp.dot(a.astype(BF16), b.astype(BF16), preferred_element_type=F32)


def _mm_nt(a, b):
    return lax.dot_general(a.astype(BF16), b.astype(BF16), (((1,), (1,)), ((), ())), preferred_element_type=F32)


def _mm_tn(a, b):
    return lax.dot_general(a.astype(BF16), b.astype(BF16), (((0,), (0,)), ((), ())), preferred_element_type=F32)


def _mm_exact(a, b):
    return jnp.dot(a, b, precision=HIGHEST, preferred_element_type=F32)


def _mm_tn_exact(a, b):
    return lax.dot_general(a, b, (((0,), (0,)), ((), ())), precision=HIGHEST, preferred_element_type=F32)


def _split2(x):
    hi = x.astype(BF16)
    return hi, (x - hi.astype(F32)).astype(BF16)


def _mm_split(a, x):
    hi, lo = _split2(x)
    return jnp.dot(a, hi, preferred_element_type=F32) + jnp.dot(a, lo, preferred_element_type=F32)


def _head_sums(x):
    ind = jnp.where((_iota((LANE, LANE), 0) // RWKV_N) == (_iota((LANE, LANE), 1) // RWKV_N), 1.0, 0.0).astype(BF16)
    hi, lo = _split2(x)
    parts = []
    for i in range(x.shape[1] // LANE):
        sl = slice(i * LANE, (i + 1) * LANE)
        parts.append(jnp.dot(hi[:, sl], ind, preferred_element_type=F32) + jnp.dot(lo[:, sl], ind, preferred_element_type=F32))
    return jnp.concatenate(parts, axis=1)


def _sigmoid(x):
    return 1.0 / (1.0 + jnp.exp(-x))


def _softplus(x):
    return jnp.maximum(x, 0.0) + jnp.log1p(jnp.exp(-jnp.abs(x)))


def _iota(shape, dim):
    return lax.broadcasted_iota(jnp.int32, shape, dim)


def _params(sem):
    return pltpu.CompilerParams(dimension_semantics=sem, vmem_limit_bytes=VMEM_LIMIT)


def _ada_kernel(c_ref, w_ref, b_ref, o_ref):
    c = c_ref[...]
    o_ref[...] = _mm(c * _sigmoid(c), w_ref[...]) + b_ref[...]


def ada_mod(c, w_ada, b_ada, tn=512):
    rows, d = c.shape
    n = w_ada.shape[1]
    return pl.pallas_call(
        _ada_kernel,
        grid=(n // tn,),
        in_specs=[pl.BlockSpec((rows, d), lambda j: (0, 0)),
                  pl.BlockSpec((d, tn), lambda j: (0, j)),
                  pl.BlockSpec((1, tn), lambda j: (0, j))],
        out_specs=pl.BlockSpec((rows, tn), lambda j: (0, j)),
        out_shape=jax.ShapeDtypeStruct((rows, n), F32),
        compiler_params=_params(("arbitrary",)),
        name="ada_mod",
    )(c, w_ada, b_ada.reshape(1, n))


def _inproj_kernel(x_ref, scale_ref, shift_ref, nw_ref, w_ref, o_ref, h_ref):
    @pl.when(pl.program_id(1) == 0)
    def _():
        x = x_ref[...]
        y = x * lax.rsqrt(jnp.mean(x * x, axis=-1, keepdims=True) + EPS) * nw_ref[...]
        h_ref[...] = (y * (1.0 + scale_ref[...]) + shift_ref[...]).astype(BF16)

    o_ref[...] = jnp.dot(h_ref[...], w_ref[...], preferred_element_type=F32)


def in_proj(x, scale, shift, norm_w, w_pad, rows_per_mod, tm, tn):
    m, d = x.shape
    n = w_pad.shape[1]
    if rows_per_mod == 1:
        mod_spec = pl.BlockSpec((tm, d), lambda i, j: (i, 0))
    else:
        per = rows_per_mod // tm
        mod_spec = pl.BlockSpec((None, 1, d), lambda i, j: (i // per, 0, 0))
    return pl.pallas_call(
        _inproj_kernel,
        grid=(m // tm, n // tn),
        in_specs=[pl.BlockSpec((tm, d), lambda i, j: (i, 0)),
                  mod_spec, mod_spec,
                  pl.BlockSpec((1, d), lambda i, j: (0, 0)),
                  pl.BlockSpec((d, tn), lambda i, j: (0, j))],
        out_specs=pl.BlockSpec((tm, tn), lambda i, j: (i, j)),
        out_shape=jax.ShapeDtypeStruct((m, n), F32),
        scratch_shapes=[pltpu.VMEM((tm, d), BF16)],
        compiler_params=_params(("parallel", "arbitrary")),
        name="in_proj",
    )(x, scale, shift, norm_w.reshape(1, d), w_pad)


def _outproj_kernel(final, x_ref, gate_ref, oa_ref, ob_ref, oc_ref, od_ref, w_ref, fw_ref, y_ref):
    acc = jnp.dot(oa_ref[...], w_ref[0 * GROUP_W:1 * GROUP_W, :], preferred_element_type=F32)
    acc += jnp.dot(ob_ref[...], w_ref[1 * GROUP_W:2 * GROUP_W, :], preferred_element_type=F32)
    acc += jnp.dot(oc_ref[...], w_ref[2 * GROUP_W:3 * GROUP_W, :], preferred_element_type=F32)
    acc += jnp.dot(od_ref[...], w_ref[3 * GROUP_W:4 * GROUP_W, :], preferred_element_type=F32)
    y = x_ref[...] + gate_ref[...] * acc
    if final:
        y = y * lax.rsqrt(jnp.mean(y * y, axis=-1, keepdims=True) + EPS) * fw_ref[...]
    y_ref[...] = y


def out_proj(x, gate, oa, ob, oc, od, w_o, final_w, rows_per_mod, tm, final):
    m, d = x.shape
    if rows_per_mod == 1:
        mod_spec = pl.BlockSpec((tm, d), lambda i: (i, 0))
    else:
        per = rows_per_mod // tm
        mod_spec = pl.BlockSpec((None, 1, d), lambda i: (i // per, 0, 0))
    o_spec = pl.BlockSpec((tm, GROUP_W), lambda i: (i, 0))
    return pl.pallas_call(
        functools.partial(_outproj_kernel, final),
        grid=(m // tm,),
        in_specs=[pl.BlockSpec((tm, d), lambda i: (i, 0)), mod_spec,
                  o_spec, o_spec, o_spec, o_spec,
                  pl.BlockSpec((d, d), lambda i: (0, 0)),
                  pl.BlockSpec((1, d), lambda i: (0, 0))],
        out_specs=pl.BlockSpec((tm, d), lambda i: (i, 0)),
        out_shape=jax.ShapeDtypeStruct((m, d), F32),
        compiler_params=_params(("parallel",)),
        name="out_proj",
    )(x, gate, oa, ob, oc, od, w_o, final_w.reshape(1, d))


def _head_rms_gate(o, g, nw):
    y = o * lax.rsqrt(jnp.mean(o * o, axis=-1, keepdims=True) + EPS) * nw
    return y * (g * _sigmoid(g))


def _rope128(x, cos, sin_signed):
    half = DK // 2
    up = pltpu.roll(x, LANE - half, axis=1)
    down = pltpu.roll(x, half, axis=1)
    first = (_iota(x.shape, 1) % DK) < half
    return x * cos + jnp.where(first, up, down) * sin_signed


def _decay_kernel(mode, C, tb, qk_ref, v_ref, g_ref, aux_ref, p1_ref, p2_ref, nw_ref, s0_ref,
                  o_ref, sout_ref, S_scr, k_scr, b_scr):
    HK = N_HEADS * DK
    tblk = pl.program_id(1)

    @pl.when(tblk == 0)
    def _():
        S_scr[...] = s0_ref[...]

    q = qk_ref[:, :HK]
    k = qk_ref[:, HK:]
    if mode == "gla":
        x = jnp.dot(aux_ref[:, LANE:2 * LANE].astype(BF16), p1_ref[...], preferred_element_type=F32) + p2_ref[...]
        la = -_softplus(-x) * (1.0 / GLA_TAU)
        q = q * (DK ** -0.5)
    else:
        cos = aux_ref[:, :HK]
        sin = aux_ref[:, HK:]
        q = jnp.concatenate([_rope128(q[:, i * LANE:(i + 1) * LANE], cos[:, i * LANE:(i + 1) * LANE],
                                      sin[:, i * LANE:(i + 1) * LANE]) for i in range(HK // LANE)], axis=1)
        k = jnp.concatenate([_rope128(k[:, i * LANE:(i + 1) * LANE], cos[:, i * LANE:(i + 1) * LANE],
                                      sin[:, i * LANE:(i + 1) * LANE]) for i in range(HK // LANE)], axis=1)
        k = k * (DK ** -0.5)

    nc = tb // C
    lane_head = _iota((C, HK), 1) // DK
    heads_rows = lambda x: jnp.concatenate([jnp.where(lane_head == h, x, 0.0) for h in range(N_HEADS)],
                                           axis=0).astype(BF16)
    diag_blocks = lambda d: jnp.concatenate([d[h * DK:(h + 1) * DK, h * DV:(h + 1) * DV] for h in range(N_HEADS)], axis=0)
    heads_lanes = lambda x: jnp.concatenate([x[h * C:(h + 1) * C, :] for h in range(N_HEADS)], axis=1)
    row = _iota((tb, tb), 0)
    col = _iota((tb, tb), 1)

    if mode == "gla":
        same = (row // C) == (col // C)
        b = _mm_split(jnp.where(same & (col <= row), 1.0, 0.0).astype(BF16), la)
        btot = _mm_split(jnp.where(same, 1.0, 0.0).astype(BF16), la)
        k_scr[...] = k
        b_scr[...] = b
        qe = q * jnp.exp(b)
        ke = k * jnp.exp(btot - b)
        sel = jnp.where(_iota((tb, nc * DV), 0) // C == _iota((tb, nc * DV), 1) // DV, 1.0, 0.0).astype(BF16)
        la_hi, la_lo = _split2(la)
        tn = (((0,), (0,)), ((), ()))
        gcols = jnp.exp(lax.dot_general(la_hi, sel, tn, preferred_element_type=F32)
                        + lax.dot_general(la_lo, sel, tn, preferred_element_type=F32))
        ind = jnp.where((_iota((HK, N_HEADS * DV), 0) // DK) == (_iota((HK, N_HEADS * DV), 1) // DV), 1.0, 0.0).astype(BF16)
        trow = _iota((C, HK), 0)
        chunks = range(nc)
        X = []
        for c in chunks:
            qc = q[c * C:(c + 1) * C]
            bc = b[c * C:(c + 1) * C]
            xs = []
            for s in range(C):
                r = c * C + s
                e = jnp.where(trow >= s, jnp.exp(jnp.minimum(bc - b_scr[r:r + 1, :], 0.0)), 0.0)
                xs.append((qc * k_scr[r:r + 1, :] * e).astype(BF16))
            X.append(jnp.concatenate(xs, axis=0))
        R = [jnp.dot(x, ind, preferred_element_type=F32) for x in X]
        o_chunks = []
        for c in chunks:
            o_c = R[c][0:C, :] * v_ref[c * C:c * C + 1, :]
            for s in range(1, C):
                o_c = o_c + R[c][s * C:(s + 1) * C, :] * v_ref[c * C + s:c * C + s + 1, :]
            o_chunks.append(o_c)
        dS = [diag_blocks(_mm_tn(ke[c * C:(c + 1) * C], v_ref[c * C:(c + 1) * C, :])) for c in chunks]
        L = [heads_rows(qe[c * C:(c + 1) * C]) for c in chunks]
        S = S_scr[...]
        for c in chunks:
            o_chunks[c] = o_chunks[c] + heads_lanes(jnp.dot(L[c], S.astype(BF16), preferred_element_type=F32))
            S = S * gcols[:, c * DV:(c + 1) * DV] + dS[c]
        S_scr[...] = S
        o = jnp.concatenate(o_chunks, axis=0)
    else:
        lg = p2_ref[...]
        tcol = _iota((tb, 1), 0).astype(F32)
        qe = q * jnp.exp((tcol + 1.0) * lg)
        ke = k * jnp.exp((tb - 1.0 - tcol) * lg)
        dist = (row - col).astype(F32)
        kb = k.astype(BF16)
        Q = heads_rows(q)
        o_heads = []
        for h in range(N_HEADS):
            s = lax.dot_general(Q[h * tb:(h + 1) * tb], kb, (((1,), (1,)), ((), ())), preferred_element_type=F32)
            dec = jnp.where(row >= col, jnp.exp(jnp.minimum(dist * lg[:, h * DK:h * DK + 1], 0.0)), 0.0)
            o_heads.append(_mm(s * dec, v_ref[:, h * DV:(h + 1) * DV]))
        S = S_scr[...]
        o = jnp.concatenate(o_heads, axis=1) + heads_lanes(jnp.dot(heads_rows(qe), S.astype(BF16),
                                                                   preferred_element_type=F32))
        S_scr[...] = S * jnp.exp(tb * p1_ref[...]) + diag_blocks(_mm_tn(ke, v_ref[...]))

    for h in range(N_HEADS):
        sl = slice(h * DV, (h + 1) * DV)
        o_ref[:, sl] = _head_rms_gate(o[:, sl], g_ref[:, sl], nw_ref[...]).astype(o_ref.dtype)

    @pl.when(tblk == pl.num_programs(1) - 1)
    def _():
        sout_ref[...] = S_scr[...]


def decay_mixer(mode, cols, batch, seq, blk_qk, blk_v, blk_g, aux, aux_is_cols, p1, p2, norm_w, s0):
    tb = min(SEQ_BLOCK, seq)
    C = min(CHUNK, seq) if mode == "gla" else tb
    nt = seq // tb
    HK = N_HEADS * DK
    row_map = lambda bl: (lambda b, t: (b * nt + t, bl))
    if aux_is_cols:
        aux_spec = pl.BlockSpec((tb, GROUP_W), row_map(BLK_MISC))
    else:
        aux_spec = pl.BlockSpec((tb, aux.shape[1]), lambda b, t: (t, 0))
    full = lambda a: pl.BlockSpec(a.shape, lambda b, t: (0,) * a.ndim)
    return pl.pallas_call(
        functools.partial(_decay_kernel, mode, C, tb),
        grid=(batch, nt),
        in_specs=[pl.BlockSpec((tb, GROUP_W), row_map(blk_qk)),
                  pl.BlockSpec((tb, GROUP_W), row_map(blk_v)),
                  pl.BlockSpec((tb, GROUP_W), row_map(blk_g)),
                  aux_spec, full(p1), full(p2), full(norm_w),
                  pl.BlockSpec((None, HK, DV), lambda b, t: (b, 0, 0))],
        out_specs=[pl.BlockSpec((tb, GROUP_W), lambda b, t: (b * nt + t, 0)),
                   pl.BlockSpec((None, HK, DV), lambda b, t: (b, 0, 0))],
        out_shape=[jax.ShapeDtypeStruct((batch * seq, GROUP_W), BF16),
                   jax.ShapeDtypeStruct((batch, HK, DV), F32)],
        scratch_shapes=[pltpu.VMEM((HK, DV), F32)] + [pltpu.VMEM((tb, HK), F32)] * 2,
        compiler_params=_params(("parallel", "arbitrary")),
        name="decay_" + mode,
    )(cols, cols, cols, aux, p1, p2, norm_w, s0)


def pad_in_cols(w):
    z = jnp.zeros(w.shape[:-1] + (COLS_W - 7312,), w.dtype)
    return jnp.concatenate([w[..., 0:1024], w[..., 1040:1552], w[..., 1552:5136], w[..., 5136:6672],
                            w[..., 6800:7312], w[..., 6672:6800], w[..., 1024:1040], z], axis=-1)


def gla_mixer(cols, batch, seq, lp, s0):
    HK = N_HEADS * DK
    w_a2 = jnp.zeros((LANE, HK), F32).at[:GLA_RANK].set(lp['gla_w_a2']).astype(BF16)
    b_a = lp['gla_b_a'].reshape(1, HK)
    return decay_mixer("gla", cols, batch, seq, BLK_GLA_QK, BLK_GLA_V, BLK_GLA_G, cols, True,
                       w_a2, b_a, lp['gla_norm'].reshape(1, DV), s0)


def ret_mixer(cols, batch, seq, lp, s0, pos):
    HK = N_HEADS * DK
    half = DK // 2
    inv = ROPE_BASE ** (-jnp.arange(half, dtype=F32) / half)
    ang = pos.astype(F32)[:, None] * inv[None, :]
    cos = jnp.tile(jnp.cos(ang), (1, 2 * N_HEADS))
    sin = jnp.tile(jnp.concatenate([-jnp.sin(ang), jnp.sin(ang)], axis=1), (1, N_HEADS))
    tables = jnp.concatenate([cos, sin], axis=1)
    log_gamma = jnp.log(1.0 - 2.0 ** (-5.0 - jnp.arange(N_HEADS, dtype=F32)))
    la = jnp.repeat(log_gamma, DK)
    return decay_mixer("ret", cols, batch, seq, BLK_RET_QK, BLK_RET_V, BLK_RET_G, tables, False,
                       jnp.broadcast_to(la[:, None], (HK, DV)), la.reshape(1, HK), lp['ret_norm'].reshape(1, DV), s0)


def _t5_bucket(rel):
    n = jnp.maximum(rel, 0)
    max_exact = REL_BUCKETS // 2
    nf = jnp.maximum(n, 1).astype(F32)
    large = max_exact + (jnp.log(nf / max_exact) / math.log(REL_MAX_DIST / max_exact)
                         * (REL_BUCKETS - max_exact)).astype(jnp.int32)
    large = jnp.minimum(large, REL_BUCKETS - 1)
    return jnp.where(n < max_exact, n, large)


def _rel_bias_tile(rel_bias, rel):
    onehot = (_t5_bucket(rel)[..., None] == jnp.arange(REL_BUCKETS, dtype=jnp.int32)).astype(F32)
    b = jnp.einsum('qkb,bh->hqk', onehot, rel_bias.astype(F32), precision=HIGHEST)
    return jnp.where(rel[None] >= 0, b, NEG_BIG)


def _lambda(lam_ref, lam_init):
    s1 = jnp.sum(lam_ref[0:1, :] * lam_ref[1:2, :], axis=-1, keepdims=True)
    s2 = jnp.sum(lam_ref[2:3, :] * lam_ref[3:4, :], axis=-1, keepdims=True)
    return jnp.exp(s1) - jnp.exp(s2) + lam_init


def _diff_finish(o1, o2, lam, lam_init, g, sw):
    o = o1 - lam * o2
    y = o * lax.rsqrt(jnp.mean(o * o, axis=-1, keepdims=True) + EPS) * sw * (1.0 - lam_init)
    return y * (g * _sigmoid(g))


def _diffattn_kernel(lam_init, tq, tk, q_ref, k_ref, v_ref, g_ref, bias_ref, lam_ref, sw_ref,
                     o_ref, kb_scr, vt_scr):
    qi = pl.program_id(1)
    heads = range(N_HEADS)
    hs = lambda h: slice(h * DV, (h + 1) * DV)

    @pl.when(qi == 0)
    def _():
        for h in heads:
            kb_scr[h] = k_ref[:, hs(h)].astype(BF16)
            for j in range(vt_scr.shape[1]):
                vt_scr[h, j] = v_ref[j * tk:(j + 1) * tk, hs(h)].T.astype(BF16)

    first_map = _iota((tq, DV), 1) < DK
    Q = []
    for h in heads:
        q = q_ref[:, hs(h)] * (DK ** -0.5)
        Q.append(jnp.concatenate([jnp.where(first_map, q, 0.0), jnp.where(first_map, 0.0, q)], axis=0).astype(BF16))

    def step(j, carry, bias_idx):
        m, l, acc = carry
        r0 = pl.multiple_of(j * tk, tk)
        st = [lax.dot_general(kb_scr[h, pl.ds(r0, tk), :], Q[h], (((1,), (1,)), ((), ())),
                              preferred_element_type=F32) for h in heads]
        if bias_idx is not None:
            st = [st[h] + jnp.concatenate([bias_ref[h, bias_idx]] * 2, axis=1) for h in heads]
        m_new = [jnp.maximum(m[h], jnp.max(st[h], axis=0, keepdims=True)) for h in heads]
        p = [jnp.exp(st[h] - m_new[h]) for h in heads]
        alpha = [jnp.exp(m[h] - m_new[h]) for h in heads]
        l = [alpha[h] * l[h] + jnp.sum(p[h], axis=0, keepdims=True) for h in heads]
        acc = [alpha[h] * acc[h] + jnp.dot(vt_scr[h, j], p[h].astype(BF16), preferred_element_type=F32)
               for h in heads]
        return m_new, l, acc

    carry = ([jnp.full((1, 2 * tq), NEG_BIG, F32)] * N_HEADS, [jnp.zeros((1, 2 * tq), F32)] * N_HEADS,
             [jnp.zeros((DV, 2 * tq), F32)] * N_HEADS)
    first_near = jnp.maximum(qi - 1, 0)
    carry = lax.fori_loop(0, first_near, lambda j, c: step(j, c, None), carry)
    _, l, acc = lax.fori_loop(first_near, qi + 1, lambda j, c: step(j, c, j - qi + 1), carry)

    lam = _lambda(lam_ref, lam_init)
    for h in heads:
        ot = acc[h][:, :tq] / l[h][:, :tq] - lam * (acc[h][:, tq:] / l[h][:, tq:])
        yt = ot * lax.rsqrt(jnp.mean(ot * ot, axis=0, keepdims=True) + EPS) * sw_ref[...] * (1.0 - lam_init)
        g = g_ref[:, hs(h)]
        o_ref[:, hs(h)] = (yt.T * (g * _sigmoid(g))).astype(o_ref.dtype)


def prompt_bias_tiles(rel_bias, seq):
    tq = tk = min(ATTN_BLOCK, seq)
    assert tk >= REL_MAX_DIST
    d = _iota((tk, tq), 1) - _iota((tk, tq), 0)
    far = rel_bias.astype(F32)[REL_BUCKETS - 1][:, None, None]
    return jnp.stack([_rel_bias_tile(rel_bias, d + tk) - far, _rel_bias_tile(rel_bias, d) - far], axis=1)


def diff_attn_prompt(cols, batch, seq, lp, bias_tiles, lam_init):
    tq = tk = min(ATTN_BLOCK, seq)
    nq = seq // tq
    lam4 = jnp.stack([lp['diff_lam_q1'], lp['diff_lam_k1'], lp['diff_lam_q2'], lp['diff_lam_k2']]).astype(F32)
    const = lambda a: pl.BlockSpec(a.shape, lambda b, i: (0,) * a.ndim)
    sw = lp['diff_subln'].reshape(DV, 1)
    return pl.pallas_call(
        functools.partial(_diffattn_kernel, lam_init, tq, tk),
        grid=(batch, nq),
        in_specs=[pl.BlockSpec((tq, GROUP_W), lambda b, i: (b * nq + i, BLK_DIFF_Q)),
                  pl.BlockSpec((seq, GROUP_W), lambda b, i: (b, BLK_DIFF_K)),
                  pl.BlockSpec((seq, GROUP_W), lambda b, i: (b, BLK_DIFF_V)),
                  pl.BlockSpec((tq, GROUP_W), lambda b, i: (b * nq + i, BLK_DIFF_G)),
                  const(bias_tiles), const(lam4), const(sw)],
        out_specs=pl.BlockSpec((tq, GROUP_W), lambda b, i: (b * nq + i, 0)),
        out_shape=jax.ShapeDtypeStruct((batch * seq, GROUP_W), BF16),
        scratch_shapes=[pltpu.VMEM((N_HEADS, seq, DV), BF16), pltpu.VMEM((N_HEADS, seq // tk, DV, tk), BF16)],
        compiler_params=_params(("arbitrary", "arbitrary")),
        name="diff_attn_prompt",
    )(cols, cols, cols, cols, bias_tiles, lam4, sw)


def _diffdec_kernel(lam_init, PP, ts, page, pt_ref, q_ref, kn_ref, vn_ref, g_ref, *rest):
    k_refs = rest[:PP]
    v_refs = rest[PP:2 * PP]
    bias_ref, biasn_ref, lam_ref, sw_ref, o_ref, q_scr = rest[2 * PP:2 * PP + 6]
    state = rest[2 * PP + 6:]
    m_scr, l_scr, acc_scr = state[0::3], state[1::3], state[2::3]
    NCH = len(m_scr)
    per = PP // NCH
    nkc = per * page
    step_id = pl.program_id(1)
    W = N_HEADS * DV

    @pl.when(step_id == 0)
    def _():
        q = q_ref[...] * (DK ** -0.5)
        lane = _iota(q.shape, 1)
        pieces = []
        for h in range(N_HEADS):
            for mp in range(2):
                lo = h * DV + mp * DK
                pieces.append(jnp.where((lane >= lo) & (lane < lo + DK), q, 0.0))
        q_scr[...] = jnp.concatenate(pieces, axis=0).astype(BF16)
        for c in range(NCH):
            m_scr[c][...] = jnp.full(m_scr[c].shape, NEG_BIG, F32)
            l_scr[c][...] = jnp.zeros(l_scr[c].shape, F32)
            acc_scr[c][...] = jnp.zeros(acc_scr[c].shape, F32)

    def update(chains, kb, vb, bias):
        qb = q_scr[...]
        n = range(len(chains))
        s = [lax.dot_general(qb, kb[i], (((1,), (1,)), ((), ())), preferred_element_type=F32) + bias[i] for i in n]
        m_old = [m_scr[c][...] for c in chains]
        m_new = [jnp.maximum(m_old[i], jnp.max(s[i], axis=-1, keepdims=True)) for i in n]
        p = [jnp.exp(s[i] - m_new[i]) for i in n]
        alpha = [jnp.exp(m_old[i] - m_new[i]) for i in n]
        for i, c in enumerate(chains):
            l_scr[c][...] = alpha[i] * l_scr[c][...] + jnp.sum(p[i], axis=-1, keepdims=True)
            acc_scr[c][...] = alpha[i] * acc_scr[c][...] + jnp.dot(p[i].astype(BF16), vb[i], preferred_element_type=F32)
            m_scr[c][...] = m_new[i]

    def page_rows(ref):
        return jnp.concatenate([ref[pl.ds(h, page, stride=N_HEADS), :] for h in range(N_HEADS)], axis=1).astype(BF16)

    update(list(range(NCH)),
           [jnp.concatenate([page_rows(r) for r in k_refs[c * per:(c + 1) * per]], axis=0) for c in range(NCH)],
           [jnp.concatenate([page_rows(r) for r in v_refs[c * per:(c + 1) * per]], axis=0) for c in range(NCH)],
           [bias_ref[:, c * nkc:(c + 1) * nkc] for c in range(NCH)])

    @pl.when(step_id == pl.num_programs(1) - 1)
    def _():
        pad = jnp.zeros((page - ts, W), BF16)
        update([0], [jnp.concatenate([kn_ref[...].astype(BF16), pad], axis=0)],
               [jnp.concatenate([vn_ref[...].astype(BF16), pad], axis=0)], [biasn_ref[...]])
        m = m_scr[0][...]
        for c in range(1, NCH):
            m = jnp.maximum(m, m_scr[c][...])
        w = [jnp.exp(m_scr[c][...] - m) for c in range(NCH)]
        ln_all = sum(w[c] * l_scr[c][...] for c in range(NCH))
        acc_all = sum(w[c] * acc_scr[c][...] for c in range(NCH))
        lam = _lambda(lam_ref, lam_init)
        rows = 2 * ts
        for h in range(N_HEADS):
            a = acc_all[h * rows:(h + 1) * rows, h * DV:(h + 1) * DV]
            ln = ln_all[h * rows:(h + 1) * rows, :]
            o_ref[:, h * DV:(h + 1) * DV] = _diff_finish(
                a[:ts] / ln[:ts], a[ts:] / ln[ts:], lam, lam_init,
                g_ref[:, h * DV:(h + 1) * DV], sw_ref[...]).astype(o_ref.dtype)


def sample_bias_tiles(rel_bias, ts, n_pages, page):
    n_steps = n_pages // DECODE_PAGES
    past = n_pages * page
    nk = DECODE_PAGES * page
    assert nk >= REL_MAX_DIST
    t_of_row = jnp.tile(jnp.arange(ts, dtype=jnp.int32), N_HEADS * 2)
    h_of_row = jnp.repeat(jnp.arange(N_HEADS, dtype=jnp.int32), 2 * ts)
    head_sel = h_of_row[None, :, None] == jnp.arange(N_HEADS, dtype=jnp.int32)[:, None, None]

    def rows_bias(k_pos):
        rel = past + t_of_row[:, None] - k_pos[None, :]
        return jnp.sum(jnp.where(head_sel, _rel_bias_tile(rel_bias, rel), 0.0), axis=0)

    kk = jnp.arange(nk, dtype=jnp.int32)
    bias_steps = jnp.stack([rows_bias(kk), rows_bias((n_steps - 1) * nk + kk)])
    kn = jnp.arange(page, dtype=jnp.int32)
    bias_new = jnp.where(kn[None, :] < ts, rows_bias(past + kn), NEG_BIG)
    return bias_steps, bias_new


def diff_attn_sample(cols, batch, ts, cache_k, cache_v, layer, page_table, lp, bias_tiles, lam_init):
    page = cache_k.shape[2] // N_HEADS
    W = N_HEADS * DV
    n_pages = page_table.shape[1]
    PP = DECODE_PAGES
    n_steps = n_pages // PP
    nk = PP * page
    rows = N_HEADS * 2 * ts
    bias_steps, bias_new = bias_tiles
    lam4 = jnp.stack([lp['diff_lam_q1'], lp['diff_lam_k1'], lp['diff_lam_q2'], lp['diff_lam_k2']]).astype(F32)

    def page_spec(i):
        return pl.BlockSpec((None, None, page * N_HEADS, DV),
                            lambda b, s, pt: (layer, pt[b * n_pages + s * PP + i], 0, 0))

    col_spec = lambda blk: pl.BlockSpec((ts, GROUP_W), lambda b, s, pt: (b, blk))
    const = lambda a: pl.BlockSpec(a.shape, lambda b, s, pt: (0,) * a.ndim)
    grid_spec = pltpu.PrefetchScalarGridSpec(
        num_scalar_prefetch=1,
        grid=(batch, n_steps),
        in_specs=[col_spec(BLK_DIFF_Q), col_spec(BLK_DIFF_K), col_spec(BLK_DIFF_V), col_spec(BLK_DIFF_G)]
                 + [page_spec(i) for i in range(PP)] + [page_spec(i) for i in range(PP)]
                 + [pl.BlockSpec((None, rows, nk), lambda b, s, pt: (jnp.where(s == n_steps - 1, 1, 0), 0, 0)),
                    const(bias_new), const(lam4), pl.BlockSpec((1, DV), lambda b, s, pt: (0, 0))],
        out_specs=pl.BlockSpec((ts, GROUP_W), lambda b, s, pt: (b, 0)),
        scratch_shapes=[pltpu.VMEM((rows, W), BF16)]
                       + [pltpu.VMEM((rows, 1), F32), pltpu.VMEM((rows, 1), F32), pltpu.VMEM((rows, W), F32)] * DECODE_CHAINS)
    return pl.pallas_call(
        functools.partial(_diffdec_kernel, lam_init, PP, ts, page),
        grid_spec=grid_spec,
        out_shape=jax.ShapeDtypeStruct((batch * ts, GROUP_W), BF16),
        compiler_params=_params(("parallel", "arbitrary")),
        name="diff_attn_sample",
    )(page_table.reshape(-1), cols, cols, cols, cols, *([cache_k] * PP), *([cache_v] * PP),
      bias_steps, bias_new, lam4, lp['diff_subln'].reshape(1, DV))


RWKV_PAIRS = RWKV_HEADS // 2
SHIFT_PAD = 4 * GROUP_W


def _rwkv_kernel(C, tb, r_ref, k_ref, v_ref, misc_ref, g_ref, shift_ref, s0_ref,
                 mu_ref, w0_ref, ww2_ref, a0_ref, aw2_ref, kk_ref, ka_ref, rk_ref, nw_ref,
                 o_ref, sout_ref, S_scr, carry_scr):
    G = GROUP_W
    tblk = pl.program_id(1)

    @pl.when(tblk == 0)
    def _():
        S_scr[...] = s0_ref[...]
        carry_scr[...] = shift_ref[...]

    first_row = _iota((tb, 1), 0) == 0

    def token_shift(p, lo, width):
        prev = jnp.where(first_row, carry_scr[:, lo:lo + width], pltpu.roll(p, 1, axis=0))
        carry_scr[:, lo:lo + width] = p[tb - 1:tb, :]
        return p + mu_ref[:, lo:lo + width] * (prev - p)

    r = token_shift(r_ref[...], 0, G)
    k = token_shift(k_ref[...], G, G)
    v = token_shift(v_ref[...], 2 * G, G)
    lora = token_shift(misc_ref[:, :LANE], 3 * G, LANE)

    w_pre = w0_ref[...] + jnp.dot(jnp.tanh(lora).astype(BF16), ww2_ref[...], preferred_element_type=F32)
    lw = -jnp.exp(-_softplus(-w_pre) - 0.5)
    a = _sigmoid(a0_ref[...] + jnp.dot(lora.astype(BF16), aw2_ref[...], preferred_element_type=F32))

    kk = k * kk_ref[...]
    kk = kk / jnp.maximum(jnp.sqrt(_head_sums(kk * kk)), 1e-12)
    k2 = k * (1.0 + (a - 1.0) * ka_ref[...])
    bb = kk * a
    bonus = _head_sums(r * k2 * rk_ref[...]) * v

    row = _iota((tb, tb), 0)
    col = _iota((tb, tb), 1)
    same = (row // C) == (col // C)
    cum = _mm_split(jnp.where(same & (col <= row), 1.0, 0.0).astype(BF16), lw)
    tot = _mm_split(jnp.where(same, 1.0, 0.0).astype(BF16), lw)
    inv = jnp.exp(-cum)
    rest = jnp.exp(tot - cum)
    rt = r * jnp.exp(cum)
    kt = kk * jnp.exp(cum - lw)
    kd = k2 * inv
    bd = bb * inv
    kh = k2 * rest
    bh = bb * rest

    C2 = 2 * C
    nc = tb // C
    lane0 = _iota((C, LANE), 1) < RWKV_N
    rr = _iota((C2, C2), 0)
    cc = _iota((C2, C2), 1)
    strict = (rr % C) > (cc % C)
    incl = (rr % C) >= (cc % C)
    eye = jnp.where(rr == cc, 1.0, 0.0)
    eye_lane = _iota((LANE, LANE), 0) == _iota((LANE, LANE), 1)
    n_double = max(int(math.log2(C)) - 1, 0)
    probs = [(c, p) for c in range(nc) for p in range(RWKV_PAIRS)]

    def split(x, c, p):
        xs = x[c * C:(c + 1) * C, p * LANE:(p + 1) * LANE]
        return jnp.concatenate([jnp.where(lane0, xs, 0.0), jnp.where(lane0, 0.0, xs)], axis=0).astype(BF16)

    Lk = [split(kt, c, p) for c, p in probs]
    Lr = [split(rt, c, p) for c, p in probs]
    Rk = [split(kd, c, p) for c, p in probs]
    Rb = [split(bd, c, p) for c, p in probs]
    Vb = [split(v, c, p) for c, p in probs]
    Kh = [split(kh, c, p) for c, p in probs]
    Bh = [split(bh, c, p) for c, p in probs]
    A_kk = [jnp.where(strict, _mm_nt(x, y), 0.0).astype(BF16) for x, y in zip(Lk, Rk)]
    A_kb = [jnp.where(strict, _mm_nt(x, y), 0.0) for x, y in zip(Lk, Rb)]
    A_rk = [jnp.where(incl, _mm_nt(x, y), 0.0).astype(BF16) for x, y in zip(Lr, Rk)]
    A_rb = [jnp.where(incl, _mm_nt(x, y), 0.0).astype(BF16) for x, y in zip(Lr, Rb)]
    X = [-x for x in A_kb]
    T = [eye + x for x in X]
    for _ in range(n_double):
        X = [_mm(x, x) for x in X]
        T = [t + _mm(t, x) for t, x in zip(T, X)]
    T = [t.astype(BF16) for t in T]
    Wk = [_mm(t, x).astype(BF16) for t, x in zip(T, Lk)]
    U0 = [_mm(t, _mm(x, y)).astype(BF16) for t, x, y in zip(T, A_kk, Vb)]
    Wr = [x.astype(F32) - _mm(y, z) for x, y, z in zip(Lr, A_rb, Wk)]
    O0 = [_mm(x, y) - _mm(z, u) for x, y, z, u in zip(A_rk, Vb, A_rb, U0)]
    N = [_mm_tn(x, y) for x, y in zip(Bh, Wk)]
    D0 = [_mm_tn(x, y) - _mm_tn(z, u) for x, y, z, u in zip(Kh, Vb, Bh, U0)]
    WN = [jnp.concatenate([x, y], axis=0).astype(BF16) for x, y in zip(Wr, N)]
    gcol = [jnp.exp(jnp.sum(jnp.where(eye_lane, tot[c * C:c * C + 1, p * LANE:(p + 1) * LANE], 0.0),
                            axis=1, keepdims=True)) for c, p in probs]

    S = [S_scr[p] for p in range(RWKV_PAIRS)]
    o_rows = []
    for c in range(nc):
        o_lanes = []
        for p in range(RWKV_PAIRS):
            i = c * RWKV_PAIRS + p
            Rm = jnp.dot(WN[i], S[p].astype(BF16), preferred_element_type=F32)
            O = Rm[:C2] + O0[i]
            o_lanes.append(O[:C] + O[C:])
            S[p] = S[p] * gcol[i] - Rm[C2:] + D0[i]
        o_rows.append(jnp.concatenate(o_lanes, axis=1))
    for p in range(RWKV_PAIRS):
        S_scr[p] = S[p]

    o = jnp.concatenate(o_rows, axis=0)
    ms = _head_sums(o * o) * (1.0 / RWKV_N)
    y = o * lax.rsqrt(ms + EPS) * nw_ref[...] + bonus
    g = g_ref[...]
    o_ref[...] = (y * (g * _sigmoid(g))).astype(o_ref.dtype)

    @pl.when(tblk == pl.num_programs(1) - 1)
    def _():
        sout_ref[...] = S_scr[...]


def rwkv_mixer(cols, batch, seq, lp, s0_pairs, shift_pad):
    C = min(64, seq)
    tb = min(SEQ_BLOCK, seq)
    nt = seq // tb
    G = GROUP_W
    mu = lp['rwkv_mu']
    mu_pad = jnp.concatenate([mu, jnp.zeros((SHIFT_PAD - mu.shape[0],), F32)]).reshape(1, SHIFT_PAD)
    ww2 = jnp.zeros((LANE, G), F32).at[:RWKV_LORA].set(lp['rwkv_w_w2']).astype(BF16)
    aw2 = jnp.zeros((LANE, G), F32).at[RWKV_LORA:].set(lp['rwkv_a_w2']).astype(BF16)
    vec = lambda a: a.reshape(1, G).astype(F32)
    params = [mu_pad, vec(lp['rwkv_w0']), ww2, vec(lp['rwkv_a0']), aw2, vec(lp['rwkv_k_k']), vec(lp['rwkv_k_a']),
              vec(lp['rwkv_r_k']), jnp.tile(lp['rwkv_norm'], RWKV_HEADS).reshape(1, G)]
    row_map = lambda bl: (lambda b, t: (b * nt + t, bl))
    full = lambda a: pl.BlockSpec(a.shape, lambda b, t: (0,) * a.ndim)
    return pl.pallas_call(
        functools.partial(_rwkv_kernel, C, tb),
        grid=(batch, nt),
        in_specs=[pl.BlockSpec((tb, G), row_map(BLK_RWKV_R)), pl.BlockSpec((tb, G), row_map(BLK_RWKV_K)),
                  pl.BlockSpec((tb, G), row_map(BLK_RWKV_V)), pl.BlockSpec((tb, G), row_map(BLK_MISC)),
                  pl.BlockSpec((tb, G), row_map(BLK_RWKV_G)),
                  pl.BlockSpec((None, 1, SHIFT_PAD), lambda b, t: (b, 0, 0)),
                  pl.BlockSpec((None, RWKV_PAIRS, LANE, LANE), lambda b, t: (b, 0, 0, 0))]
                 + [full(a) for a in params],
        out_specs=[pl.BlockSpec((tb, G), lambda b, t: (b * nt + t, 0)),
                   pl.BlockSpec((None, RWKV_PAIRS, LANE, LANE), lambda b, t: (b, 0, 0, 0))],
        out_shape=[jax.ShapeDtypeStruct((batch * seq, G), BF16),
                   jax.ShapeDtypeStruct((batch, RWKV_PAIRS, LANE, LANE), F32)],
        scratch_shapes=[pltpu.VMEM((RWKV_PAIRS, LANE, LANE), F32), pltpu.VMEM((1, SHIFT_PAD), F32)],
        compiler_params=_params(("parallel", "arbitrary")),
        name="rwkv7",
    )(cols, cols, cols, cols, cols, shift_pad, s0_pairs, *params)


def rwkv_state_to_pairs(s):
    b = s.shape[0]
    st = jnp.swapaxes(s, -1, -2).reshape(b, RWKV_PAIRS, 2, RWKV_N, RWKV_N)
    z = jnp.zeros_like(st[:, :, 0])
    top = jnp.concatenate([st[:, :, 0], z], axis=-1)
    bot = jnp.concatenate([z, st[:, :, 1]], axis=-1)
    return jnp.concatenate([top, bot], axis=-2)


def rwkv_pairs_to_state(sp):
    b = sp.shape[0]
    h0 = sp[:, :, :RWKV_N, :RWKV_N]
    h1 = sp[:, :, RWKV_N:, RWKV_N:]
    st = jnp.stack([h0, h1], axis=2).reshape(b, RWKV_HEADS, RWKV_N, RWKV_N)
    return jnp.swapaxes(st, -1, -2)


def _group_layer(x, scale, shift, gate, rows_per_mod, batch, seq, lp, w_pad, w_o_bf, final_w, final,
                 s_gla, s_ret, s_rwkv, s_shift, pos, diff_fn, tm_in, tm_out):
    cols = in_proj(x, scale, shift, lp['norm_w'], w_pad, rows_per_mod, tm_in, IN_TN)
    oa, n_gla = gla_mixer(cols, batch, seq, lp, s_gla)
    ob = diff_fn(cols)
    oc, n_ret = ret_mixer(cols, batch, seq, lp, s_ret, pos)
    od, n_rwkv = rwkv_mixer(cols, batch, seq, lp, s_rwkv, s_shift)
    y = out_proj(x, gate, oa, ob, oc, od, w_o_bf, final_w, rows_per_mod, tm_out, final)
    c3 = cols.reshape(batch, seq, COLS_W)
    k_new = c3[:, :, BLK_DIFF_K * GROUP_W:(BLK_DIFF_K + 1) * GROUP_W].reshape(batch, seq, N_HEADS, DV)
    v_new = c3[:, :, BLK_DIFF_V * GROUP_W:(BLK_DIFF_V + 1) * GROUP_W].reshape(batch, seq, N_HEADS, DV)
    last = c3[:, -1]
    n_shift = jnp.concatenate([last[:, BLK_RWKV_R * GROUP_W:(BLK_RWKV_V + 1) * GROUP_W],
                               last[:, BLK_MISC * GROUP_W:BLK_MISC * GROUP_W + 2 * RWKV_LORA]], axis=-1)
    return y, (k_new, v_new, n_gla.reshape(batch, N_HEADS, DK, DV), n_ret.reshape(batch, N_HEADS, DK, DV),
               rwkv_pairs_to_state(n_rwkv), n_shift)


def kernel(x_prompt, x_sample, cache_k, cache_v, state_gla, state_ret, state_rwkv, state_shift, page_table,
           c_prompt, c_sample, rel_bias, w_ada, b_ada, norm_w, w_in, w_o, gla_w_a2, gla_b_a, gla_norm,
           diff_lam_q1, diff_lam_k1, diff_lam_q2, diff_lam_k2, diff_subln, ret_norm, rwkv_mu, rwkv_w0, rwkv_w_w2,
           rwkv_a0, rwkv_a_w2, rwkv_k_k, rwkv_k_a, rwkv_r_k, rwkv_norm, final_norm):
    Bp, Sp, D = x_prompt.shape
    Bs, Ts, _ = x_sample.shape
    depth = w_in.shape[0]
    n_pool, page = cache_k.shape[1], cache_k.shape[2]
    past_len = page_table.shape[1] * page
    ck = cache_k.reshape(depth, n_pool, page * N_HEADS, DV)
    cv = cache_v.reshape(depth, n_pool, page * N_HEADS, DV)
    pos_p = jnp.arange(Sp, dtype=jnp.int32)
    pos_s = past_len + jnp.arange(Ts, dtype=jnp.int32)
    c_all = jnp.concatenate([c_prompt, c_sample], axis=0)
    xp = x_prompt.reshape(Bp * Sp, D)
    xs = x_sample.reshape(Bs * Ts, D)
    zeros_dec = jnp.zeros((Bp, N_HEADS * DK, DV), F32)
    zeros_rwkv = jnp.zeros((Bp, RWKV_PAIRS, LANE, LANE), F32)
    zeros_shift = jnp.zeros((Bp, 1, SHIFT_PAD), F32)
    bias_p = prompt_bias_tiles(rel_bias, Sp)
    bias_s = sample_bias_tiles(rel_bias, Ts, page_table.shape[1], page)
    new_p, new_s = [], []
    for l in range(depth):
        lp = dict(norm_w=norm_w[l], gla_w_a2=gla_w_a2[l], gla_b_a=gla_b_a[l], gla_norm=gla_norm[l],
                  diff_lam_q1=diff_lam_q1[l], diff_lam_k1=diff_lam_k1[l], diff_lam_q2=diff_lam_q2[l],
                  diff_lam_k2=diff_lam_k2[l], diff_subln=diff_subln[l], ret_norm=ret_norm[l],
                  rwkv_mu=rwkv_mu[l], rwkv_w0=rwkv_w0[l], rwkv_w_w2=rwkv_w_w2[l], rwkv_a0=rwkv_a0[l],
                  rwkv_a_w2=rwkv_a_w2[l], rwkv_k_k=rwkv_k_k[l], rwkv_k_a=rwkv_k_a[l], rwkv_r_k=rwkv_r_k[l],
                  rwkv_norm=rwkv_norm[l])
        final = l == depth - 1
        lam_init = 0.8 - 0.6 * math.exp(-0.3 * l)
        mod = ada_mod(c_all, w_ada[l], b_ada[l])
        shift, scale, gate = mod[:, :D], mod[:, D:2 * D], mod[:, 2 * D:]
        w_pad = pad_in_cols(w_in[l]).astype(BF16)
        w_o_bf = w_o[l].astype(BF16)

        per_batch = lambda a: a[:Bp].reshape(Bp, 1, D)
        xp, st_p = _group_layer(
            xp, per_batch(scale), per_batch(shift), per_batch(gate), Sp, Bp, Sp, lp, w_pad, w_o_bf, final_norm, final,
            zeros_dec, zeros_dec, zeros_rwkv, zeros_shift, pos_p,
            lambda cols: diff_attn_prompt(cols, Bp, Sp, lp, bias_p, lam_init), min(IN_TM, Sp), min(OUT_TM, Sp))

        per_row = lambda a: jnp.repeat(a[Bp:], Ts, axis=0)
        sh = state_shift[l]
        sh_pad = jnp.concatenate([sh, jnp.zeros((Bs, SHIFT_PAD - sh.shape[1]), F32)], axis=1).reshape(Bs, 1, SHIFT_PAD)
        xs, st_s = _group_layer(
            xs, per_row(scale), per_row(shift), per_row(gate), 1, Bs, Ts, lp, w_pad, w_o_bf, final_norm, final,
            state_gla[l].reshape(Bs, N_HEADS * DK, DV), state_ret[l].reshape(Bs, N_HEADS * DK, DV),
            rwkv_state_to_pairs(state_rwkv[l]), sh_pad, pos_s,
            lambda cols: diff_attn_sample(cols, Bs, Ts, ck, cv, l, page_table, lp, bias_s, lam_init),
            Bs * Ts, Bs * Ts)
        new_p.append(st_p)
        new_s.append(st_s)
    stack = lambda states, i: jnp.stack([s[i] for s in states])
    return (xp.reshape(Bp, Sp, D), xs.reshape(Bs, Ts, D),
            stack(new_p, 0), stack(new_p, 1), stack(new_s, 0), stack(new_s, 1),
            stack(new_p, 2), stack(new_s, 2), stack(new_p, 3), stack(new_s, 3),
            stack(new_p, 4), stack(new_s, 4), stack(new_p, 5), stack(new_s, 5))
```

```python
import functools
import math

import jax
import jax.numpy as jnp
from jax import lax
from jax.experimental import pallas as pl
from jax.experimental.pallas import tpu as pltpu

F32 = jnp.float32
BF16 = jnp.bfloat16
HIGHEST = lax.Precision.HIGHEST

D_MODEL = 2048
GROUP_W = 512
N_HEADS = 4
DK = 64
DV = 128
GLA_RANK = 16
GLA_TAU = 16.0
RWKV_N = 64
RWKV_HEADS = 8
RWKV_LORA = 64
ROPE_BASE = 10000.0
REL_BUCKETS = 32
REL_MAX_DIST = 128
CHUNK = 16
EPS = 1e-6
NEG_BIG = -1e30

LANE = 128
VMEM_LIMIT = 48 * 1024 * 1024
ATTN_BLOCK = 256
ATTN_HEADS = 4
DECODE_PAGES = 16
DECODE_CHAINS = 2
SEQ_BLOCK = 256
IN_TM, IN_TN = 512, 1536
OUT_TM = 256

COLS_W = 15 * GROUP_W
BLK_GLA_QK, BLK_GLA_V, BLK_GLA_G = 0, 1, 2
BLK_DIFF_Q, BLK_DIFF_K, BLK_DIFF_V, BLK_DIFF_G = 3, 4, 5, 6
BLK_RET_QK, BLK_RET_V, BLK_RET_G = 7, 8, 9
BLK_RWKV_R, BLK_RWKV_K, BLK_RWKV_V, BLK_RWKV_G = 10, 11, 12, 13
BLK_MISC = 14


def _mm(a, b):
    return jnp.dot(a.astype(BF16), b.astype(BF16), preferred_element_type=F32)


def _mm_nt(a, b):
    return lax.dot_general(a.astype(BF16), b.astype(BF16), (((1,), (1,)), ((), ())), preferred_element_type=F32)


def _mm_tn(a, b):
    return lax.dot_general(a.astype(BF16), b.astype(BF16), (((0,), (0,)), ((), ())), preferred_element_type=F32)


def _mm_exact(a, b):
    return jnp.dot(a, b, precision=HIGHEST, preferred_element_type=F32)


def _mm_tn_exact(a, b):
    return lax.dot_general(a, b, (((0,), (0,)), ((), ())), precision=HIGHEST, preferred_element_type=F32)


def _split2(x):
    hi = x.astype(BF16)
    return hi, (x - hi.astype(F32)).astype(BF16)


def _mm_split(a, x):
    hi, lo = _split2(x)
    return jnp.dot(a, hi, preferred_element_type=F32) + jnp.dot(a, lo, preferred_element_type=F32)


def _head_sums(x):
    ind = jnp.where((_iota((LANE, LANE), 0) // RWKV_N) == (_iota((LANE, LANE), 1) // RWKV_N), 1.0, 0.0).astype(BF16)
    hi, lo = _split2(x)
    parts = []
    for i in range(x.shape[1] // LANE):
        sl = slice(i * LANE, (i + 1) * LANE)
        parts.append(jnp.dot(hi[:, sl], ind, preferred_element_type=F32) + jnp.dot(lo[:, sl], ind, preferred_element_type=F32))
    return jnp.concatenate(parts, axis=1)


def _sigmoid(x):
    return 1.0 / (1.0 + jnp.exp(-x))


def _softplus(x):
    return jnp.maximum(x, 0.0) + jnp.log1p(jnp.exp(-jnp.abs(x)))


def _iota(shape, dim):
    return lax.broadcasted_iota(jnp.int32, shape, dim)


def _params(sem):
    return pltpu.CompilerParams(dimension_semantics=sem, vmem_limit_bytes=VMEM_LIMIT)


def _ada_kernel(c_ref, w_ref, b_ref, o_ref):
    c = c_ref[...]
    o_ref[...] = _mm(c * _sigmoid(c), w_ref[...]) + b_ref[...]


def ada_mod(c, w_ada, b_ada, tn=512):
    rows, d = c.shape
    n = w_ada.shape[1]
    return pl.pallas_call(
        _ada_kernel,
        grid=(n // tn,),
        in_specs=[pl.BlockSpec((rows, d), lambda j: (0, 0)),
                  pl.BlockSpec((d, tn), lambda j: (0, j)),
                  pl.BlockSpec((1, tn), lambda j: (0, j))],
        out_specs=pl.BlockSpec((rows, tn), lambda j: (0, j)),
        out_shape=jax.ShapeDtypeStruct((rows, n), F32),
        compiler_params=_params(("arbitrary",)),
        name="ada_mod",
    )(c, w_ada, b_ada.reshape(1, n))


def _inproj_kernel(kv_tile, k_lo, v_lo, x_ref, scale_ref, shift_ref, nw_ref, w_ref, kin_ref, vin_ref,
                   o_ref, k_ref, v_ref, h_ref):
    del kin_ref, vin_ref
    j = pl.program_id(1)

    @pl.when(j == 0)
    def _():
        x = x_ref[...]
        y = x * lax.rsqrt(jnp.mean(x * x, axis=-1, keepdims=True) + EPS) * nw_ref[...]
        h_ref[...] = (y * (1.0 + scale_ref[...]) + shift_ref[...]).astype(BF16)

    o_ref[...] = jnp.dot(h_ref[...], w_ref[...], preferred_element_type=F32)

    @pl.when(j == kv_tile)
    def _():
        k_ref[...] = o_ref[:, k_lo:k_lo + GROUP_W]
        v_ref[...] = o_ref[:, v_lo:v_lo + GROUP_W]


def in_proj(x, scale, shift, norm_w, w_pad, k_all, v_all, layer, rows_per_mod, tm, tn):
    m, d = x.shape
    n = w_pad.shape[1]
    kv_tile = (BLK_DIFF_K * GROUP_W) // tn
    assert (BLK_DIFF_V * GROUP_W) // tn == kv_tile
    if rows_per_mod == 1:
        mod_spec = pl.BlockSpec((tm, d), lambda i, j: (i, 0))
    else:
        per = rows_per_mod // tm
        mod_spec = pl.BlockSpec((None, 1, d), lambda i, j: (i // per, 0, 0))
    kv_spec = pl.BlockSpec((None, tm, GROUP_W), lambda i, j: (layer, i, 0))
    return pl.pallas_call(
        functools.partial(_inproj_kernel, kv_tile, BLK_DIFF_K * GROUP_W - kv_tile * tn, BLK_DIFF_V * GROUP_W - kv_tile * tn),
        grid=(m // tm, n // tn),
        in_specs=[pl.BlockSpec((tm, d), lambda i, j: (i, 0)),
                  mod_spec, mod_spec,
                  pl.BlockSpec((1, d), lambda i, j: (0, 0)),
                  pl.BlockSpec((d, tn), lambda i, j: (0, j)),
                  pl.BlockSpec(memory_space=pl.ANY), pl.BlockSpec(memory_space=pl.ANY)],
        out_specs=[pl.BlockSpec((tm, tn), lambda i, j: (i, j)), kv_spec, kv_spec],
        out_shape=[jax.ShapeDtypeStruct((m, n), F32), jax.ShapeDtypeStruct(k_all.shape, F32),
                   jax.ShapeDtypeStruct(v_all.shape, F32)],
        input_output_aliases={5: 1, 6: 2},
        scratch_shapes=[pltpu.VMEM((tm, d), BF16)],
        compiler_params=_params(("parallel", "arbitrary")),
        name="in_proj",
    )(x, scale, shift, norm_w.reshape(1, d), w_pad, k_all, v_all)


def _outproj_kernel(final, x_ref, gate_ref, oa_ref, ob_ref, oc_ref, od_ref, w_ref, fw_ref, y_ref):
    acc = jnp.dot(oa_ref[...], w_ref[0 * GROUP_W:1 * GROUP_W, :], preferred_element_type=F32)
    acc += jnp.dot(ob_ref[...], w_ref[1 * GROUP_W:2 * GROUP_W, :], preferred_element_type=F32)
    acc += jnp.dot(oc_ref[...], w_ref[2 * GROUP_W:3 * GROUP_W, :], preferred_element_type=F32)
    acc += jnp.dot(od_ref[...], w_ref[3 * GROUP_W:4 * GROUP_W, :], preferred_element_type=F32)
    y = x_ref[...] + gate_ref[...] * acc
    if final:
        y = y * lax.rsqrt(jnp.mean(y * y, axis=-1, keepdims=True) + EPS) * fw_ref[...]
    y_ref[...] = y


def out_proj(x, gate, oa, ob, oc, od, w_o, final_w, rows_per_mod, tm, final):
    m, d = x.shape
    if rows_per_mod == 1:
        mod_spec = pl.BlockSpec((tm, d), lambda i: (i, 0))
    else:
        per = rows_per_mod // tm
        mod_spec = pl.BlockSpec((None, 1, d), lambda i: (i // per, 0, 0))
    o_spec = pl.BlockSpec((tm, GROUP_W), lambda i: (i, 0))
    return pl.pallas_call(
        functools.partial(_outproj_kernel, final),
        grid=(m // tm,),
        in_specs=[pl.BlockSpec((tm, d), lambda i: (i, 0)), mod_spec,
                  o_spec, o_spec, o_spec, o_spec,
                  pl.BlockSpec((d, d), lambda i: (0, 0)),
                  pl.BlockSpec((1, d), lambda i: (0, 0))],
        out_specs=pl.BlockSpec((tm, d), lambda i: (i, 0)),
        out_shape=jax.ShapeDtypeStruct((m, d), F32),
        compiler_params=_params(("parallel",)),
        name="out_proj",
    )(x, gate, oa, ob, oc, od, w_o, final_w.reshape(1, d))


def _head_rms_gate(o, g, nw):
    y = o * lax.rsqrt(jnp.mean(o * o, axis=-1, keepdims=True) + EPS) * nw
    return y * (g * _sigmoid(g))


def _rope128(x, cos, sin_signed):
    half = DK // 2
    up = pltpu.roll(x, LANE - half, axis=1)
    down = pltpu.roll(x, half, axis=1)
    first = (_iota(x.shape, 1) % DK) < half
    return x * cos + jnp.where(first, up, down) * sin_signed


def _decay_kernel(mode, C, tb, qk_ref, v_ref, g_ref, aux_ref, p1_ref, p2_ref, nw_ref, s0_ref,
                  o_ref, sout_ref, S_scr, k_scr, b_scr):
    HK = N_HEADS * DK
    tblk = pl.program_id(1)

    @pl.when(tblk == 0)
    def _():
        S_scr[...] = s0_ref[...]

    q = qk_ref[:, :HK]
    k = qk_ref[:, HK:]
    if mode == "gla":
        x = jnp.dot(aux_ref[:, LANE:2 * LANE].astype(BF16), p1_ref[...], preferred_element_type=F32) + p2_ref[...]
        la = -_softplus(-x) * (1.0 / GLA_TAU)
        q = q * (DK ** -0.5)
    else:
        cos = aux_ref[:, :HK]
        sin = aux_ref[:, HK:]
        q = jnp.concatenate([_rope128(q[:, i * LANE:(i + 1) * LANE], cos[:, i * LANE:(i + 1) * LANE],
                                      sin[:, i * LANE:(i + 1) * LANE]) for i in range(HK // LANE)], axis=1)
        k = jnp.concatenate([_rope128(k[:, i * LANE:(i + 1) * LANE], cos[:, i * LANE:(i + 1) * LANE],
                                      sin[:, i * LANE:(i + 1) * LANE]) for i in range(HK // LANE)], axis=1)
        k = k * (DK ** -0.5)

    nc = tb // C
    lane_head = _iota((C, HK), 1) // DK
    heads_rows = lambda x: jnp.concatenate([jnp.where(lane_head == h, x, 0.0) for h in range(N_HEADS)],
                                           axis=0).astype(BF16)
    diag_blocks = lambda d: jnp.concatenate([d[h * DK:(h + 1) * DK, h * DV:(h + 1) * DV] for h in range(N_HEADS)], axis=0)
    heads_lanes = lambda x: jnp.concatenate([x[h * C:(h + 1) * C, :] for h in range(N_HEADS)], axis=1)
    row = _iota((tb, tb), 0)
    col = _iota((tb, tb), 1)

    if mode == "gla":
        same = (row // C) == (col // C)
        b = _mm_split(jnp.where(same & (col <= row), 1.0, 0.0).astype(BF16), la)
        btot = _mm_split(jnp.where(same, 1.0, 0.0).astype(BF16), la)
        k_scr[...] = k
        b_scr[...] = b
        qe = q * jnp.exp(b)
        ke = k * jnp.exp(btot - b)
        sel = jnp.where(_iota((tb, nc * DV), 0) // C == _iota((tb, nc * DV), 1) // DV, 1.0, 0.0).astype(BF16)
        la_hi, la_lo = _split2(la)
        tn = (((0,), (0,)), ((), ()))
        gcols = jnp.exp(lax.dot_general(la_hi, sel, tn, preferred_element_type=F32)
                        + lax.dot_general(la_lo, sel, tn, preferred_element_type=F32))
        ind = jnp.where((_iota((HK, N_HEADS * DV), 0) // DK) == (_iota((HK, N_HEADS * DV), 1) // DV), 1.0, 0.0).astype(BF16)
        trow = _iota((C, HK), 0)
        chunks = range(nc)
        X = []
        for c in chunks:
            qc = q[c * C:(c + 1) * C]
            bc = b[c * C:(c + 1) * C]
            xs = []
            for s in range(C):
                r = c * C + s
                e = jnp.where(trow >= s, jnp.exp(jnp.minimum(bc - b_scr[r:r + 1, :], 0.0)), 0.0)
                xs.append((qc * k_scr[r:r + 1, :] * e).astype(BF16))
            X.append(jnp.concatenate(xs, axis=0))
        R = [jnp.dot(x, ind, preferred_element_type=F32) for x in X]
        o_chunks = []
        for c in chunks:
            o_c = R[c][0:C, :] * v_ref[c * C:c * C + 1, :]
            for s in range(1, C):
                o_c = o_c + R[c][s * C:(s + 1) * C, :] * v_ref[c * C + s:c * C + s + 1, :]
            o_chunks.append(o_c)
        v_rows = lambda c: jnp.concatenate([v_ref[c * C:(c + 1) * C, h * DV:(h + 1) * DV] for h in range(N_HEADS)], axis=0)
        dS = [_mm_tn(heads_rows(ke[c * C:(c + 1) * C]), v_rows(c)) for c in chunks]
        L = [heads_rows(qe[c * C:(c + 1) * C]) for c in chunks]
        S = S_scr[...]
        for c in chunks:
            o_chunks[c] = o_chunks[c] + heads_lanes(jnp.dot(L[c], S.astype(BF16), preferred_element_type=F32))
            S = S * gcols[:, c * DV:(c + 1) * DV] + dS[c]
        S_scr[...] = S
        o = jnp.concatenate(o_chunks, axis=0)
    else:
        lg = p2_ref[...]
        tcol = _iota((tb, 1), 0).astype(F32)
        qe = q * jnp.exp((tcol + 1.0) * lg)
        ke = k * jnp.exp((tb - 1.0 - tcol) * lg)
        dist = (row - col).astype(F32)
        kb = k.astype(BF16)
        Q = heads_rows(q)
        o_heads = []
        for h in range(N_HEADS):
            s = lax.dot_general(Q[h * tb:(h + 1) * tb], kb, (((1,), (1,)), ((), ())), preferred_element_type=F32)
            dec = jnp.where(row >= col, jnp.exp(jnp.minimum(dist * lg[:, h * DK:h * DK + 1], 0.0)), 0.0)
            o_heads.append(_mm(s * dec, v_ref[:, h * DV:(h + 1) * DV]))
        S = S_scr[...]
        o = jnp.concatenate(o_heads, axis=1) + heads_lanes(jnp.dot(heads_rows(qe), S.astype(BF16),
                                                                   preferred_element_type=F32))
        S_scr[...] = S * jnp.exp(tb * p1_ref[...]) + diag_blocks(_mm_tn(ke, v_ref[...]))

    for h in range(N_HEADS):
        sl = slice(h * DV, (h + 1) * DV)
        o_ref[:, sl] = _head_rms_gate(o[:, sl], g_ref[:, sl], nw_ref[...]).astype(o_ref.dtype)

    @pl.when(tblk == pl.num_programs(1) - 1)
    def _():
        sout_ref[...] = S_scr[...]


def decay_mixer(mode, cols, batch, seq, blk_qk, blk_v, blk_g, aux, aux_is_cols, p1, p2, norm_w, s0):
    tb = min(SEQ_BLOCK, seq)
    C = min(CHUNK, seq) if mode == "gla" else tb
    nt = seq // tb
    HK = N_HEADS * DK
    row_map = lambda bl: (lambda b, t: (b * nt + t, bl))
    if aux_is_cols:
        aux_spec = pl.BlockSpec((tb, GROUP_W), row_map(BLK_MISC))
    else:
        aux_spec = pl.BlockSpec((tb, aux.shape[1]), lambda b, t: (t, 0))
    full = lambda a: pl.BlockSpec(a.shape, lambda b, t: (0,) * a.ndim)
    return pl.pallas_call(
        functools.partial(_decay_kernel, mode, C, tb),
        grid=(batch, nt),
        in_specs=[pl.BlockSpec((tb, GROUP_W), row_map(blk_qk)),
                  pl.BlockSpec((tb, GROUP_W), row_map(blk_v)),
                  pl.BlockSpec((tb, GROUP_W), row_map(blk_g)),
                  aux_spec, full(p1), full(p2), full(norm_w),
                  pl.BlockSpec((None, HK, DV), lambda b, t: (b, 0, 0))],
        out_specs=[pl.BlockSpec((tb, GROUP_W), lambda b, t: (b * nt + t, 0)),
                   pl.BlockSpec((None, HK, DV), lambda b, t: (b, 0, 0))],
        out_shape=[jax.ShapeDtypeStruct((batch * seq, GROUP_W), BF16),
                   jax.ShapeDtypeStruct((batch, HK, DV), F32)],
        scratch_shapes=[pltpu.VMEM((HK, DV), F32)] + [pltpu.VMEM((tb, HK), F32)] * 2,
        compiler_params=_params(("parallel", "arbitrary")),
        name="decay_" + mode,
    )(cols, cols, cols, aux, p1, p2, norm_w, s0)


def pad_in_cols(w):
    z = jnp.zeros(w.shape[:-1] + (COLS_W - 7312,), w.dtype)
    return jnp.concatenate([w[..., 0:1024], w[..., 1040:1552], w[..., 1552:5136], w[..., 5136:6672],
                            w[..., 6800:7312], w[..., 6672:6800], w[..., 1024:1040], z], axis=-1)


def gla_mixer(cols, batch, seq, lp, s0):
    HK = N_HEADS * DK
    w_a2 = jnp.zeros((LANE, HK), F32).at[:GLA_RANK].set(lp['gla_w_a2']).astype(BF16)
    b_a = lp['gla_b_a'].reshape(1, HK)
    return decay_mixer("gla", cols, batch, seq, BLK_GLA_QK, BLK_GLA_V, BLK_GLA_G, cols, True,
                       w_a2, b_a, lp['gla_norm'].reshape(1, DV), s0)


def ret_mixer(cols, batch, seq, lp, s0, pos):
    HK = N_HEADS * DK
    half = DK // 2
    inv = ROPE_BASE ** (-jnp.arange(half, dtype=F32) / half)
    ang = pos.astype(F32)[:, None] * inv[None, :]
    cos = jnp.tile(jnp.cos(ang), (1, 2 * N_HEADS))
    sin = jnp.tile(jnp.concatenate([-jnp.sin(ang), jnp.sin(ang)], axis=1), (1, N_HEADS))
    tables = jnp.concatenate([cos, sin], axis=1)
    log_gamma = jnp.log(1.0 - 2.0 ** (-5.0 - jnp.arange(N_HEADS, dtype=F32)))
    la = jnp.repeat(log_gamma, DK)
    return decay_mixer("ret", cols, batch, seq, BLK_RET_QK, BLK_RET_V, BLK_RET_G, tables, False,
                       jnp.broadcast_to(la[:, None], (HK, DV)), la.reshape(1, HK), lp['ret_norm'].reshape(1, DV), s0)


def _t5_bucket(rel):
    n = jnp.maximum(rel, 0)
    max_exact = REL_BUCKETS // 2
    nf = jnp.maximum(n, 1).astype(F32)
    large = max_exact + (jnp.log(nf / max_exact) / math.log(REL_MAX_DIST / max_exact)
                         * (REL_BUCKETS - max_exact)).astype(jnp.int32)
    large = jnp.minimum(large, REL_BUCKETS - 1)
    return jnp.where(n < max_exact, n, large)


def _rel_bias_tile(rel_bias, rel):
    onehot = (_t5_bucket(rel)[..., None] == jnp.arange(REL_BUCKETS, dtype=jnp.int32)).astype(F32)
    b = jnp.einsum('qkb,bh->hqk', onehot, rel_bias.astype(F32), precision=HIGHEST)
    return jnp.where(rel[None] >= 0, b, NEG_BIG)


def _lambda(lam_ref, lam_init):
    s1 = jnp.sum(lam_ref[0:1, :] * lam_ref[1:2, :], axis=-1, keepdims=True)
    s2 = jnp.sum(lam_ref[2:3, :] * lam_ref[3:4, :], axis=-1, keepdims=True)
    return jnp.exp(s1) - jnp.exp(s2) + lam_init


def _diff_finish(o1, o2, lam, lam_init, g, sw):
    o = o1 - lam * o2
    y = o * lax.rsqrt(jnp.mean(o * o, axis=-1, keepdims=True) + EPS) * sw * (1.0 - lam_init)
    return y * (g * _sigmoid(g))


def _diffattn_kernel(lam_init, tq, tk, q_ref, k_ref, v_ref, g_ref, bias_ref, lam_ref, sw_ref,
                     o_ref, kb_scr, vt_scr):
    qi = pl.program_id(2)
    nh = q_ref.shape[1] // DV
    heads = range(nh)
    hs = lambda h: slice(h * DV, (h + 1) * DV)

    @pl.when(qi == 0)
    def _():
        for h in heads:
            kb_scr[h] = k_ref[:, hs(h)].astype(BF16)
            for j in range(vt_scr.shape[1]):
                vt_scr[h, j] = v_ref[j * tk:(j + 1) * tk, hs(h)].T.astype(BF16)

    first_map = _iota((tq, DV), 1) < DK
    Q = []
    for h in heads:
        q = q_ref[:, hs(h)] * (DK ** -0.5)
        Q.append(jnp.concatenate([jnp.where(first_map, q, 0.0), jnp.where(first_map, 0.0, q)], axis=0).astype(BF16))

    def step(j, carry, bias_idx):
        m, l, acc = carry
        r0 = pl.multiple_of(j * tk, tk)
        st = [lax.dot_general(kb_scr[h, pl.ds(r0, tk), :], Q[h], (((1,), (1,)), ((), ())),
                              preferred_element_type=F32) for h in heads]
        if bias_idx is not None:
            st = [st[h] + jnp.concatenate([bias_ref[h, bias_idx]] * 2, axis=1) for h in heads]
        m_new = [jnp.maximum(m[h], jnp.max(st[h], axis=0, keepdims=True)) for h in heads]
        p = [jnp.exp(st[h] - m_new[h]) for h in heads]
        alpha = [jnp.exp(m[h] - m_new[h]) for h in heads]
        l = [alpha[h] * l[h] + jnp.sum(p[h], axis=0, keepdims=True) for h in heads]
        acc = [alpha[h] * acc[h] + jnp.dot(vt_scr[h, j], p[h].astype(BF16), preferred_element_type=F32)
               for h in heads]
        return m_new, l, acc

    carry = ([jnp.full((1, 2 * tq), NEG_BIG, F32)] * nh, [jnp.zeros((1, 2 * tq), F32)] * nh,
             [jnp.zeros((DV, 2 * tq), F32)] * nh)
    first_near = jnp.maximum(qi - 1, 0)
    carry = lax.fori_loop(0, first_near, lambda j, c: step(j, c, None), carry)
    _, l, acc = lax.fori_loop(first_near, qi + 1, lambda j, c: step(j, c, j - qi + 1), carry)

    lam = _lambda(lam_ref, lam_init)
    for h in heads:
        ot = acc[h][:, :tq] / l[h][:, :tq] - lam * (acc[h][:, tq:] / l[h][:, tq:])
        yt = ot * lax.rsqrt(jnp.mean(ot * ot, axis=0, keepdims=True) + EPS) * sw_ref[...] * (1.0 - lam_init)
        g = g_ref[:, hs(h)]
        o_ref[:, hs(h)] = (yt.T * (g * _sigmoid(g))).astype(o_ref.dtype)


def prompt_bias_tiles(rel_bias, seq):
    tq = tk = min(ATTN_BLOCK, seq)
    assert tk >= REL_MAX_DIST
    d = _iota((tk, tq), 1) - _iota((tk, tq), 0)
    far = rel_bias.astype(F32)[REL_BUCKETS - 1][:, None, None]
    return jnp.stack([_rel_bias_tile(rel_bias, d + tk) - far, _rel_bias_tile(rel_bias, d) - far], axis=1)


def diff_attn_prompt(cols, batch, seq, lp, bias_tiles, lam_init):
    tq = tk = min(ATTN_BLOCK, seq)
    nq = seq // tq
    lam4 = jnp.stack([lp['diff_lam_q1'], lp['diff_lam_k1'], lp['diff_lam_q2'], lp['diff_lam_k2']]).astype(F32)
    const = lambda a: pl.BlockSpec(a.shape, lambda b, hg, i: (0,) * a.ndim)
    sw = lp['diff_subln'].reshape(DV, 1)
    nh = ATTN_HEADS
    wg = nh * DV
    per = GROUP_W // wg
    return pl.pallas_call(
        functools.partial(_diffattn_kernel, lam_init, tq, tk),
        grid=(batch, N_HEADS // nh, nq),
        in_specs=[pl.BlockSpec((tq, wg), lambda b, hg, i: (b * nq + i, BLK_DIFF_Q * per + hg)),
                  pl.BlockSpec((seq, wg), lambda b, hg, i: (b, BLK_DIFF_K * per + hg)),
                  pl.BlockSpec((seq, wg), lambda b, hg, i: (b, BLK_DIFF_V * per + hg)),
                  pl.BlockSpec((tq, wg), lambda b, hg, i: (b * nq + i, BLK_DIFF_G * per + hg)),
                  pl.BlockSpec((nh, 2, tk, tq), lambda b, hg, i: (hg, 0, 0, 0)), const(lam4), const(sw)],
        out_specs=pl.BlockSpec((tq, wg), lambda b, hg, i: (b * nq + i, hg)),
        out_shape=jax.ShapeDtypeStruct((batch * seq, GROUP_W), BF16),
        scratch_shapes=[pltpu.VMEM((nh, seq, DV), BF16), pltpu.VMEM((nh, seq // tk, DV, tk), BF16)],
        compiler_params=_params(("arbitrary", "arbitrary", "arbitrary")),
        name="diff_attn_prompt",
    )(cols, cols, cols, cols, bias_tiles, lam4, sw)


def _diffdec_kernel(lam_init, PP, ts, page, pt_ref, q_ref, kn_ref, vn_ref, g_ref, *rest):
    k_refs = rest[:PP]
    v_refs = rest[PP:2 * PP]
    bias_ref, biasn_ref, lam_ref, sw_ref, o_ref, q_scr = rest[2 * PP:2 * PP + 6]
    state = rest[2 * PP + 6:]
    m_scr, l_scr, acc_scr = state[0::3], state[1::3], state[2::3]
    NCH = len(m_scr)
    per = PP // NCH
    nkc = per * page
    step_id = pl.program_id(1)
    W = N_HEADS * DV

    @pl.when(step_id == 0)
    def _():
        q = q_ref[...] * (DK ** -0.5)
        lane = _iota(q.shape, 1)
        pieces = []
        for h in range(N_HEADS):
            for mp in range(2):
                lo = h * DV + mp * DK
                pieces.append(jnp.where((lane >= lo) & (lane < lo + DK), q, 0.0))
        q_scr[...] = jnp.concatenate(pieces, axis=0).astype(BF16)
        for c in range(NCH):
            m_scr[c][...] = jnp.full(m_scr[c].shape, NEG_BIG, F32)
            l_scr[c][...] = jnp.zeros(l_scr[c].shape, F32)
            acc_scr[c][...] = jnp.zeros(acc_scr[c].shape, F32)

    def update(chains, kb, vb, bias):
        qb = q_scr[...]
        n = range(len(chains))
        s = [lax.dot_general(qb, kb[i], (((1,), (1,)), ((), ())), preferred_element_type=F32) + bias[i] for i in n]
        m_old = [m_scr[c][...] for c in chains]
        m_new = [jnp.maximum(m_old[i], jnp.max(s[i], axis=-1, keepdims=True)) for i in n]
        p = [jnp.exp(s[i] - m_new[i]) for i in n]
        alpha = [jnp.exp(m_old[i] - m_new[i]) for i in n]
        for i, c in enumerate(chains):
            l_scr[c][...] = alpha[i] * l_scr[c][...] + jnp.sum(p[i], axis=-1, keepdims=True)
            acc_scr[c][...] = alpha[i] * acc_scr[c][...] + jnp.dot(p[i].astype(BF16), vb[i], preferred_element_type=F32)
            m_scr[c][...] = m_new[i]

    def page_rows(ref):
        return jnp.concatenate([ref[pl.ds(h, page, stride=N_HEADS), :] for h in range(N_HEADS)], axis=1).astype(BF16)

    update(list(range(NCH)),
           [jnp.concatenate([page_rows(r) for r in k_refs[c * per:(c + 1) * per]], axis=0) for c in range(NCH)],
           [jnp.concatenate([page_rows(r) for r in v_refs[c * per:(c + 1) * per]], axis=0) for c in range(NCH)],
           [bias_ref[:, c * nkc:(c + 1) * nkc] for c in range(NCH)])

    @pl.when(step_id == pl.num_programs(1) - 1)
    def _():
        pad = jnp.zeros((page - ts, W), BF16)
        update([0], [jnp.concatenate([kn_ref[...].astype(BF16), pad], axis=0)],
               [jnp.concatenate([vn_ref[...].astype(BF16), pad], axis=0)], [biasn_ref[...]])
        m = m_scr[0][...]
        for c in range(1, NCH):
            m = jnp.maximum(m, m_scr[c][...])
        w = [jnp.exp(m_scr[c][...] - m) for c in range(NCH)]
        ln_all = sum(w[c] * l_scr[c][...] for c in range(NCH))
        acc_all = sum(w[c] * acc_scr[c][...] for c in range(NCH))
        lam = _lambda(lam_ref, lam_init)
        rows = 2 * ts
        for h in range(N_HEADS):
            a = acc_all[h * rows:(h + 1) * rows, h * DV:(h + 1) * DV]
            ln = ln_all[h * rows:(h + 1) * rows, :]
            o_ref[:, h * DV:(h + 1) * DV] = _diff_finish(
                a[:ts] / ln[:ts], a[ts:] / ln[ts:], lam, lam_init,
                g_ref[:, h * DV:(h + 1) * DV], sw_ref[...]).astype(o_ref.dtype)


def sample_bias_tiles(rel_bias, ts, n_pages, page):
    n_steps = n_pages // DECODE_PAGES
    past = n_pages * page
    nk = DECODE_PAGES * page
    assert nk >= REL_MAX_DIST
    t_of_row = jnp.tile(jnp.arange(ts, dtype=jnp.int32), N_HEADS * 2)
    h_of_row = jnp.repeat(jnp.arange(N_HEADS, dtype=jnp.int32), 2 * ts)
    head_sel = h_of_row[None, :, None] == jnp.arange(N_HEADS, dtype=jnp.int32)[:, None, None]

    def rows_bias(k_pos):
        rel = past + t_of_row[:, None] - k_pos[None, :]
        return jnp.sum(jnp.where(head_sel, _rel_bias_tile(rel_bias, rel), 0.0), axis=0)

    kk = jnp.arange(nk, dtype=jnp.int32)
    bias_steps = jnp.stack([rows_bias(kk), rows_bias((n_steps - 1) * nk + kk)])
    kn = jnp.arange(page, dtype=jnp.int32)
    bias_new = jnp.where(kn[None, :] < ts, rows_bias(past + kn), NEG_BIG)
    return bias_steps, bias_new


def diff_attn_sample(cols, batch, ts, cache_k, cache_v, layer, page_table, lp, bias_tiles, lam_init):
    page = cache_k.shape[2] // N_HEADS
    W = N_HEADS * DV
    n_pages = page_table.shape[1]
    PP = DECODE_PAGES
    n_steps = n_pages // PP
    nk = PP * page
    rows = N_HEADS * 2 * ts
    bias_steps, bias_new = bias_tiles
    lam4 = jnp.stack([lp['diff_lam_q1'], lp['diff_lam_k1'], lp['diff_lam_q2'], lp['diff_lam_k2']]).astype(F32)

    def page_spec(i):
        return pl.BlockSpec((None, None, page * N_HEADS, DV),
                            lambda b, s, pt: (layer, pt[b * n_pages + s * PP + i], 0, 0))

    col_spec = lambda blk: pl.BlockSpec((ts, GROUP_W), lambda b, s, pt: (b, blk))
    const = lambda a: pl.BlockSpec(a.shape, lambda b, s, pt: (0,) * a.ndim)
    grid_spec = pltpu.PrefetchScalarGridSpec(
        num_scalar_prefetch=1,
        grid=(batch, n_steps),
        in_specs=[col_spec(BLK_DIFF_Q), col_spec(BLK_DIFF_K), col_spec(BLK_DIFF_V), col_spec(BLK_DIFF_G)]
                 + [page_spec(i) for i in range(PP)] + [page_spec(i) for i in range(PP)]
                 + [pl.BlockSpec((None, rows, nk), lambda b, s, pt: (jnp.where(s == n_steps - 1, 1, 0), 0, 0)),
                    const(bias_new), const(lam4), pl.BlockSpec((1, DV), lambda b, s, pt: (0, 0))],
        out_specs=pl.BlockSpec((ts, GROUP_W), lambda b, s, pt: (b, 0)),
        scratch_shapes=[pltpu.VMEM((rows, W), BF16)]
                       + [pltpu.VMEM((rows, 1), F32), pltpu.VMEM((rows, 1), F32), pltpu.VMEM((rows, W), F32)] * DECODE_CHAINS)
    return pl.pallas_call(
        functools.partial(_diffdec_kernel, lam_init, PP, ts, page),
        grid_spec=grid_spec,
        out_shape=jax.ShapeDtypeStruct((batch * ts, GROUP_W), BF16),
        compiler_params=_params(("parallel", "arbitrary")),
        name="diff_attn_sample",
    )(page_table.reshape(-1), cols, cols, cols, cols, *([cache_k] * PP), *([cache_v] * PP),
      bias_steps, bias_new, lam4, lp['diff_subln'].reshape(1, DV))


RWKV_PAIRS = RWKV_HEADS // 2
SHIFT_PAD = 4 * GROUP_W


def _rwkv_kernel(C, tb, r_ref, k_ref, v_ref, misc_ref, g_ref, shift_ref, s0_ref,
                 mu_ref, w0_ref, ww2_ref, a0_ref, aw2_ref, kk_ref, ka_ref, rk_ref, nw_ref,
                 o_ref, sout_ref, S_scr, carry_scr):
    G = GROUP_W
    tblk = pl.program_id(1)

    @pl.when(tblk == 0)
    def _():
        S_scr[...] = s0_ref[...]
        carry_scr[...] = shift_ref[...]

    first_row = _iota((tb, 1), 0) == 0

    def token_shift(p, lo, width):
        prev = jnp.where(first_row, carry_scr[:, lo:lo + width], pltpu.roll(p, 1, axis=0))
        carry_scr[:, lo:lo + width] = p[tb - 1:tb, :]
        return p + mu_ref[:, lo:lo + width] * (prev - p)

    r = token_shift(r_ref[...], 0, G)
    k = token_shift(k_ref[...], G, G)
    v = token_shift(v_ref[...], 2 * G, G)
    lora = token_shift(misc_ref[:, :LANE], 3 * G, LANE)

    w_pre = w0_ref[...] + jnp.dot(jnp.tanh(lora).astype(BF16), ww2_ref[...], preferred_element_type=F32)
    lw = -jnp.exp(-_softplus(-w_pre) - 0.5)
    a = _sigmoid(a0_ref[...] + jnp.dot(lora.astype(BF16), aw2_ref[...], preferred_element_type=F32))

    kk = k * kk_ref[...]
    kk = kk / jnp.maximum(jnp.sqrt(_head_sums(kk * kk)), 1e-12)
    k2 = k * (1.0 + (a - 1.0) * ka_ref[...])
    bb = kk * a
    bonus = _head_sums(r * k2 * rk_ref[...]) * v

    row = _iota((tb, tb), 0)
    col = _iota((tb, tb), 1)
    same = (row // C) == (col // C)
    cum = _mm_split(jnp.where(same & (col <= row), 1.0, 0.0).astype(BF16), lw)
    tot = _mm_split(jnp.where(same, 1.0, 0.0).astype(BF16), lw)
    inv = jnp.exp(-cum)
    rest = jnp.exp(tot - cum)
    rt = r * jnp.exp(cum)
    kt = kk * jnp.exp(cum - lw)
    kd = k2 * inv
    bd = bb * inv
    kh = k2 * rest
    bh = bb * rest

    C2 = 2 * C
    nc = tb // C
    lane0 = _iota((C, LANE), 1) < RWKV_N
    rr = _iota((C2, C2), 0)
    cc = _iota((C2, C2), 1)
    strict = (rr % C) > (cc % C)
    incl = (rr % C) >= (cc % C)
    eye = jnp.where(rr == cc, 1.0, 0.0)
    eye_lane = _iota((LANE, LANE), 0) == _iota((LANE, LANE), 1)
    n_double = max(int(math.log2(C)) - 1, 0)
    probs = [(c, p) for c in range(nc) for p in range(RWKV_PAIRS)]

    def split(x, c, p):
        xs = x[c * C:(c + 1) * C, p * LANE:(p + 1) * LANE]
        return jnp.concatenate([jnp.where(lane0, xs, 0.0), jnp.where(lane0, 0.0, xs)], axis=0).astype(BF16)

    Lk = [split(kt, c, p) for c, p in probs]
    Lr = [split(rt, c, p) for c, p in probs]
    Rk = [split(kd, c, p) for c, p in probs]
    Rb = [split(bd, c, p) for c, p in probs]
    Vb = [split(v, c, p) for c, p in probs]
    Kh = [split(kh, c, p) for c, p in probs]
    Bh = [split(bh, c, p) for c, p in probs]
    if C2 % LANE == 0:
        G = [_mm_nt(jnp.concatenate([x, y], axis=0), jnp.concatenate([z, u], axis=0))
             for x, y, z, u in zip(Lk, Lr, Rk, Rb)]
        g_kk, g_kb = [g[:C2, :C2] for g in G], [g[:C2, C2:] for g in G]
        g_rk, g_rb = [g[C2:, :C2] for g in G], [g[C2:, C2:] for g in G]
    else:
        g_kk, g_kb = [_mm_nt(x, y) for x, y in zip(Lk, Rk)], [_mm_nt(x, y) for x, y in zip(Lk, Rb)]
        g_rk, g_rb = [_mm_nt(x, y) for x, y in zip(Lr, Rk)], [_mm_nt(x, y) for x, y in zip(Lr, Rb)]
    A_kr = [jnp.concatenate([jnp.where(strict, x, 0.0), jnp.where(incl, y, 0.0)], axis=0).astype(BF16)
            for x, y in zip(g_kk, g_rk)]
    A_rb = [jnp.where(incl, x, 0.0).astype(BF16) for x in g_rb]
    P = [jnp.where(strict, -x, 0.0) for x in g_kb]
    T = [eye + x for x in P]
    if n_double > 0:
        P = [_mm(x, x) for x in P]
    for it in range(n_double):
        if it < n_double - 1:
            TP = [_mm(jnp.concatenate([t, x], axis=0), x) for t, x in zip(T, P)]
            T = [t + y[:C2] for t, y in zip(T, TP)]
            P = [y[C2:] for y in TP]
        else:
            T = [t + _mm(t, x) for t, x in zip(T, P)]
    T = [t.astype(BF16) for t in T]
    AV = [_mm(x, y) for x, y in zip(A_kr, Vb)]
    TW = [_mm(t, jnp.concatenate([x, y[:C2].astype(BF16)], axis=1)).astype(BF16)
          for t, x, y in zip(T, Lk, AV)]
    AW = [_mm(x, y) for x, y in zip(A_rb, TW)]
    Wr = [x.astype(F32) - y[:, :LANE] for x, y in zip(Lr, AW)]
    O0 = [x[C2:] - y[:, LANE:] for x, y in zip(AV, AW)]
    BW = [_mm_tn(x, y) for x, y in zip(Bh, TW)]
    D0 = [_mm_tn(x, y) - z[:, LANE:] for x, y, z in zip(Kh, Vb, BW)]
    WN = [jnp.concatenate([x, y[:, :LANE]], axis=0).astype(BF16) for x, y in zip(Wr, BW)]
    gcol = [jnp.exp(jnp.sum(jnp.where(eye_lane, tot[c * C:c * C + 1, p * LANE:(p + 1) * LANE], 0.0),
                            axis=1, keepdims=True)) for c, p in probs]

    S = [S_scr[p] for p in range(RWKV_PAIRS)]
    o_rows = []
    for c in range(nc):
        o_lanes = []
        for p in range(RWKV_PAIRS):
            i = c * RWKV_PAIRS + p
            Rm = jnp.dot(WN[i], S[p].astype(BF16), preferred_element_type=F32)
            O = Rm[:C2] + O0[i]
            o_lanes.append(O[:C] + O[C:])
            S[p] = S[p] * gcol[i] - Rm[C2:] + D0[i]
        o_rows.append(jnp.concatenate(o_lanes, axis=1))
    for p in range(RWKV_PAIRS):
        S_scr[p] = S[p]

    o = jnp.concatenate(o_rows, axis=0)
    ms = _head_sums(o * o) * (1.0 / RWKV_N)
    y = o * lax.rsqrt(ms + EPS) * nw_ref[...] + bonus
    g = g_ref[...]
    o_ref[...] = (y * (g * _sigmoid(g))).astype(o_ref.dtype)

    @pl.when(tblk == pl.num_programs(1) - 1)
    def _():
        sout_ref[...] = S_scr[...]


def rwkv_mixer(cols, batch, seq, lp, s0_pairs, shift_pad):
    C = min(64, seq)
    tb = min(SEQ_BLOCK, seq)
    nt = seq // tb
    G = GROUP_W
    mu = lp['rwkv_mu']
    mu_pad = jnp.concatenate([mu, jnp.zeros((SHIFT_PAD - mu.shape[0],), F32)]).reshape(1, SHIFT_PAD)
    ww2 = jnp.zeros((LANE, G), F32).at[:RWKV_LORA].set(lp['rwkv_w_w2']).astype(BF16)
    aw2 = jnp.zeros((LANE, G), F32).at[RWKV_LORA:].set(lp['rwkv_a_w2']).astype(BF16)
    vec = lambda a: a.reshape(1, G).astype(F32)
    params = [mu_pad, vec(lp['rwkv_w0']), ww2, vec(lp['rwkv_a0']), aw2, vec(lp['rwkv_k_k']), vec(lp['rwkv_k_a']),
              vec(lp['rwkv_r_k']), jnp.tile(lp['rwkv_norm'], RWKV_HEADS).reshape(1, G)]
    row_map = lambda bl: (lambda b, t: (b * nt + t, bl))
    full = lambda a: pl.BlockSpec(a.shape, lambda b, t: (0,) * a.ndim)
    return pl.pallas_call(
        functools.partial(_rwkv_kernel, C, tb),
        grid=(batch, nt),
        in_specs=[pl.BlockSpec((tb, G), row_map(BLK_RWKV_R)), pl.BlockSpec((tb, G), row_map(BLK_RWKV_K)),
                  pl.BlockSpec((tb, G), row_map(BLK_RWKV_V)), pl.BlockSpec((tb, G), row_map(BLK_MISC)),
                  pl.BlockSpec((tb, G), row_map(BLK_RWKV_G)),
                  pl.BlockSpec((None, 1, SHIFT_PAD), lambda b, t: (b, 0, 0)),
                  pl.BlockSpec((None, RWKV_PAIRS, LANE, LANE), lambda b, t: (b, 0, 0, 0))]
                 + [full(a) for a in params],
        out_specs=[pl.BlockSpec((tb, G), lambda b, t: (b * nt + t, 0)),
                   pl.BlockSpec((None, RWKV_PAIRS, LANE, LANE), lambda b, t: (b, 0, 0, 0))],
        out_shape=[jax.ShapeDtypeStruct((batch * seq, G), BF16),
                   jax.ShapeDtypeStruct((batch, RWKV_PAIRS, LANE, LANE), F32)],
        scratch_shapes=[pltpu.VMEM((RWKV_PAIRS, LANE, LANE), F32), pltpu.VMEM((1, SHIFT_PAD), F32)],
        compiler_params=_params(("parallel", "arbitrary")),
        name="rwkv7",
    )(cols, cols, cols, cols, cols, shift_pad, s0_pairs, *params)


def rwkv_state_to_pairs(s):
    b = s.shape[0]
    st = jnp.swapaxes(s, -1, -2).reshape(b, RWKV_PAIRS, 2, RWKV_N, RWKV_N)
    z = jnp.zeros_like(st[:, :, 0])
    top = jnp.concatenate([st[:, :, 0], z], axis=-1)
    bot = jnp.concatenate([z, st[:, :, 1]], axis=-1)
    return jnp.concatenate([top, bot], axis=-2)


def rwkv_pairs_to_state(sp):
    b = sp.shape[0]
    h0 = sp[:, :, :RWKV_N, :RWKV_N]
    h1 = sp[:, :, RWKV_N:, RWKV_N:]
    st = jnp.stack([h0, h1], axis=2).reshape(b, RWKV_HEADS, RWKV_N, RWKV_N)
    return jnp.swapaxes(st, -1, -2)


def _group_layer(x, scale, shift, gate, rows_per_mod, batch, seq, lp, w_pad, w_o_bf, final_w, final,
                 s_gla, s_ret, s_rwkv, s_shift, pos, diff_fn, tm_in, tm_out, k_all, v_all, layer):
    cols, k_all, v_all = in_proj(x, scale, shift, lp['norm_w'], w_pad, k_all, v_all, layer, rows_per_mod, tm_in, IN_TN)
    oa, n_gla = gla_mixer(cols, batch, seq, lp, s_gla)
    ob = diff_fn(cols)
    oc, n_ret = ret_mixer(cols, batch, seq, lp, s_ret, pos)
    od, n_rwkv = rwkv_mixer(cols, batch, seq, lp, s_rwkv, s_shift)
    y = out_proj(x, gate, oa, ob, oc, od, w_o_bf, final_w, rows_per_mod, tm_out, final)
    last = cols.reshape(batch, seq, COLS_W)[:, -1]
    n_shift = jnp.concatenate([last[:, BLK_RWKV_R * GROUP_W:(BLK_RWKV_V + 1) * GROUP_W],
                               last[:, BLK_MISC * GROUP_W:BLK_MISC * GROUP_W + 2 * RWKV_LORA]], axis=-1)
    return y, k_all, v_all, (n_gla.reshape(batch, N_HEADS, DK, DV), n_ret.reshape(batch, N_HEADS, DK, DV),
                             rwkv_pairs_to_state(n_rwkv), n_shift)


def kernel(x_prompt, x_sample, cache_k, cache_v, state_gla, state_ret, state_rwkv, state_shift, page_table,
           c_prompt, c_sample, rel_bias, w_ada, b_ada, norm_w, w_in, w_o, gla_w_a2, gla_b_a, gla_norm,
           diff_lam_q1, diff_lam_k1, diff_lam_q2, diff_lam_k2, diff_subln, ret_norm, rwkv_mu, rwkv_w0, rwkv_w_w2,
           rwkv_a0, rwkv_a_w2, rwkv_k_k, rwkv_k_a, rwkv_r_k, rwkv_norm, final_norm):
    Bp, Sp, D = x_prompt.shape
    Bs, Ts, _ = x_sample.shape
    depth = w_in.shape[0]
    n_pool, page = cache_k.shape[1], cache_k.shape[2]
    past_len = page_table.shape[1] * page
    ck = cache_k.reshape(depth, n_pool, page * N_HEADS, DV)
    cv = cache_v.reshape(depth, n_pool, page * N_HEADS, DV)
    pos_p = jnp.arange(Sp, dtype=jnp.int32)
    pos_s = past_len + jnp.arange(Ts, dtype=jnp.int32)
    c_all = jnp.concatenate([c_prompt, c_sample], axis=0)
    xp = x_prompt.reshape(Bp * Sp, D)
    xs = x_sample.reshape(Bs * Ts, D)
    zeros_dec = jnp.zeros((Bp, N_HEADS * DK, DV), F32)
    zeros_rwkv = jnp.zeros((Bp, RWKV_PAIRS, LANE, LANE), F32)
    zeros_shift = jnp.zeros((Bp, 1, SHIFT_PAD), F32)
    bias_p = prompt_bias_tiles(rel_bias, Sp)
    bias_s = sample_bias_tiles(rel_bias, Ts, page_table.shape[1], page)
    kp = jnp.zeros((depth, Bp * Sp, GROUP_W), F32)
    vp = jnp.zeros((depth, Bp * Sp, GROUP_W), F32)
    ks = jnp.zeros((depth, Bs * Ts, GROUP_W), F32)
    vs = jnp.zeros((depth, Bs * Ts, GROUP_W), F32)
    new_p, new_s = [], []
    for l in range(depth):
        lp = dict(norm_w=norm_w[l], gla_w_a2=gla_w_a2[l], gla_b_a=gla_b_a[l], gla_norm=gla_norm[l],
                  diff_lam_q1=diff_lam_q1[l], diff_lam_k1=diff_lam_k1[l], diff_lam_q2=diff_lam_q2[l],
                  diff_lam_k2=diff_lam_k2[l], diff_subln=diff_subln[l], ret_norm=ret_norm[l],
                  rwkv_mu=rwkv_mu[l], rwkv_w0=rwkv_w0[l], rwkv_w_w2=rwkv_w_w2[l], rwkv_a0=rwkv_a0[l],
                  rwkv_a_w2=rwkv_a_w2[l], rwkv_k_k=rwkv_k_k[l], rwkv_k_a=rwkv_k_a[l], rwkv_r_k=rwkv_r_k[l],
                  rwkv_norm=rwkv_norm[l])
        final = l == depth - 1
        lam_init = 0.8 - 0.6 * math.exp(-0.3 * l)
        mod = ada_mod(c_all, w_ada[l], b_ada[l])
        shift, scale, gate = mod[:, :D], mod[:, D:2 * D], mod[:, 2 * D:]
        w_pad = pad_in_cols(w_in[l].astype(BF16))
        w_o_bf = w_o[l].astype(BF16)

        per_batch = lambda a: a[:Bp].reshape(Bp, 1, D)
        xp, kp, vp, st_p = _group_layer(
            xp, per_batch(scale), per_batch(shift), per_batch(gate), Sp, Bp, Sp, lp, w_pad, w_o_bf, final_norm, final,
            zeros_dec, zeros_dec, zeros_rwkv, zeros_shift, pos_p,
            lambda cols: diff_attn_prompt(cols, Bp, Sp, lp, bias_p, lam_init), min(IN_TM, Sp), min(OUT_TM, Sp),
            kp, vp, l)

        per_row = lambda a: jnp.repeat(a[Bp:], Ts, axis=0)
        sh = state_shift[l]
        sh_pad = jnp.concatenate([sh, jnp.zeros((Bs, SHIFT_PAD - sh.shape[1]), F32)], axis=1).reshape(Bs, 1, SHIFT_PAD)
        xs, ks, vs, st_s = _group_layer(
            xs, per_row(scale), per_row(shift), per_row(gate), 1, Bs, Ts, lp, w_pad, w_o_bf, final_norm, final,
            state_gla[l].reshape(Bs, N_HEADS * DK, DV), state_ret[l].reshape(Bs, N_HEADS * DK, DV),
            rwkv_state_to_pairs(state_rwkv[l]), sh_pad, pos_s,
            lambda cols: diff_attn_sample(cols, Bs, Ts, ck, cv, l, page_table, lp, bias_s, lam_init),
            Bs * Ts, Bs * Ts, ks, vs, l)
        new_p.append(st_p)
        new_s.append(st_s)
    stack = lambda states, i: jnp.stack([s[i] for s in states])
    heads = lambda a, b, t: a.reshape(depth, b, t, N_HEADS, DV)
    return (xp.reshape(Bp, Sp, D), xs.reshape(Bs, Ts, D),
            heads(kp, Bp, Sp), heads(vp, Bp, Sp), heads(ks, Bs, Ts), heads(vs, Bs, Ts),
            stack(new_p, 0), stack(new_s, 0), stack(new_p, 1), stack(new_s, 1),
            stack(new_p, 2), stack(new_s, 2), stack(new_p, 3), stack(new_s, 3))
```

```python
import functools
import math

import jax
import jax.numpy as jnp
from jax import lax
from jax.experimental import pallas as pl
from jax.experimental.pallas import tpu as pltpu

F32 = jnp.float32
BF16 = jnp.bfloat16
HIGHEST = lax.Precision.HIGHEST

D_MODEL = 2048
GROUP_W = 512
N_HEADS = 4
DK = 64
DV = 128
GLA_RANK = 16
GLA_TAU = 16.0
RWKV_N = 64
RWKV_HEADS = 8
RWKV_LORA = 64
ROPE_BASE = 10000.0
REL_BUCKETS = 32
REL_MAX_DIST = 128
CHUNK = 16
EPS = 1e-6
NEG_BIG = -1e30

LANE = 128
VMEM_LIMIT = 48 * 1024 * 1024
ATTN_BLOCK = 256
ATTN_HEADS = 4
DECODE_PAGES = 16
DECODE_CHAINS = 2
SEQ_BLOCK = 256
IN_TM, IN_TN = 1024, 1280
IN_VMEM_LIMIT = 56 * 1024 * 1024
NORM_ROWS = 128
OUT_TM = 256

COLS_W = 15 * GROUP_W
BLK_GLA_QK, BLK_GLA_V, BLK_GLA_G = 0, 1, 2
BLK_DIFF_Q, BLK_DIFF_K, BLK_DIFF_V, BLK_DIFF_G = 3, 4, 5, 6
BLK_RET_QK, BLK_RET_V, BLK_RET_G = 7, 8, 9
BLK_RWKV_R, BLK_RWKV_K, BLK_RWKV_V, BLK_RWKV_G = 10, 11, 12, 13
BLK_MISC = 14


def _mm(a, b):
    return jnp.dot(a.astype(BF16), b.astype(BF16), preferred_element_type=F32)


def _mm_nt(a, b):
    return lax.dot_general(a.astype(BF16), b.astype(BF16), (((1,), (1,)), ((), ())), preferred_element_type=F32)


def _mm_tn(a, b):
    return lax.dot_general(a.astype(BF16), b.astype(BF16), (((0,), (0,)), ((), ())), preferred_element_type=F32)


def _mm_exact(a, b):
    return jnp.dot(a, b, precision=HIGHEST, preferred_element_type=F32)


def _mm_tn_exact(a, b):
    return lax.dot_general(a, b, (((0,), (0,)), ((), ())), precision=HIGHEST, preferred_element_type=F32)


def _split2(x):
    hi = x.astype(BF16)
    return hi, (x - hi.astype(F32)).astype(BF16)


def _mm_split(a, x):
    hi, lo = _split2(x)
    return jnp.dot(a, hi, preferred_element_type=F32) + jnp.dot(a, lo, preferred_element_type=F32)


def _head_sums(x):
    ind = jnp.where((_iota((LANE, LANE), 0) // RWKV_N) == (_iota((LANE, LANE), 1) // RWKV_N), 1.0, 0.0).astype(BF16)
    hi, lo = _split2(x)
    parts = []
    for i in range(x.shape[1] // LANE):
        sl = slice(i * LANE, (i + 1) * LANE)
        parts.append(jnp.dot(hi[:, sl], ind, preferred_element_type=F32) + jnp.dot(lo[:, sl], ind, preferred_element_type=F32))
    return jnp.concatenate(parts, axis=1)


def _sigmoid(x):
    return 1.0 / (1.0 + jnp.exp(-x))


def _softplus(x):
    return jnp.maximum(x, 0.0) + jnp.log1p(jnp.exp(-jnp.abs(x)))


def _iota(shape, dim):
    return lax.broadcasted_iota(jnp.int32, shape, dim)


def _params(sem):
    return pltpu.CompilerParams(dimension_semantics=sem, vmem_limit_bytes=VMEM_LIMIT)


def _ada_kernel(c_ref, w_ref, b_ref, o_ref):
    c = c_ref[...]
    o_ref[...] = _mm(c * _sigmoid(c), w_ref[...]) + b_ref[...]


def ada_mod(c, w_ada, b_ada, layer, tn=512):
    rows, d = c.shape
    depth, _, n = w_ada.shape
    return pl.pallas_call(
        _ada_kernel,
        grid=(n // tn,),
        in_specs=[pl.BlockSpec((rows, d), lambda j: (0, 0)),
                  pl.BlockSpec((None, d, tn), lambda j: (layer, 0, j)),
                  pl.BlockSpec((None, 1, tn), lambda j: (layer, 0, j))],
        out_specs=pl.BlockSpec((rows, tn), lambda j: (0, j)),
        out_shape=jax.ShapeDtypeStruct((rows, n), F32),
        compiler_params=_params(("arbitrary",)),
        name="ada_mod",
    )(c, w_ada, b_ada.reshape(depth, 1, n))


def _inproj_kernel(k_tile, k_lo, v_tile, v_lo, x_ref, scale_ref, shift_ref, nw_ref, w_ref, kin_ref, vin_ref,
                   o_ref, k_ref, v_ref, h_ref):
    del kin_ref, vin_ref
    j = pl.program_id(1)

    tm = o_ref.shape[0]
    rows = min(NORM_ROWS, tm)
    per_row_mod = scale_ref.shape[0] == tm

    @pl.when(j == 0)
    def _():
        def norm_rows(i, carry):
            sl = pl.ds(pl.multiple_of(i * rows, rows), rows)
            x = x_ref[sl, :]
            y = x * lax.rsqrt(jnp.mean(x * x, axis=-1, keepdims=True) + EPS) * nw_ref[...]
            scale = scale_ref[sl, :] if per_row_mod else scale_ref[...]
            shift = shift_ref[sl, :] if per_row_mod else shift_ref[...]
            h_ref[sl, :] = (y * (1.0 + scale) + shift).astype(BF16)
            return carry

        lax.fori_loop(0, tm // rows, norm_rows, 0)

    o_ref[...] = jnp.dot(h_ref[...], w_ref[...], preferred_element_type=F32)

    def heads_to_rows(dst_ref, lo):
        for h in range(N_HEADS):
            dst_ref[pl.ds(h, tm, stride=N_HEADS), :] = o_ref[:, lo + h * DV:lo + (h + 1) * DV]

    @pl.when(j == k_tile)
    def _():
        heads_to_rows(k_ref, k_lo)

    @pl.when(j == v_tile)
    def _():
        heads_to_rows(v_ref, v_lo)


def in_proj(x, scale, shift, norm_w, w_pad, k_all, v_all, layer, rows_per_mod, tm, tn):
    m, d = x.shape
    n = w_pad.shape[2]
    k_lo, v_lo = BLK_DIFF_K * GROUP_W, BLK_DIFF_V * GROUP_W
    k_tile, v_tile = k_lo // tn, v_lo // tn
    assert (k_lo + GROUP_W - 1) // tn == k_tile and (v_lo + GROUP_W - 1) // tn == v_tile
    if rows_per_mod == 1:
        mod_spec = pl.BlockSpec((tm, d), lambda i, j: (i, 0))
    else:
        per = rows_per_mod // tm
        mod_spec = pl.BlockSpec((None, 1, d), lambda i, j: (i // per, 0, 0))
    kv_spec = pl.BlockSpec((None, tm * N_HEADS, DV), lambda i, j: (layer, i, 0), pipeline_mode=pl.Buffered(1))
    return pl.pallas_call(
        functools.partial(_inproj_kernel, k_tile, k_lo - k_tile * tn, v_tile, v_lo - v_tile * tn),
        grid=(m // tm, n // tn),
        in_specs=[pl.BlockSpec((tm, d), lambda i, j: (i, 0)),
                  mod_spec, mod_spec,
                  pl.BlockSpec((1, d), lambda i, j: (0, 0)),
                  pl.BlockSpec((None, d, tn), lambda i, j: (layer, 0, j)),
                  pl.BlockSpec(memory_space=pl.ANY), pl.BlockSpec(memory_space=pl.ANY)],
        out_specs=[pl.BlockSpec((tm, tn), lambda i, j: (i, j)), kv_spec, kv_spec],
        out_shape=[jax.ShapeDtypeStruct((m, n), F32), jax.ShapeDtypeStruct(k_all.shape, F32),
                   jax.ShapeDtypeStruct(v_all.shape, F32)],
        input_output_aliases={5: 1, 6: 2},
        scratch_shapes=[pltpu.VMEM((tm, d), BF16)],
        compiler_params=pltpu.CompilerParams(dimension_semantics=("parallel", "arbitrary"),
                                             vmem_limit_bytes=IN_VMEM_LIMIT),
        name="in_proj",
    )(x, scale, shift, norm_w.reshape(1, d), w_pad, k_all, v_all)


def _outproj_kernel(final, x_ref, gate_ref, oa_ref, ob_ref, oc_ref, od_ref, w_ref, fw_ref, y_ref):
    acc = jnp.dot(oa_ref[...], w_ref[0 * GROUP_W:1 * GROUP_W, :], preferred_element_type=F32)
    acc += jnp.dot(ob_ref[...], w_ref[1 * GROUP_W:2 * GROUP_W, :], preferred_element_type=F32)
    acc += jnp.dot(oc_ref[...], w_ref[2 * GROUP_W:3 * GROUP_W, :], preferred_element_type=F32)
    acc += jnp.dot(od_ref[...], w_ref[3 * GROUP_W:4 * GROUP_W, :], preferred_element_type=F32)
    y = x_ref[...] + gate_ref[...] * acc
    if final:
        y = y * lax.rsqrt(jnp.mean(y * y, axis=-1, keepdims=True) + EPS) * fw_ref[...]
    y_ref[...] = y


def out_proj(x, gate, oa, ob, oc, od, w_o, layer, final_w, rows_per_mod, tm, final):
    m, d = x.shape
    if rows_per_mod == 1:
        mod_spec = pl.BlockSpec((tm, d), lambda i: (i, 0))
    else:
        per = rows_per_mod // tm
        mod_spec = pl.BlockSpec((None, 1, d), lambda i: (i // per, 0, 0))
    o_spec = pl.BlockSpec((tm, GROUP_W), lambda i: (i, 0))
    return pl.pallas_call(
        functools.partial(_outproj_kernel, final),
        grid=(m // tm,),
        in_specs=[pl.BlockSpec((tm, d), lambda i: (i, 0)), mod_spec,
                  o_spec, o_spec, o_spec, o_spec,
                  pl.BlockSpec((None, d, d), lambda i: (layer, 0, 0)),
                  pl.BlockSpec((1, d), lambda i: (0, 0))],
        out_specs=pl.BlockSpec((tm, d), lambda i: (i, 0)),
        out_shape=jax.ShapeDtypeStruct((m, d), F32),
        compiler_params=_params(("parallel",)),
        name="out_proj",
    )(x, gate, oa, ob, oc, od, w_o, final_w.reshape(1, d))


def _head_rms_gate(o, g, nw):
    y = o * lax.rsqrt(jnp.mean(o * o, axis=-1, keepdims=True) + EPS) * nw
    return y * (g * _sigmoid(g))


def _rope128(x, cos, sin_signed):
    half = DK // 2
    up = pltpu.roll(x, LANE - half, axis=1)
    down = pltpu.roll(x, half, axis=1)
    first = (_iota(x.shape, 1) % DK) < half
    return x * cos + jnp.where(first, up, down) * sin_signed


def _decay_kernel(mode, C, tb, qk_ref, v_ref, g_ref, aux_ref, p1_ref, p2_ref, nw_ref, s0_ref,
                  o_ref, sout_ref, S_scr, k_scr, b_scr):
    HK = N_HEADS * DK
    tblk = pl.program_id(1)

    @pl.when(tblk == 0)
    def _():
        S_scr[...] = s0_ref[...]

    q = qk_ref[:, :HK]
    k = qk_ref[:, HK:]
    if mode == "gla":
        x = jnp.dot(aux_ref[:, LANE:2 * LANE].astype(BF16), p1_ref[...], preferred_element_type=F32) + p2_ref[...]
        la = -_softplus(-x) * (1.0 / GLA_TAU)
        q = q * (DK ** -0.5)
    else:
        cos = aux_ref[:, :HK]
        sin = aux_ref[:, HK:]
        q = jnp.concatenate([_rope128(q[:, i * LANE:(i + 1) * LANE], cos[:, i * LANE:(i + 1) * LANE],
                                      sin[:, i * LANE:(i + 1) * LANE]) for i in range(HK // LANE)], axis=1)
        k = jnp.concatenate([_rope128(k[:, i * LANE:(i + 1) * LANE], cos[:, i * LANE:(i + 1) * LANE],
                                      sin[:, i * LANE:(i + 1) * LANE]) for i in range(HK // LANE)], axis=1)
        k = k * (DK ** -0.5)

    nc = tb // C
    lane_head = _iota((C, HK), 1) // DK
    heads_rows = lambda x: jnp.concatenate([jnp.where(lane_head == h, x, 0.0) for h in range(N_HEADS)],
                                           axis=0).astype(BF16)
    diag_blocks = lambda d: jnp.concatenate([d[h * DK:(h + 1) * DK, h * DV:(h + 1) * DV] for h in range(N_HEADS)], axis=0)
    heads_lanes = lambda x: jnp.concatenate([x[h * C:(h + 1) * C, :] for h in range(N_HEADS)], axis=1)
    row = _iota((tb, tb), 0)
    col = _iota((tb, tb), 1)

    if mode == "gla":
        same = (row // C) == (col // C)
        b = _mm_split(jnp.where(same & (col <= row), 1.0, 0.0).astype(BF16), la)
        btot = _mm_split(jnp.where(same, 1.0, 0.0).astype(BF16), la)
        k_scr[...] = k
        b_scr[...] = b
        qe = q * jnp.exp(b)
        ke = k * jnp.exp(btot - b)
        sel = jnp.where(_iota((tb, nc * DV), 0) // C == _iota((tb, nc * DV), 1) // DV, 1.0, 0.0).astype(BF16)
        la_hi, la_lo = _split2(la)
        tn = (((0,), (0,)), ((), ()))
        gcols = jnp.exp(lax.dot_general(la_hi, sel, tn, preferred_element_type=F32)
                        + lax.dot_general(la_lo, sel, tn, preferred_element_type=F32))
        ind = jnp.where((_iota((HK, N_HEADS * DV), 0) // DK) == (_iota((HK, N_HEADS * DV), 1) // DV), 1.0, 0.0).astype(BF16)
        trow = _iota((C, HK), 0)
        chunks = range(nc)
        X = []
        for c in chunks:
            qc = q[c * C:(c + 1) * C]
            bc = b[c * C:(c + 1) * C]
            xs = []
            for s in range(C):
                r = c * C + s
                e = jnp.where(trow >= s, jnp.exp(jnp.minimum(bc - b_scr[r:r + 1, :], 0.0)), 0.0)
                xs.append((qc * k_scr[r:r + 1, :] * e).astype(BF16))
            X.append(jnp.concatenate(xs, axis=0))
        R = [jnp.dot(x, ind, preferred_element_type=F32) for x in X]
        o_chunks = []
        for c in chunks:
            o_c = R[c][0:C, :] * v_ref[c * C:c * C + 1, :]
            for s in range(1, C):
                o_c = o_c + R[c][s * C:(s + 1) * C, :] * v_ref[c * C + s:c * C + s + 1, :]
            o_chunks.append(o_c)
        v_rows = lambda c: jnp.concatenate([v_ref[c * C:(c + 1) * C, h * DV:(h + 1) * DV] for h in range(N_HEADS)], axis=0)
        dS = [_mm_tn(heads_rows(ke[c * C:(c + 1) * C]), v_rows(c)) for c in chunks]
        L = [heads_rows(qe[c * C:(c + 1) * C]) for c in chunks]
        S = S_scr[...]
        for c in chunks:
            o_chunks[c] = o_chunks[c] + heads_lanes(jnp.dot(L[c], S.astype(BF16), preferred_element_type=F32))
            S = S * gcols[:, c * DV:(c + 1) * DV] + dS[c]
        S_scr[...] = S
        o = jnp.concatenate(o_chunks, axis=0)
    else:
        lg = p2_ref[...]
        tcol = _iota((tb, 1), 0).astype(F32)
        qe = q * jnp.exp((tcol + 1.0) * lg)
        ke = k * jnp.exp((tb - 1.0 - tcol) * lg)
        dist = (row - col).astype(F32)
        kb = k.astype(BF16)
        Q = heads_rows(q)
        o_heads = []
        for h in range(N_HEADS):
            s = lax.dot_general(Q[h * tb:(h + 1) * tb], kb, (((1,), (1,)), ((), ())), preferred_element_type=F32)
            dec = jnp.where(row >= col, jnp.exp(jnp.minimum(dist * lg[:, h * DK:h * DK + 1], 0.0)), 0.0)
            o_heads.append(_mm(s * dec, v_ref[:, h * DV:(h + 1) * DV]))
        S = S_scr[...]
        o = jnp.concatenate(o_heads, axis=1) + heads_lanes(jnp.dot(heads_rows(qe), S.astype(BF16),
                                                                   preferred_element_type=F32))
        S_scr[...] = S * jnp.exp(tb * p1_ref[...]) + diag_blocks(_mm_tn(ke, v_ref[...]))

    for h in range(N_HEADS):
        sl = slice(h * DV, (h + 1) * DV)
        o_ref[:, sl] = _head_rms_gate(o[:, sl], g_ref[:, sl], nw_ref[...]).astype(o_ref.dtype)

    @pl.when(tblk == pl.num_programs(1) - 1)
    def _():
        sout_ref[...] = S_scr[...]


def decay_mixer(mode, cols, batch, seq, blk_qk, blk_v, blk_g, aux, aux_is_cols, p1, p2, norm_w, s0):
    tb = min(SEQ_BLOCK, seq)
    C = min(CHUNK, seq) if mode == "gla" else tb
    nt = seq // tb
    HK = N_HEADS * DK
    row_map = lambda bl: (lambda b, t: (b * nt + t, bl))
    if aux_is_cols:
        aux_spec = pl.BlockSpec((tb, GROUP_W), row_map(BLK_MISC))
    else:
        aux_spec = pl.BlockSpec((tb, aux.shape[1]), lambda b, t: (t, 0))
    full = lambda a: pl.BlockSpec(a.shape, lambda b, t: (0,) * a.ndim)
    return pl.pallas_call(
        functools.partial(_decay_kernel, mode, C, tb),
        grid=(batch, nt),
        in_specs=[pl.BlockSpec((tb, GROUP_W), row_map(blk_qk)),
                  pl.BlockSpec((tb, GROUP_W), row_map(blk_v)),
                  pl.BlockSpec((tb, GROUP_W), row_map(blk_g)),
                  aux_spec, full(p1), full(p2), full(norm_w),
                  pl.BlockSpec((None, HK, DV), lambda b, t: (b, 0, 0))],
        out_specs=[pl.BlockSpec((tb, GROUP_W), lambda b, t: (b * nt + t, 0)),
                   pl.BlockSpec((None, HK, DV), lambda b, t: (b, 0, 0))],
        out_shape=[jax.ShapeDtypeStruct((batch * seq, GROUP_W), BF16),
                   jax.ShapeDtypeStruct((batch, HK, DV), F32)],
        scratch_shapes=[pltpu.VMEM((HK, DV), F32)] + [pltpu.VMEM((tb, HK), F32)] * 2,
        compiler_params=_params(("parallel", "arbitrary")),
        name="decay_" + mode,
    )(cols, cols, cols, aux, p1, p2, norm_w, s0)


def pad_in_cols(w):
    z = jnp.zeros(w.shape[:-1] + (COLS_W - 7312,), w.dtype)
    return jnp.concatenate([w[..., 0:1024], w[..., 1040:1552], w[..., 1552:5136], w[..., 5136:6672],
                            w[..., 6800:7312], w[..., 6672:6800], w[..., 1024:1040], z], axis=-1)


def gla_mixer(cols, batch, seq, lp, s0):
    HK = N_HEADS * DK
    w_a2 = jnp.zeros((LANE, HK), F32).at[:GLA_RANK].set(lp['gla_w_a2']).astype(BF16)
    b_a = lp['gla_b_a'].reshape(1, HK)
    return decay_mixer("gla", cols, batch, seq, BLK_GLA_QK, BLK_GLA_V, BLK_GLA_G, cols, True,
                       w_a2, b_a, lp['gla_norm'].reshape(1, DV), s0)


def ret_mixer(cols, batch, seq, lp, s0, pos):
    HK = N_HEADS * DK
    half = DK // 2
    inv = ROPE_BASE ** (-jnp.arange(half, dtype=F32) / half)
    ang = pos.astype(F32)[:, None] * inv[None, :]
    cos = jnp.tile(jnp.cos(ang), (1, 2 * N_HEADS))
    sin = jnp.tile(jnp.concatenate([-jnp.sin(ang), jnp.sin(ang)], axis=1), (1, N_HEADS))
    tables = jnp.concatenate([cos, sin], axis=1)
    log_gamma = jnp.log(1.0 - 2.0 ** (-5.0 - jnp.arange(N_HEADS, dtype=F32)))
    la = jnp.repeat(log_gamma, DK)
    return decay_mixer("ret", cols, batch, seq, BLK_RET_QK, BLK_RET_V, BLK_RET_G, tables, False,
                       jnp.broadcast_to(la[:, None], (HK, DV)), la.reshape(1, HK), lp['ret_norm'].reshape(1, DV), s0)


def _t5_bucket(rel):
    n = jnp.maximum(rel, 0)
    max_exact = REL_BUCKETS // 2
    nf = jnp.maximum(n, 1).astype(F32)
    large = max_exact + (jnp.log(nf / max_exact) / math.log(REL_MAX_DIST / max_exact)
                         * (REL_BUCKETS - max_exact)).astype(jnp.int32)
    large = jnp.minimum(large, REL_BUCKETS - 1)
    return jnp.where(n < max_exact, n, large)


def _rel_bias_tile(rel_bias, rel):
    onehot = (_t5_bucket(rel)[..., None] == jnp.arange(REL_BUCKETS, dtype=jnp.int32)).astype(F32)
    b = jnp.einsum('qkb,bh->hqk', onehot, rel_bias.astype(F32), precision=HIGHEST)
    return jnp.where(rel[None] >= 0, b, NEG_BIG)


def _lambda(lam_ref, lam_init):
    s1 = jnp.sum(lam_ref[0:1, :] * lam_ref[1:2, :], axis=-1, keepdims=True)
    s2 = jnp.sum(lam_ref[2:3, :] * lam_ref[3:4, :], axis=-1, keepdims=True)
    return jnp.exp(s1) - jnp.exp(s2) + lam_init


def _diff_finish(o1, o2, lam, lam_init, g, sw):
    o = o1 - lam * o2
    y = o * lax.rsqrt(jnp.mean(o * o, axis=-1, keepdims=True) + EPS) * sw * (1.0 - lam_init)
    return y * (g * _sigmoid(g))


def _diffattn_kernel(lam_init, tq, tk, q_ref, k_ref, v_ref, g_ref, bias_ref, lam_ref, sw_ref,
                     o_ref, kb_scr, vt_scr):
    qi = pl.program_id(2)
    nh = q_ref.shape[1] // DV
    heads = range(nh)
    hs = lambda h: slice(h * DV, (h + 1) * DV)

    @pl.when(qi == 0)
    def _():
        for h in heads:
            kb_scr[h] = k_ref[:, hs(h)].astype(BF16)
            for j in range(vt_scr.shape[1]):
                vt_scr[h, j] = v_ref[j * tk:(j + 1) * tk, hs(h)].T.astype(BF16)

    first_map = _iota((tq, DV), 1) < DK
    Q = []
    for h in heads:
        q = q_ref[:, hs(h)] * (DK ** -0.5)
        Q.append(jnp.concatenate([jnp.where(first_map, q, 0.0), jnp.where(first_map, 0.0, q)], axis=0).astype(BF16))

    def step(j, carry, bias_idx):
        m, l, acc = carry
        r0 = pl.multiple_of(j * tk, tk)
        st = [lax.dot_general(kb_scr[h, pl.ds(r0, tk), :], Q[h], (((1,), (1,)), ((), ())),
                              preferred_element_type=F32) for h in heads]
        if bias_idx is not None:
            st = [st[h] + jnp.concatenate([bias_ref[h, bias_idx]] * 2, axis=1) for h in heads]
        m_new = [jnp.maximum(m[h], jnp.max(st[h], axis=0, keepdims=True)) for h in heads]
        p = [jnp.exp(st[h] - m_new[h]) for h in heads]
        alpha = [jnp.exp(m[h] - m_new[h]) for h in heads]
        l = [alpha[h] * l[h] + jnp.sum(p[h], axis=0, keepdims=True) for h in heads]
        acc = [alpha[h] * acc[h] + jnp.dot(vt_scr[h, j], p[h].astype(BF16), preferred_element_type=F32)
               for h in heads]
        return m_new, l, acc

    carry = ([jnp.full((1, 2 * tq), NEG_BIG, F32)] * nh, [jnp.zeros((1, 2 * tq), F32)] * nh,
             [jnp.zeros((DV, 2 * tq), F32)] * nh)
    first_near = jnp.maximum(qi - 1, 0)
    carry = lax.fori_loop(0, first_near, lambda j, c: step(j, c, None), carry)
    _, l, acc = lax.fori_loop(first_near, qi + 1, lambda j, c: step(j, c, j - qi + 1), carry)

    lam = _lambda(lam_ref, lam_init)
    for h in heads:
        ot = acc[h][:, :tq] / l[h][:, :tq] - lam * (acc[h][:, tq:] / l[h][:, tq:])
        yt = ot * lax.rsqrt(jnp.mean(ot * ot, axis=0, keepdims=True) + EPS) * sw_ref[...] * (1.0 - lam_init)
        g = g_ref[:, hs(h)]
        o_ref[:, hs(h)] = (yt.T * (g * _sigmoid(g))).astype(o_ref.dtype)


def prompt_bias_tiles(rel_bias, seq):
    tq = tk = min(ATTN_BLOCK, seq)
    assert tk >= REL_MAX_DIST
    d = _iota((tk, tq), 1) - _iota((tk, tq), 0)
    far = rel_bias.astype(F32)[REL_BUCKETS - 1][:, None, None]
    return jnp.stack([_rel_bias_tile(rel_bias, d + tk) - far, _rel_bias_tile(rel_bias, d) - far], axis=1)


def diff_attn_prompt(cols, batch, seq, lp, bias_tiles, lam_init):
    tq = tk = min(ATTN_BLOCK, seq)
    nq = seq // tq
    lam4 = jnp.stack([lp['diff_lam_q1'], lp['diff_lam_k1'], lp['diff_lam_q2'], lp['diff_lam_k2']]).astype(F32)
    const = lambda a: pl.BlockSpec(a.shape, lambda b, hg, i: (0,) * a.ndim)
    sw = lp['diff_subln'].reshape(DV, 1)
    nh = ATTN_HEADS
    wg = nh * DV
    per = GROUP_W // wg
    return pl.pallas_call(
        functools.partial(_diffattn_kernel, lam_init, tq, tk),
        grid=(batch, N_HEADS // nh, nq),
        in_specs=[pl.BlockSpec((tq, wg), lambda b, hg, i: (b * nq + i, BLK_DIFF_Q * per + hg)),
                  pl.BlockSpec((seq, wg), lambda b, hg, i: (b, BLK_DIFF_K * per + hg)),
                  pl.BlockSpec((seq, wg), lambda b, hg, i: (b, BLK_DIFF_V * per + hg)),
                  pl.BlockSpec((tq, wg), lambda b, hg, i: (b * nq + i, BLK_DIFF_G * per + hg)),
                  pl.BlockSpec((nh, 2, tk, tq), lambda b, hg, i: (hg, 0, 0, 0)), const(lam4), const(sw)],
        out_specs=pl.BlockSpec((tq, wg), lambda b, hg, i: (b * nq + i, hg)),
        out_shape=jax.ShapeDtypeStruct((batch * seq, GROUP_W), BF16),
        scratch_shapes=[pltpu.VMEM((nh, seq, DV), BF16), pltpu.VMEM((nh, seq // tk, DV, tk), BF16)],
        compiler_params=_params(("arbitrary", "arbitrary", "arbitrary")),
        name="diff_attn_prompt",
    )(cols, cols, cols, cols, bias_tiles, lam4, sw)


def _diffdec_kernel(lam_init, PP, ts, page, pt_ref, q_ref, kn_ref, vn_ref, g_ref, *rest):
    k_refs = rest[:PP]
    v_refs = rest[PP:2 * PP]
    bias_ref, biasn_ref, lam_ref, sw_ref, o_ref, q_scr = rest[2 * PP:2 * PP + 6]
    state = rest[2 * PP + 6:]
    m_scr, l_scr, acc_scr = state[0::3], state[1::3], state[2::3]
    NCH = len(m_scr)
    per = PP // NCH
    nkc = per * page
    step_id = pl.program_id(1)
    W = N_HEADS * DV

    @pl.when(step_id == 0)
    def _():
        q = q_ref[...] * (DK ** -0.5)
        lane = _iota(q.shape, 1)
        pieces = []
        for h in range(N_HEADS):
            for mp in range(2):
                lo = h * DV + mp * DK
                pieces.append(jnp.where((lane >= lo) & (lane < lo + DK), q, 0.0))
        q_scr[...] = jnp.concatenate(pieces, axis=0).astype(BF16)
        for c in range(NCH):
            m_scr[c][...] = jnp.full(m_scr[c].shape, NEG_BIG, F32)
            l_scr[c][...] = jnp.zeros(l_scr[c].shape, F32)
            acc_scr[c][...] = jnp.zeros(acc_scr[c].shape, F32)

    def update(chains, kb, vb, bias):
        qb = q_scr[...]
        n = range(len(chains))
        s = [lax.dot_general(qb, kb[i], (((1,), (1,)), ((), ())), preferred_element_type=F32) + bias[i] for i in n]
        m_old = [m_scr[c][...] for c in chains]
        m_new = [jnp.maximum(m_old[i], jnp.max(s[i], axis=-1, keepdims=True)) for i in n]
        p = [jnp.exp(s[i] - m_new[i]) for i in n]
        alpha = [jnp.exp(m_old[i] - m_new[i]) for i in n]
        for i, c in enumerate(chains):
            l_scr[c][...] = alpha[i] * l_scr[c][...] + jnp.sum(p[i], axis=-1, keepdims=True)
            acc_scr[c][...] = alpha[i] * acc_scr[c][...] + jnp.dot(p[i].astype(BF16), vb[i], preferred_element_type=F32)
            m_scr[c][...] = m_new[i]

    def page_rows(ref):
        return jnp.concatenate([ref[pl.ds(h, page, stride=N_HEADS), :] for h in range(N_HEADS)], axis=1).astype(BF16)

    update(list(range(NCH)),
           [jnp.concatenate([page_rows(r) for r in k_refs[c * per:(c + 1) * per]], axis=0) for c in range(NCH)],
           [jnp.concatenate([page_rows(r) for r in v_refs[c * per:(c + 1) * per]], axis=0) for c in range(NCH)],
           [bias_ref[:, c * nkc:(c + 1) * nkc] for c in range(NCH)])

    @pl.when(step_id == pl.num_programs(1) - 1)
    def _():
        pad = jnp.zeros((page - ts, W), BF16)
        update([0], [jnp.concatenate([kn_ref[...].astype(BF16), pad], axis=0)],
               [jnp.concatenate([vn_ref[...].astype(BF16), pad], axis=0)], [biasn_ref[...]])
        m = m_scr[0][...]
        for c in range(1, NCH):
            m = jnp.maximum(m, m_scr[c][...])
        w = [jnp.exp(m_scr[c][...] - m) for c in range(NCH)]
        ln_all = sum(w[c] * l_scr[c][...] for c in range(NCH))
        acc_all = sum(w[c] * acc_scr[c][...] for c in range(NCH))
        lam = _lambda(lam_ref, lam_init)
        rows = 2 * ts
        for h in range(N_HEADS):
            a = acc_all[h * rows:(h + 1) * rows, h * DV:(h + 1) * DV]
            ln = ln_all[h * rows:(h + 1) * rows, :]
            o_ref[:, h * DV:(h + 1) * DV] = _diff_finish(
                a[:ts] / ln[:ts], a[ts:] / ln[ts:], lam, lam_init,
                g_ref[:, h * DV:(h + 1) * DV], sw_ref[...]).astype(o_ref.dtype)


def sample_bias_tiles(rel_bias, ts, n_pages, page):
    n_steps = n_pages // DECODE_PAGES
    past = n_pages * page
    nk = DECODE_PAGES * page
    assert nk >= REL_MAX_DIST
    t_of_row = jnp.tile(jnp.arange(ts, dtype=jnp.int32), N_HEADS * 2)
    h_of_row = jnp.repeat(jnp.arange(N_HEADS, dtype=jnp.int32), 2 * ts)
    head_sel = h_of_row[None, :, None] == jnp.arange(N_HEADS, dtype=jnp.int32)[:, None, None]

    def rows_bias(k_pos):
        rel = past + t_of_row[:, None] - k_pos[None, :]
        return jnp.sum(jnp.where(head_sel, _rel_bias_tile(rel_bias, rel), 0.0), axis=0)

    kk = jnp.arange(nk, dtype=jnp.int32)
    bias_steps = jnp.stack([rows_bias(kk), rows_bias((n_steps - 1) * nk + kk)])
    kn = jnp.arange(page, dtype=jnp.int32)
    bias_new = jnp.where(kn[None, :] < ts, rows_bias(past + kn), NEG_BIG)
    return bias_steps, bias_new


def diff_attn_sample(cols, batch, ts, cache_k, cache_v, layer, page_table, lp, bias_tiles, lam_init):
    page = cache_k.shape[2] // N_HEADS
    W = N_HEADS * DV
    n_pages = page_table.shape[1]
    PP = DECODE_PAGES
    n_steps = n_pages // PP
    nk = PP * page
    rows = N_HEADS * 2 * ts
    bias_steps, bias_new = bias_tiles
    lam4 = jnp.stack([lp['diff_lam_q1'], lp['diff_lam_k1'], lp['diff_lam_q2'], lp['diff_lam_k2']]).astype(F32)

    def page_spec(i):
        return pl.BlockSpec((None, None, page * N_HEADS, DV),
                            lambda b, s, pt: (layer, pt[b * n_pages + s * PP + i], 0, 0))

    col_spec = lambda blk: pl.BlockSpec((ts, GROUP_W), lambda b, s, pt: (b, blk))
    const = lambda a: pl.BlockSpec(a.shape, lambda b, s, pt: (0,) * a.ndim)
    grid_spec = pltpu.PrefetchScalarGridSpec(
        num_scalar_prefetch=1,
        grid=(batch, n_steps),
        in_specs=[col_spec(BLK_DIFF_Q), col_spec(BLK_DIFF_K), col_spec(BLK_DIFF_V), col_spec(BLK_DIFF_G)]
                 + [page_spec(i) for i in range(PP)] + [page_spec(i) for i in range(PP)]
                 + [pl.BlockSpec((None, rows, nk), lambda b, s, pt: (jnp.where(s == n_steps - 1, 1, 0), 0, 0)),
                    const(bias_new), const(lam4), pl.BlockSpec((1, DV), lambda b, s, pt: (0, 0))],
        out_specs=pl.BlockSpec((ts, GROUP_W), lambda b, s, pt: (b, 0)),
        scratch_shapes=[pltpu.VMEM((rows, W), BF16)]
                       + [pltpu.VMEM((rows, 1), F32), pltpu.VMEM((rows, 1), F32), pltpu.VMEM((rows, W), F32)] * DECODE_CHAINS)
    return pl.pallas_call(
        functools.partial(_diffdec_kernel, lam_init, PP, ts, page),
        grid_spec=grid_spec,
        out_shape=jax.ShapeDtypeStruct((batch * ts, GROUP_W), BF16),
        compiler_params=_params(("parallel", "arbitrary")),
        name="diff_attn_sample",
    )(page_table.reshape(-1), cols, cols, cols, cols, *([cache_k] * PP), *([cache_v] * PP),
      bias_steps, bias_new, lam4, lp['diff_subln'].reshape(1, DV))


RWKV_PAIRS = RWKV_HEADS // 2
SHIFT_PAD = 4 * GROUP_W


def _rwkv_kernel(C, tb, r_ref, k_ref, v_ref, misc_ref, g_ref, shift_ref, s0_ref,
                 mu_ref, w0_ref, ww2_ref, a0_ref, aw2_ref, kk_ref, ka_ref, rk_ref, nw_ref,
                 o_ref, sout_ref, S_scr, carry_scr):
    G = GROUP_W
    tblk = pl.program_id(1)

    @pl.when(tblk == 0)
    def _():
        S_scr[...] = s0_ref[...]
        carry_scr[...] = shift_ref[...]

    first_row = _iota((tb, 1), 0) == 0

    def token_shift(p, lo, width):
        prev = jnp.where(first_row, carry_scr[:, lo:lo + width], pltpu.roll(p, 1, axis=0))
        carry_scr[:, lo:lo + width] = p[tb - 1:tb, :]
        return p + mu_ref[:, lo:lo + width] * (prev - p)

    r = token_shift(r_ref[...], 0, G)
    k = token_shift(k_ref[...], G, G)
    v = token_shift(v_ref[...], 2 * G, G)
    lora = token_shift(misc_ref[:, :LANE], 3 * G, LANE)

    w_pre = w0_ref[...] + jnp.dot(jnp.tanh(lora).astype(BF16), ww2_ref[...], preferred_element_type=F32)
    lw = -jnp.exp(-_softplus(-w_pre) - 0.5)
    a = _sigmoid(a0_ref[...] + jnp.dot(lora.astype(BF16), aw2_ref[...], preferred_element_type=F32))

    kk = k * kk_ref[...]
    kk = kk / jnp.maximum(jnp.sqrt(_head_sums(kk * kk)), 1e-12)
    k2 = k * (1.0 + (a - 1.0) * ka_ref[...])
    bb = kk * a
    bonus = _head_sums(r * k2 * rk_ref[...]) * v

    row = _iota((tb, tb), 0)
    col = _iota((tb, tb), 1)
    same = (row // C) == (col // C)
    cum = _mm_split(jnp.where(same & (col <= row), 1.0, 0.0).astype(BF16), lw)
    tot = _mm_split(jnp.where(same, 1.0, 0.0).astype(BF16), lw)
    inv = jnp.exp(-cum)
    rest = jnp.exp(tot - cum)
    rt = r * jnp.exp(cum)
    kt = kk * jnp.exp(cum - lw)
    kd = k2 * inv
    bd = bb * inv
    kh = k2 * rest
    bh = bb * rest

    C2 = 2 * C
    nc = tb // C
    lane0 = _iota((C, LANE), 1) < RWKV_N
    rr = _iota((C2, C2), 0)
    cc = _iota((C2, C2), 1)
    strict = (rr % C) > (cc % C)
    incl = (rr % C) >= (cc % C)
    eye = jnp.where(rr == cc, 1.0, 0.0)
    eye_lane = _iota((LANE, LANE), 0) == _iota((LANE, LANE), 1)
    n_double = max(int(math.log2(C)) - 1, 0)
    probs = [(c, p) for c in range(nc) for p in range(RWKV_PAIRS)]

    def split(x, c, p):
        xs = x[c * C:(c + 1) * C, p * LANE:(p + 1) * LANE]
        return jnp.concatenate([jnp.where(lane0, xs, 0.0), jnp.where(lane0, 0.0, xs)], axis=0).astype(BF16)

    Lk = [split(kt, c, p) for c, p in probs]
    Lr = [split(rt, c, p) for c, p in probs]
    Rk = [split(kd, c, p) for c, p in probs]
    Rb = [split(bd, c, p) for c, p in probs]
    Vb = [split(v, c, p) for c, p in probs]
    Kh = [split(kh, c, p) for c, p in probs]
    Bh = [split(bh, c, p) for c, p in probs]
    if C2 % LANE == 0:
        G = [_mm_nt(jnp.concatenate([x, y], axis=0), jnp.concatenate([z, u], axis=0))
             for x, y, z, u in zip(Lk, Lr, Rk, Rb)]
        g_kk, g_kb = [g[:C2, :C2] for g in G], [g[:C2, C2:] for g in G]
        g_rk, g_rb = [g[C2:, :C2] for g in G], [g[C2:, C2:] for g in G]
    else:
        g_kk, g_kb = [_mm_nt(x, y) for x, y in zip(Lk, Rk)], [_mm_nt(x, y) for x, y in zip(Lk, Rb)]
        g_rk, g_rb = [_mm_nt(x, y) for x, y in zip(Lr, Rk)], [_mm_nt(x, y) for x, y in zip(Lr, Rb)]
    A_kr = [jnp.concatenate([jnp.where(strict, x, 0.0), jnp.where(incl, y, 0.0)], axis=0).astype(BF16)
            for x, y in zip(g_kk, g_rk)]
    A_rb = [jnp.where(incl, x, 0.0).astype(BF16) for x in g_rb]
    P = [jnp.where(strict, -x, 0.0) for x in g_kb]
    T = [eye + x for x in P]
    if n_double > 0:
        P = [_mm(x, x) for x in P]
    for it in range(n_double):
        if it < n_double - 1:
            TP = [_mm(jnp.concatenate([t, x], axis=0), x) for t, x in zip(T, P)]
            T = [t + y[:C2] for t, y in zip(T, TP)]
            P = [y[C2:] for y in TP]
        else:
            T = [t + _mm(t, x) for t, x in zip(T, P)]
    T = [t.astype(BF16) for t in T]
    AV = [_mm(x, y) for x, y in zip(A_kr, Vb)]
    TW = [_mm(t, jnp.concatenate([x, y[:C2].astype(BF16)], axis=1)).astype(BF16)
          for t, x, y in zip(T, Lk, AV)]
    AW = [_mm(x, y) for x, y in zip(A_rb, TW)]
    Wr = [x.astype(F32) - y[:, :LANE] for x, y in zip(Lr, AW)]
    O0 = [x[C2:] - y[:, LANE:] for x, y in zip(AV, AW)]
    BW = [_mm_tn(x, y) for x, y in zip(Bh, TW)]
    D0 = [_mm_tn(x, y) - z[:, LANE:] for x, y, z in zip(Kh, Vb, BW)]
    WN = [jnp.concatenate([x, y[:, :LANE]], axis=0).astype(BF16) for x, y in zip(Wr, BW)]
    gcol = [jnp.exp(jnp.sum(jnp.where(eye_lane, tot[c * C:c * C + 1, p * LANE:(p + 1) * LANE], 0.0),
                            axis=1, keepdims=True)) for c, p in probs]

    S = [S_scr[p] for p in range(RWKV_PAIRS)]
    o_rows = []
    for c in range(nc):
        o_lanes = []
        for p in range(RWKV_PAIRS):
            i = c * RWKV_PAIRS + p
            Rm = jnp.dot(WN[i], S[p].astype(BF16), preferred_element_type=F32)
            O = Rm[:C2] + O0[i]
            o_lanes.append(O[:C] + O[C:])
            S[p] = S[p] * gcol[i] - Rm[C2:] + D0[i]
        o_rows.append(jnp.concatenate(o_lanes, axis=1))
    for p in range(RWKV_PAIRS):
        S_scr[p] = S[p]

    o = jnp.concatenate(o_rows, axis=0)
    ms = _head_sums(o * o) * (1.0 / RWKV_N)
    y = o * lax.rsqrt(ms + EPS) * nw_ref[...] + bonus
    g = g_ref[...]
    o_ref[...] = (y * (g * _sigmoid(g))).astype(o_ref.dtype)

    @pl.when(tblk == pl.num_programs(1) - 1)
    def _():
        sout_ref[...] = S_scr[...]


def rwkv_mixer(cols, batch, seq, lp, s0_pairs, shift_pad):
    C = min(64, seq)
    tb = min(SEQ_BLOCK, seq)
    nt = seq // tb
    G = GROUP_W
    mu = lp['rwkv_mu']
    mu_pad = jnp.concatenate([mu, jnp.zeros((SHIFT_PAD - mu.shape[0],), F32)]).reshape(1, SHIFT_PAD)
    ww2 = jnp.zeros((LANE, G), F32).at[:RWKV_LORA].set(lp['rwkv_w_w2']).astype(BF16)
    aw2 = jnp.zeros((LANE, G), F32).at[RWKV_LORA:].set(lp['rwkv_a_w2']).astype(BF16)
    vec = lambda a: a.reshape(1, G).astype(F32)
    params = [mu_pad, vec(lp['rwkv_w0']), ww2, vec(lp['rwkv_a0']), aw2, vec(lp['rwkv_k_k']), vec(lp['rwkv_k_a']),
              vec(lp['rwkv_r_k']), jnp.tile(lp['rwkv_norm'], RWKV_HEADS).reshape(1, G)]
    row_map = lambda bl: (lambda b, t: (b * nt + t, bl))
    full = lambda a: pl.BlockSpec(a.shape, lambda b, t: (0,) * a.ndim)
    return pl.pallas_call(
        functools.partial(_rwkv_kernel, C, tb),
        grid=(batch, nt),
        in_specs=[pl.BlockSpec((tb, G), row_map(BLK_RWKV_R)), pl.BlockSpec((tb, G), row_map(BLK_RWKV_K)),
                  pl.BlockSpec((tb, G), row_map(BLK_RWKV_V)), pl.BlockSpec((tb, G), row_map(BLK_MISC)),
                  pl.BlockSpec((tb, G), row_map(BLK_RWKV_G)),
                  pl.BlockSpec((None, 1, SHIFT_PAD), lambda b, t: (b, 0, 0)),
                  pl.BlockSpec((None, RWKV_PAIRS, LANE, LANE), lambda b, t: (b, 0, 0, 0))]
                 + [full(a) for a in params],
        out_specs=[pl.BlockSpec((tb, G), lambda b, t: (b * nt + t, 0)),
                   pl.BlockSpec((None, RWKV_PAIRS, LANE, LANE), lambda b, t: (b, 0, 0, 0))],
        out_shape=[jax.ShapeDtypeStruct((batch * seq, G), BF16),
                   jax.ShapeDtypeStruct((batch, RWKV_PAIRS, LANE, LANE), F32)],
        scratch_shapes=[pltpu.VMEM((RWKV_PAIRS, LANE, LANE), F32), pltpu.VMEM((1, SHIFT_PAD), F32)],
        compiler_params=_params(("parallel", "arbitrary")),
        name="rwkv7",
    )(cols, cols, cols, cols, cols, shift_pad, s0_pairs, *params)


def rwkv_state_to_pairs(s):
    b = s.shape[0]
    st = jnp.swapaxes(s, -1, -2).reshape(b, RWKV_PAIRS, 2, RWKV_N, RWKV_N)
    z = jnp.zeros_like(st[:, :, 0])
    top = jnp.concatenate([st[:, :, 0], z], axis=-1)
    bot = jnp.concatenate([z, st[:, :, 1]], axis=-1)
    return jnp.concatenate([top, bot], axis=-2)


def rwkv_pairs_to_state(sp):
    b = sp.shape[0]
    h0 = sp[:, :, :RWKV_N, :RWKV_N]
    h1 = sp[:, :, RWKV_N:, RWKV_N:]
    st = jnp.stack([h0, h1], axis=2).reshape(b, RWKV_HEADS, RWKV_N, RWKV_N)
    return jnp.swapaxes(st, -1, -2)


def _group_layer(x, scale, shift, gate, rows_per_mod, batch, seq, lp, w_pad, w_o_bf, final_w, final,
                 s_gla, s_ret, s_rwkv, s_shift, pos, diff_fn, tm_in, tm_out, k_all, v_all, layer):
    cols, k_all, v_all = in_proj(x, scale, shift, lp['norm_w'], w_pad, k_all, v_all, layer, rows_per_mod, tm_in, IN_TN)
    oa, n_gla = gla_mixer(cols, batch, seq, lp, s_gla)
    ob = diff_fn(cols)
    oc, n_ret = ret_mixer(cols, batch, seq, lp, s_ret, pos)
    od, n_rwkv = rwkv_mixer(cols, batch, seq, lp, s_rwkv, s_shift)
    y = out_proj(x, gate, oa, ob, oc, od, w_o_bf, layer, final_w, rows_per_mod, tm_out, final)
    last = cols.reshape(batch, seq, COLS_W)[:, -1]
    n_shift = jnp.concatenate([last[:, BLK_RWKV_R * GROUP_W:(BLK_RWKV_V + 1) * GROUP_W],
                               last[:, BLK_MISC * GROUP_W:BLK_MISC * GROUP_W + 2 * RWKV_LORA]], axis=-1)
    return y, k_all, v_all, (n_gla.reshape(batch, N_HEADS, DK, DV), n_ret.reshape(batch, N_HEADS, DK, DV),
                             rwkv_pairs_to_state(n_rwkv), n_shift)


def kernel(x_prompt, x_sample, cache_k, cache_v, state_gla, state_ret, state_rwkv, state_shift, page_table,
           c_prompt, c_sample, rel_bias, w_ada, b_ada, norm_w, w_in, w_o, gla_w_a2, gla_b_a, gla_norm,
           diff_lam_q1, diff_lam_k1, diff_lam_q2, diff_lam_k2, diff_subln, ret_norm, rwkv_mu, rwkv_w0, rwkv_w_w2,
           rwkv_a0, rwkv_a_w2, rwkv_k_k, rwkv_k_a, rwkv_r_k, rwkv_norm, final_norm):
    Bp, Sp, D = x_prompt.shape
    Bs, Ts, _ = x_sample.shape
    depth = w_in.shape[0]
    n_pool, page = cache_k.shape[1], cache_k.shape[2]
    past_len = page_table.shape[1] * page
    ck = cache_k.reshape(depth, n_pool, page * N_HEADS, DV)
    cv = cache_v.reshape(depth, n_pool, page * N_HEADS, DV)
    pos_p = jnp.arange(Sp, dtype=jnp.int32)
    pos_s = past_len + jnp.arange(Ts, dtype=jnp.int32)
    c_all = jnp.concatenate([c_prompt, c_sample], axis=0)
    xp = x_prompt.reshape(Bp * Sp, D)
    xs = x_sample.reshape(Bs * Ts, D)
    zeros_dec = jnp.zeros((Bp, N_HEADS * DK, DV), F32)
    zeros_rwkv = jnp.zeros((Bp, RWKV_PAIRS, LANE, LANE), F32)
    zeros_shift = jnp.zeros((Bp, 1, SHIFT_PAD), F32)
    bias_p = prompt_bias_tiles(rel_bias, Sp)
    bias_s = sample_bias_tiles(rel_bias, Ts, page_table.shape[1], page)
    kp = jnp.zeros((depth, Bp * Sp * N_HEADS, DV), F32)
    vp = jnp.zeros((depth, Bp * Sp * N_HEADS, DV), F32)
    ks = jnp.zeros((depth, Bs * Ts * N_HEADS, DV), F32)
    vs = jnp.zeros((depth, Bs * Ts * N_HEADS, DV), F32)
    w_pad_all = pad_in_cols(w_in.astype(BF16))
    w_o_all = w_o.astype(BF16)
    new_p, new_s = [], []
    for l in range(depth):
        lp = dict(norm_w=norm_w[l], gla_w_a2=gla_w_a2[l], gla_b_a=gla_b_a[l], gla_norm=gla_norm[l],
                  diff_lam_q1=diff_lam_q1[l], diff_lam_k1=diff_lam_k1[l], diff_lam_q2=diff_lam_q2[l],
                  diff_lam_k2=diff_lam_k2[l], diff_subln=diff_subln[l], ret_norm=ret_norm[l],
                  rwkv_mu=rwkv_mu[l], rwkv_w0=rwkv_w0[l], rwkv_w_w2=rwkv_w_w2[l], rwkv_a0=rwkv_a0[l],
                  rwkv_a_w2=rwkv_a_w2[l], rwkv_k_k=rwkv_k_k[l], rwkv_k_a=rwkv_k_a[l], rwkv_r_k=rwkv_r_k[l],
                  rwkv_norm=rwkv_norm[l])
        final = l == depth - 1
        lam_init = 0.8 - 0.6 * math.exp(-0.3 * l)
        mod = ada_mod(c_all, w_ada, b_ada, l)
        shift, scale, gate = mod[:, :D], mod[:, D:2 * D], mod[:, 2 * D:]
        w_pad, w_o_bf = w_pad_all, w_o_all

        per_batch = lambda a: a[:Bp].reshape(Bp, 1, D)
        xp, kp, vp, st_p = _group_layer(
            xp, per_batch(scale), per_batch(shift), per_batch(gate), Sp, Bp, Sp, lp, w_pad, w_o_bf, final_norm, final,
            zeros_dec, zeros_dec, zeros_rwkv, zeros_shift, pos_p,
            lambda cols: diff_attn_prompt(cols, Bp, Sp, lp, bias_p, lam_init), min(IN_TM, Sp), min(OUT_TM, Sp),
            kp, vp, l)

        per_row = lambda a: jnp.repeat(a[Bp:], Ts, axis=0)
        sh = state_shift[l]
        sh_pad = jnp.concatenate([sh, jnp.zeros((Bs, SHIFT_PAD - sh.shape[1]), F32)], axis=1).reshape(Bs, 1, SHIFT_PAD)
        xs, ks, vs, st_s = _group_layer(
            xs, per_row(scale), per_row(shift), per_row(gate), 1, Bs, Ts, lp, w_pad, w_o_bf, final_norm, final,
            state_gla[l].reshape(Bs, N_HEADS * DK, DV), state_ret[l].reshape(Bs, N_HEADS * DK, DV),
            rwkv_state_to_pairs(state_rwkv[l]), sh_pad, pos_s,
            lambda cols: diff_attn_sample(cols, Bs, Ts, ck, cv, l, page_table, lp, bias_s, lam_init),
            Bs * Ts, Bs * Ts, ks, vs, l)
        new_p.append(st_p)
        new_s.append(st_s)
    stack = lambda states, i: jnp.stack([s[i] for s in states])
    heads = lambda a, b, t: a.reshape(depth, b, t, N_HEADS, DV)
    return (xp.reshape(Bp, Sp, D), xs.reshape(Bs, Ts, D),
            heads(kp, Bp, Sp), heads(vp, Bp, Sp), heads(ks, Bs, Ts), heads(vs, Bs, Ts),
            stack(new_p, 0), stack(new_s, 0), stack(new_p, 1), stack(new_s, 1),
            stack(new_p, 2), stack(new_s, 2), stack(new_p, 3), stack(new_s, 3))
```

```python
import functools
import math

import jax
import jax.numpy as jnp
from jax import lax
from jax.experimental import pallas as pl
from jax.experimental.pallas import tpu as pltpu

F32 = jnp.float32
BF16 = jnp.bfloat16
HIGHEST = lax.Precision.HIGHEST

D_MODEL = 2048
GROUP_W = 512
N_HEADS = 4
DK = 64
DV = 128
GLA_RANK = 16
GLA_TAU = 16.0
RWKV_N = 64
RWKV_HEADS = 8
RWKV_LORA = 64
ROPE_BASE = 10000.0
REL_BUCKETS = 32
REL_MAX_DIST = 128
CHUNK = 16
GLA_DIAG = 8
EPS = 1e-6
NEG_BIG = -1e30
LOG2E = math.log2(math.e)

LANE = 128
VMEM_LIMIT = 48 * 1024 * 1024
ATTN_BLOCK = 256
ATTN_HEADS = 4
DECODE_PAGES = 16
DECODE_CHAINS = 2
SEQ_BLOCK = 256
IN_TM, IN_TN = 1024, 1280
IN_VMEM_LIMIT = 56 * 1024 * 1024
NORM_ROWS = 128
OUT_TM = 256

COLS_W = 15 * GROUP_W
BLK_GLA_QK, BLK_GLA_V, BLK_GLA_G = 0, 1, 2
BLK_DIFF_Q, BLK_DIFF_K, BLK_DIFF_V, BLK_DIFF_G = 3, 4, 5, 6
BLK_RET_QK, BLK_RET_V, BLK_RET_G = 7, 8, 9
BLK_RWKV_R, BLK_RWKV_K, BLK_RWKV_V, BLK_RWKV_G = 10, 11, 12, 13
BLK_MISC = 14


def _mm(a, b):
    return jnp.dot(a.astype(BF16), b.astype(BF16), preferred_element_type=F32)


def _mm_nt(a, b):
    return lax.dot_general(a.astype(BF16), b.astype(BF16), (((1,), (1,)), ((), ())), preferred_element_type=F32)


def _mm_tn(a, b):
    return lax.dot_general(a.astype(BF16), b.astype(BF16), (((0,), (0,)), ((), ())), preferred_element_type=F32)


def _mm_exact(a, b):
    return jnp.dot(a, b, precision=HIGHEST, preferred_element_type=F32)


def _mm_tn_exact(a, b):
    return lax.dot_general(a, b, (((0,), (0,)), ((), ())), precision=HIGHEST, preferred_element_type=F32)


def _split2(x):
    hi = x.astype(BF16)
    return hi, (x - hi.astype(F32)).astype(BF16)


def _mm_split(a, x):
    hi, lo = _split2(x)
    return jnp.dot(a, hi, preferred_element_type=F32) + jnp.dot(a, lo, preferred_element_type=F32)


def _head_sums(x):
    ind = jnp.where((_iota((LANE, LANE), 0) // RWKV_N) == (_iota((LANE, LANE), 1) // RWKV_N), 1.0, 0.0).astype(BF16)
    hi, lo = _split2(x)
    parts = []
    for i in range(x.shape[1] // LANE):
        sl = slice(i * LANE, (i + 1) * LANE)
        parts.append(jnp.dot(hi[:, sl], ind, preferred_element_type=F32) + jnp.dot(lo[:, sl], ind, preferred_element_type=F32))
    return jnp.concatenate(parts, axis=1)


def _sigmoid(x):
    return 1.0 / (1.0 + jnp.exp(-x))


def _softplus(x):
    return jnp.maximum(x, 0.0) + jnp.log1p(jnp.exp(-jnp.abs(x)))


def _iota(shape, dim):
    return lax.broadcasted_iota(jnp.int32, shape, dim)


def _params(sem):
    return pltpu.CompilerParams(dimension_semantics=sem, vmem_limit_bytes=VMEM_LIMIT)


def _ada_kernel(c_ref, w_ref, b_ref, o_ref):
    c = c_ref[...]
    o_ref[...] = _mm(c * _sigmoid(c), w_ref[...]) + b_ref[...]


def ada_mod(c, w_ada, b_ada, layer, tn=512):
    rows, d = c.shape
    depth, _, n = w_ada.shape
    return pl.pallas_call(
        _ada_kernel,
        grid=(n // tn,),
        in_specs=[pl.BlockSpec((rows, d), lambda j: (0, 0)),
                  pl.BlockSpec((None, d, tn), lambda j: (layer, 0, j)),
                  pl.BlockSpec((None, 1, tn), lambda j: (layer, 0, j))],
        out_specs=pl.BlockSpec((rows, tn), lambda j: (0, j)),
        out_shape=jax.ShapeDtypeStruct((rows, n), F32),
        compiler_params=_params(("arbitrary",)),
        name="ada_mod",
    )(c, w_ada, b_ada.reshape(depth, 1, n))


def _inproj_kernel(k_tile, k_lo, v_tile, v_lo, x_ref, scale_ref, shift_ref, nw_ref, w_ref, kin_ref, vin_ref,
                   o_ref, k_ref, v_ref, h_ref):
    del kin_ref, vin_ref
    j = pl.program_id(1)

    tm = o_ref.shape[0]
    rows = min(NORM_ROWS, tm)
    per_row_mod = scale_ref.shape[0] == tm

    @pl.when(j == 0)
    def _():
        def norm_rows(i, carry):
            sl = pl.ds(pl.multiple_of(i * rows, rows), rows)
            x = x_ref[sl, :]
            y = x * lax.rsqrt(jnp.mean(x * x, axis=-1, keepdims=True) + EPS) * nw_ref[...]
            scale = scale_ref[sl, :] if per_row_mod else scale_ref[...]
            shift = shift_ref[sl, :] if per_row_mod else shift_ref[...]
            h_ref[sl, :] = (y * (1.0 + scale) + shift).astype(BF16)
            return carry

        lax.fori_loop(0, tm // rows, norm_rows, 0)

    o_ref[...] = jnp.dot(h_ref[...], w_ref[...], preferred_element_type=F32)

    def heads_to_rows(dst_ref, lo):
        for h in range(N_HEADS):
            dst_ref[pl.ds(h, tm, stride=N_HEADS), :] = o_ref[:, lo + h * DV:lo + (h + 1) * DV]

    @pl.when(j == k_tile)
    def _():
        heads_to_rows(k_ref, k_lo)

    @pl.when(j == v_tile)
    def _():
        heads_to_rows(v_ref, v_lo)


def in_proj(x, scale, shift, norm_w, w_pad, k_all, v_all, layer, rows_per_mod, tm, tn):
    m, d = x.shape
    n = w_pad.shape[2]
    k_lo, v_lo = BLK_DIFF_K * GROUP_W, BLK_DIFF_V * GROUP_W
    k_tile, v_tile = k_lo // tn, v_lo // tn
    assert (k_lo + GROUP_W - 1) // tn == k_tile and (v_lo + GROUP_W - 1) // tn == v_tile
    if rows_per_mod == 1:
        mod_spec = pl.BlockSpec((tm, d), lambda i, j: (i, 0))
    else:
        per = rows_per_mod // tm
        mod_spec = pl.BlockSpec((None, 1, d), lambda i, j: (i // per, 0, 0))
    kv_spec = pl.BlockSpec((None, tm * N_HEADS, DV), lambda i, j: (layer, i, 0), pipeline_mode=pl.Buffered(1))
    return pl.pallas_call(
        functools.partial(_inproj_kernel, k_tile, k_lo - k_tile * tn, v_tile, v_lo - v_tile * tn),
        grid=(m // tm, n // tn),
        in_specs=[pl.BlockSpec((tm, d), lambda i, j: (i, 0)),
                  mod_spec, mod_spec,
                  pl.BlockSpec((1, d), lambda i, j: (0, 0)),
                  pl.BlockSpec((None, d, tn), lambda i, j: (layer, 0, j)),
                  pl.BlockSpec(memory_space=pl.ANY), pl.BlockSpec(memory_space=pl.ANY)],
        out_specs=[pl.BlockSpec((tm, tn), lambda i, j: (i, j)), kv_spec, kv_spec],
        out_shape=[jax.ShapeDtypeStruct((m, n), F32), jax.ShapeDtypeStruct(k_all.shape, F32),
                   jax.ShapeDtypeStruct(v_all.shape, F32)],
        input_output_aliases={5: 1, 6: 2},
        scratch_shapes=[pltpu.VMEM((tm, d), BF16)],
        compiler_params=pltpu.CompilerParams(dimension_semantics=("parallel", "arbitrary"),
                                             vmem_limit_bytes=IN_VMEM_LIMIT),
        name="in_proj",
    )(x, scale, shift, norm_w.reshape(1, d), w_pad, k_all, v_all)


def _outproj_kernel(final, x_ref, gate_ref, oa_ref, ob_ref, oc_ref, od_ref, w_ref, fw_ref, y_ref):
    acc = jnp.dot(oa_ref[...], w_ref[0 * GROUP_W:1 * GROUP_W, :], preferred_element_type=F32)
    acc += jnp.dot(ob_ref[...], w_ref[1 * GROUP_W:2 * GROUP_W, :], preferred_element_type=F32)
    acc += jnp.dot(oc_ref[...], w_ref[2 * GROUP_W:3 * GROUP_W, :], preferred_element_type=F32)
    acc += jnp.dot(od_ref[...], w_ref[3 * GROUP_W:4 * GROUP_W, :], preferred_element_type=F32)
    y = x_ref[...] + gate_ref[...] * acc
    if final:
        y = y * lax.rsqrt(jnp.mean(y * y, axis=-1, keepdims=True) + EPS) * fw_ref[...]
    y_ref[...] = y


def out_proj(x, gate, oa, ob, oc, od, w_o, layer, final_w, rows_per_mod, tm, final):
    m, d = x.shape
    if rows_per_mod == 1:
        mod_spec = pl.BlockSpec((tm, d), lambda i: (i, 0))
    else:
        per = rows_per_mod // tm
        mod_spec = pl.BlockSpec((None, 1, d), lambda i: (i // per, 0, 0))
    o_spec = pl.BlockSpec((tm, GROUP_W), lambda i: (i, 0))
    return pl.pallas_call(
        functools.partial(_outproj_kernel, final),
        grid=(m // tm,),
        in_specs=[pl.BlockSpec((tm, d), lambda i: (i, 0)), mod_spec,
                  o_spec, o_spec, o_spec, o_spec,
                  pl.BlockSpec((None, d, d), lambda i: (layer, 0, 0)),
                  pl.BlockSpec((1, d), lambda i: (0, 0))],
        out_specs=pl.BlockSpec((tm, d), lambda i: (i, 0)),
        out_shape=jax.ShapeDtypeStruct((m, d), F32),
        compiler_params=_params(("parallel",)),
        name="out_proj",
    )(x, gate, oa, ob, oc, od, w_o, final_w.reshape(1, d))


def _head_rms_gate(o, g, nw):
    y = o * lax.rsqrt(jnp.mean(o * o, axis=-1, keepdims=True) + EPS) * nw
    return y * (g * _sigmoid(g))


def _rope128(x, cos, sin_signed):
    half = DK // 2
    up = pltpu.roll(x, LANE - half, axis=1)
    down = pltpu.roll(x, half, axis=1)
    first = (_iota(x.shape, 1) % DK) < half
    return x * cos + jnp.where(first, up, down) * sin_signed


def _decay_kernel(mode, C, tb, qk_ref, v_ref, g_ref, aux_ref, p1_ref, p2_ref, nw_ref, s0_ref,
                  o_ref, sout_ref, S_scr, k_scr, b_scr):
    HK = N_HEADS * DK
    tblk = pl.program_id(1)

    @pl.when(tblk == 0)
    def _():
        S_scr[...] = s0_ref[...]

    q = qk_ref[:, :HK]
    k = qk_ref[:, HK:]
    if mode == "gla":
        x = jnp.dot(aux_ref[:, LANE:2 * LANE].astype(BF16), p1_ref[...], preferred_element_type=F32) + p2_ref[...]
        la = -_softplus(-x) * (1.0 / GLA_TAU)
        q = q * (DK ** -0.5)
    else:
        cos = aux_ref[:, :HK]
        sin = aux_ref[:, HK:]
        q = jnp.concatenate([_rope128(q[:, i * LANE:(i + 1) * LANE], cos[:, i * LANE:(i + 1) * LANE],
                                      sin[:, i * LANE:(i + 1) * LANE]) for i in range(HK // LANE)], axis=1)
        k = jnp.concatenate([_rope128(k[:, i * LANE:(i + 1) * LANE], cos[:, i * LANE:(i + 1) * LANE],
                                      sin[:, i * LANE:(i + 1) * LANE]) for i in range(HK // LANE)], axis=1)
        k = k * (DK ** -0.5)

    nc = tb // C
    lane_head = _iota((C, HK), 1) // DK
    heads_rows = lambda x: jnp.concatenate([jnp.where(lane_head == h, x, 0.0) for h in range(N_HEADS)],
                                           axis=0).astype(BF16)
    diag_blocks = lambda d: jnp.concatenate([d[h * DK:(h + 1) * DK, h * DV:(h + 1) * DV] for h in range(N_HEADS)], axis=0)
    heads_lanes = lambda x: jnp.concatenate([x[h * C:(h + 1) * C, :] for h in range(N_HEADS)], axis=1)
    row = _iota((tb, tb), 0)
    col = _iota((tb, tb), 1)

    if mode == "gla":
        same = (row // C) == (col // C)
        b = _mm_split(jnp.where(same & (col <= row), 1.0, 0.0).astype(BF16), la)
        btot = _mm_split(jnp.where(same, 1.0, 0.0).astype(BF16), la)
        k_scr[...] = k
        b_scr[...] = b
        qe = q * jnp.exp(b)
        ke = k * jnp.exp(btot - b)
        sel = jnp.where(_iota((tb, nc * DV), 0) // C == _iota((tb, nc * DV), 1) // DV, 1.0, 0.0).astype(BF16)
        la_hi, la_lo = _split2(la)
        tn = (((0,), (0,)), ((), ()))
        gcols = jnp.exp(lax.dot_general(la_hi, sel, tn, preferred_element_type=F32)
                        + lax.dot_general(la_lo, sel, tn, preferred_element_type=F32))
        ind = jnp.where((_iota((HK, N_HEADS * DV), 0) // DK) == (_iota((HK, N_HEADS * DV), 1) // DV), 1.0, 0.0).astype(BF16)
        D = min(GLA_DIAG, C)
        halves = C // D
        trow = _iota((D, HK), 0)
        chunks = range(nc)
        X = []
        for c in chunks:
            xs = []
            for a in range(halves):
                r0 = c * C + a * D
                qd, bd = q[r0:r0 + D], b[r0:r0 + D]
                for s in range(D):
                    r = r0 + s
                    e = jnp.where(trow >= s, jnp.exp(jnp.minimum(bd - b_scr[r:r + 1, :], 0.0)), 0.0)
                    xs.append((qd * k_scr[r:r + 1, :] * e).astype(BF16))
            X.append(jnp.concatenate(xs, axis=0))
        R = [jnp.dot(x, ind, preferred_element_type=F32) for x in X]
        o_blocks = []
        for c in chunks:
            blocks = []
            for a in range(halves):
                r0 = c * C + a * D
                base = a * D * D
                o_d = R[c][base:base + D, :] * v_ref[r0:r0 + 1, :]
                for s in range(1, D):
                    o_d = o_d + R[c][base + s * D:base + (s + 1) * D, :] * v_ref[r0 + s:r0 + s + 1, :]
                blocks.append(o_d)
            o_blocks.append(blocks)
        if halves == 2:
            lane_head_d = _iota((D, HK), 1) // DK
            qa, ka = [], []
            for c in chunks:
                r0 = c * C
                anchor = b_scr[r0 + D - 1:r0 + D, :]
                qs = q[r0 + D:r0 + C] * jnp.exp(b[r0 + D:r0 + C] - anchor)
                qa.append(jnp.concatenate([jnp.where(lane_head_d == h, qs, 0.0) for h in range(N_HEADS)], axis=0))
                ka.append(k[r0:r0 + D] * jnp.exp(anchor - b[r0:r0 + D]))
            rows_q = nc * N_HEADS * D
            p_off = _mm_nt(jnp.concatenate(qa, axis=0), jnp.concatenate(ka, axis=0))
            own = (_iota((rows_q, nc * D), 0) // (N_HEADS * D)) == (_iota((rows_q, nc * D), 1) // D)
            v_first = jnp.concatenate([v_ref[c * C:c * C + D, :] for c in chunks], axis=0)
            o_off = _mm(jnp.where(own, p_off, 0.0), v_first)
            for c in chunks:
                base = c * N_HEADS * D
                o_blocks[c][1] = o_blocks[c][1] + jnp.concatenate(
                    [o_off[base + h * D:base + (h + 1) * D, h * DV:(h + 1) * DV] for h in range(N_HEADS)], axis=1)
        o_chunks = [jnp.concatenate(blk, axis=0) for blk in o_blocks]
        v_rows = lambda c: jnp.concatenate([v_ref[c * C:(c + 1) * C, h * DV:(h + 1) * DV] for h in range(N_HEADS)], axis=0)
        dS = [_mm_tn(heads_rows(ke[c * C:(c + 1) * C]), v_rows(c)) for c in chunks]
        L = [heads_rows(qe[c * C:(c + 1) * C]) for c in chunks]
        S = S_scr[...]
        for c in chunks:
            o_chunks[c] = o_chunks[c] + heads_lanes(jnp.dot(L[c], S.astype(BF16), preferred_element_type=F32))
            S = S * gcols[:, c * DV:(c + 1) * DV] + dS[c]
        S_scr[...] = S
        o = jnp.concatenate(o_chunks, axis=0)
    else:
        lg = p2_ref[...]
        tcol = _iota((tb, 1), 0).astype(F32)
        qe = q * jnp.exp((tcol + 1.0) * lg)
        ke = k * jnp.exp((tb - 1.0 - tcol) * lg)
        dist = (row - col).astype(F32)
        kb = k.astype(BF16)
        Q = heads_rows(q)
        o_heads = []
        for h in range(N_HEADS):
            s = lax.dot_general(Q[h * tb:(h + 1) * tb], kb, (((1,), (1,)), ((), ())), preferred_element_type=F32)
            dec = jnp.where(row >= col, jnp.exp(jnp.minimum(dist * lg[:, h * DK:h * DK + 1], 0.0)), 0.0)
            o_heads.append(_mm(s * dec, v_ref[:, h * DV:(h + 1) * DV]))
        S = S_scr[...]
        o = jnp.concatenate(o_heads, axis=1) + heads_lanes(jnp.dot(heads_rows(qe), S.astype(BF16),
                                                                   preferred_element_type=F32))
        S_scr[...] = S * jnp.exp(tb * p1_ref[...]) + diag_blocks(_mm_tn(ke, v_ref[...]))

    for h in range(N_HEADS):
        sl = slice(h * DV, (h + 1) * DV)
        o_ref[:, sl] = _head_rms_gate(o[:, sl], g_ref[:, sl], nw_ref[...]).astype(o_ref.dtype)

    @pl.when(tblk == pl.num_programs(1) - 1)
    def _():
        sout_ref[...] = S_scr[...]


def decay_mixer(mode, cols, batch, seq, blk_qk, blk_v, blk_g, aux, aux_is_cols, p1, p2, norm_w, s0):
    tb = min(SEQ_BLOCK, seq)
    C = min(CHUNK, seq) if mode == "gla" else tb
    nt = seq // tb
    HK = N_HEADS * DK
    row_map = lambda bl: (lambda b, t: (b * nt + t, bl))
    if aux_is_cols:
        aux_spec = pl.BlockSpec((tb, GROUP_W), row_map(BLK_MISC))
    else:
        aux_spec = pl.BlockSpec((tb, aux.shape[1]), lambda b, t: (t, 0))
    full = lambda a: pl.BlockSpec(a.shape, lambda b, t: (0,) * a.ndim)
    return pl.pallas_call(
        functools.partial(_decay_kernel, mode, C, tb),
        grid=(batch, nt),
        in_specs=[pl.BlockSpec((tb, GROUP_W), row_map(blk_qk)),
                  pl.BlockSpec((tb, GROUP_W), row_map(blk_v)),
                  pl.BlockSpec((tb, GROUP_W), row_map(blk_g)),
                  aux_spec, full(p1), full(p2), full(norm_w),
                  pl.BlockSpec((None, HK, DV), lambda b, t: (b, 0, 0))],
        out_specs=[pl.BlockSpec((tb, GROUP_W), lambda b, t: (b * nt + t, 0)),
                   pl.BlockSpec((None, HK, DV), lambda b, t: (b, 0, 0))],
        out_shape=[jax.ShapeDtypeStruct((batch * seq, GROUP_W), BF16),
                   jax.ShapeDtypeStruct((batch, HK, DV), F32)],
        scratch_shapes=[pltpu.VMEM((HK, DV), F32)] + [pltpu.VMEM((tb, HK), F32)] * 2,
        compiler_params=_params(("parallel", "arbitrary")),
        name="decay_" + mode,
    )(cols, cols, cols, aux, p1, p2, norm_w, s0)


def pad_in_cols(w):
    z = jnp.zeros(w.shape[:-1] + (COLS_W - 7312,), w.dtype)
    return jnp.concatenate([w[..., 0:1024], w[..., 1040:1552], w[..., 1552:5136], w[..., 5136:6672],
                            w[..., 6800:7312], w[..., 6672:6800], w[..., 1024:1040], z], axis=-1)


def gla_mixer(cols, batch, seq, lp, s0):
    HK = N_HEADS * DK
    w_a2 = jnp.zeros((LANE, HK), F32).at[:GLA_RANK].set(lp['gla_w_a2']).astype(BF16)
    b_a = lp['gla_b_a'].reshape(1, HK)
    return decay_mixer("gla", cols, batch, seq, BLK_GLA_QK, BLK_GLA_V, BLK_GLA_G, cols, True,
                       w_a2, b_a, lp['gla_norm'].reshape(1, DV), s0)


def ret_mixer(cols, batch, seq, lp, s0, pos):
    HK = N_HEADS * DK
    half = DK // 2
    inv = ROPE_BASE ** (-jnp.arange(half, dtype=F32) / half)
    ang = pos.astype(F32)[:, None] * inv[None, :]
    cos = jnp.tile(jnp.cos(ang), (1, 2 * N_HEADS))
    sin = jnp.tile(jnp.concatenate([-jnp.sin(ang), jnp.sin(ang)], axis=1), (1, N_HEADS))
    tables = jnp.concatenate([cos, sin], axis=1)
    log_gamma = jnp.log(1.0 - 2.0 ** (-5.0 - jnp.arange(N_HEADS, dtype=F32)))
    la = jnp.repeat(log_gamma, DK)
    return decay_mixer("ret", cols, batch, seq, BLK_RET_QK, BLK_RET_V, BLK_RET_G, tables, False,
                       jnp.broadcast_to(la[:, None], (HK, DV)), la.reshape(1, HK), lp['ret_norm'].reshape(1, DV), s0)


def _t5_bucket(rel):
    n = jnp.maximum(rel, 0)
    max_exact = REL_BUCKETS // 2
    nf = jnp.maximum(n, 1).astype(F32)
    large = max_exact + (jnp.log(nf / max_exact) / math.log(REL_MAX_DIST / max_exact)
                         * (REL_BUCKETS - max_exact)).astype(jnp.int32)
    large = jnp.minimum(large, REL_BUCKETS - 1)
    return jnp.where(n < max_exact, n, large)


def _rel_bias_tile(rel_bias, rel):
    onehot = (_t5_bucket(rel)[..., None] == jnp.arange(REL_BUCKETS, dtype=jnp.int32)).astype(F32)
    b = jnp.einsum('qkb,bh->hqk', onehot, rel_bias.astype(F32), precision=HIGHEST)
    return jnp.where(rel[None] >= 0, b, NEG_BIG)


def _lambda(lam_ref, lam_init):
    s1 = jnp.sum(lam_ref[0:1, :] * lam_ref[1:2, :], axis=-1, keepdims=True)
    s2 = jnp.sum(lam_ref[2:3, :] * lam_ref[3:4, :], axis=-1, keepdims=True)
    return jnp.exp(s1) - jnp.exp(s2) + lam_init


def _diff_finish(o1, o2, lam, lam_init, g, sw):
    o = o1 - lam * o2
    y = o * lax.rsqrt(jnp.mean(o * o, axis=-1, keepdims=True) + EPS) * sw * (1.0 - lam_init)
    return y * (g * _sigmoid(g))


def _diffattn_kernel(lam_init, tq, tk, q_ref, k_ref, v_ref, g_ref, bias_ref, lam_ref, sw_ref,
                     o_ref, kb_scr, vt_scr):
    qi = pl.program_id(2)
    nh = q_ref.shape[1] // DV
    heads = range(nh)
    hs = lambda h: slice(h * DV, (h + 1) * DV)

    @pl.when(qi == 0)
    def _():
        for h in heads:
            kb_scr[h] = k_ref[:, hs(h)].astype(BF16)
            for j in range(vt_scr.shape[1]):
                vt_scr[h, j] = v_ref[j * tk:(j + 1) * tk, hs(h)].T.astype(BF16)

    first_map = _iota((tq, DV), 1) < DK
    Q = []
    for h in heads:
        q = q_ref[:, hs(h)] * (DK ** -0.5 * LOG2E)
        Q.append(jnp.concatenate([jnp.where(first_map, q, 0.0), jnp.where(first_map, 0.0, q)], axis=0).astype(BF16))

    def step(j, carry, bias_idx):
        m, l, acc = carry
        r0 = pl.multiple_of(j * tk, tk)
        st = [lax.dot_general(kb_scr[h, pl.ds(r0, tk), :], Q[h], (((1,), (1,)), ((), ())),
                              preferred_element_type=F32) for h in heads]
        if bias_idx is not None:
            st = [st[h] + jnp.concatenate([bias_ref[h, bias_idx]] * 2, axis=1) for h in heads]
        m_new = [jnp.maximum(m[h], jnp.max(st[h], axis=0, keepdims=True)) for h in heads]
        p = [jnp.exp2(st[h] - m_new[h]) for h in heads]
        alpha = [jnp.exp2(m[h] - m_new[h]) for h in heads]
        l = [alpha[h] * l[h] + jnp.sum(p[h], axis=0, keepdims=True) for h in heads]
        acc = [alpha[h] * acc[h] + jnp.dot(vt_scr[h, j], p[h].astype(BF16), preferred_element_type=F32)
               for h in heads]
        return m_new, l, acc

    carry = ([jnp.full((1, 2 * tq), NEG_BIG, F32)] * nh, [jnp.zeros((1, 2 * tq), F32)] * nh,
             [jnp.zeros((DV, 2 * tq), F32)] * nh)
    first_near = jnp.maximum(qi - 1, 0)
    carry = lax.fori_loop(0, first_near, lambda j, c: step(j, c, None), carry)
    _, l, acc = lax.fori_loop(first_near, qi + 1, lambda j, c: step(j, c, j - qi + 1), carry)

    lam = _lambda(lam_ref, lam_init)
    for h in heads:
        ot = acc[h][:, :tq] / l[h][:, :tq] - lam * (acc[h][:, tq:] / l[h][:, tq:])
        yt = ot * lax.rsqrt(jnp.mean(ot * ot, axis=0, keepdims=True) + EPS) * sw_ref[...] * (1.0 - lam_init)
        g = g_ref[:, hs(h)]
        o_ref[:, hs(h)] = (yt.T * (g * _sigmoid(g))).astype(o_ref.dtype)


def prompt_bias_tiles(rel_bias, seq):
    tq = tk = min(ATTN_BLOCK, seq)
    assert tk >= REL_MAX_DIST
    d = _iota((tk, tq), 1) - _iota((tk, tq), 0)
    far = rel_bias.astype(F32)[REL_BUCKETS - 1][:, None, None]
    return jnp.stack([_rel_bias_tile(rel_bias, d + tk) - far, _rel_bias_tile(rel_bias, d) - far], axis=1) * LOG2E


def diff_attn_prompt(cols, batch, seq, lp, bias_tiles, lam_init):
    tq = tk = min(ATTN_BLOCK, seq)
    nq = seq // tq
    lam4 = jnp.stack([lp['diff_lam_q1'], lp['diff_lam_k1'], lp['diff_lam_q2'], lp['diff_lam_k2']]).astype(F32)
    const = lambda a: pl.BlockSpec(a.shape, lambda b, hg, i: (0,) * a.ndim)
    sw = lp['diff_subln'].reshape(DV, 1)
    nh = ATTN_HEADS
    wg = nh * DV
    per = GROUP_W // wg
    return pl.pallas_call(
        functools.partial(_diffattn_kernel, lam_init, tq, tk),
        grid=(batch, N_HEADS // nh, nq),
        in_specs=[pl.BlockSpec((tq, wg), lambda b, hg, i: (b * nq + i, BLK_DIFF_Q * per + hg)),
                  pl.BlockSpec((seq, wg), lambda b, hg, i: (b, BLK_DIFF_K * per + hg)),
                  pl.BlockSpec((seq, wg), lambda b, hg, i: (b, BLK_DIFF_V * per + hg)),
                  pl.BlockSpec((tq, wg), lambda b, hg, i: (b * nq + i, BLK_DIFF_G * per + hg)),
                  pl.BlockSpec((nh, 2, tk, tq), lambda b, hg, i: (hg, 0, 0, 0)), const(lam4), const(sw)],
        out_specs=pl.BlockSpec((tq, wg), lambda b, hg, i: (b * nq + i, hg)),
        out_shape=jax.ShapeDtypeStruct((batch * seq, GROUP_W), BF16),
        scratch_shapes=[pltpu.VMEM((nh, seq, DV), BF16), pltpu.VMEM((nh, seq // tk, DV, tk), BF16)],
        compiler_params=_params(("arbitrary", "arbitrary", "arbitrary")),
        name="diff_attn_prompt",
    )(cols, cols, cols, cols, bias_tiles, lam4, sw)


def _diffdec_kernel(lam_init, PP, ts, page, pt_ref, q_ref, kn_ref, vn_ref, g_ref, *rest):
    k_refs = rest[:PP]
    v_refs = rest[PP:2 * PP]
    bias_ref, biasn_ref, lam_ref, sw_ref, o_ref, q_scr = rest[2 * PP:2 * PP + 6]
    state = rest[2 * PP + 6:]
    m_scr, l_scr, acc_scr = state[0::3], state[1::3], state[2::3]
    NCH = len(m_scr)
    per = PP // NCH
    nkc = per * page
    step_id = pl.program_id(1)
    W = N_HEADS * DV

    @pl.when(step_id == 0)
    def _():
        q = q_ref[...] * (DK ** -0.5)
        lane = _iota(q.shape, 1)
        pieces = []
        for h in range(N_HEADS):
            for mp in range(2):
                lo = h * DV + mp * DK
                pieces.append(jnp.where((lane >= lo) & (lane < lo + DK), q, 0.0))
        q_scr[...] = jnp.concatenate(pieces, axis=0).astype(BF16)
        for c in range(NCH):
            m_scr[c][...] = jnp.full(m_scr[c].shape, NEG_BIG, F32)
            l_scr[c][...] = jnp.zeros(l_scr[c].shape, F32)
            acc_scr[c][...] = jnp.zeros(acc_scr[c].shape, F32)

    def update(chains, kb, vb, bias):
        qb = q_scr[...]
        n = range(len(chains))
        s = [lax.dot_general(qb, kb[i], (((1,), (1,)), ((), ())), preferred_element_type=F32) + bias[i] for i in n]
        m_old = [m_scr[c][...] for c in chains]
        m_new = [jnp.maximum(m_old[i], jnp.max(s[i], axis=-1, keepdims=True)) for i in n]
        p = [jnp.exp(s[i] - m_new[i]) for i in n]
        alpha = [jnp.exp(m_old[i] - m_new[i]) for i in n]
        for i, c in enumerate(chains):
            l_scr[c][...] = alpha[i] * l_scr[c][...] + jnp.sum(p[i], axis=-1, keepdims=True)
            acc_scr[c][...] = alpha[i] * acc_scr[c][...] + jnp.dot(p[i].astype(BF16), vb[i], preferred_element_type=F32)
            m_scr[c][...] = m_new[i]

    def page_rows(ref):
        return jnp.concatenate([ref[pl.ds(h, page, stride=N_HEADS), :] for h in range(N_HEADS)], axis=1).astype(BF16)

    update(list(range(NCH)),
           [jnp.concatenate([page_rows(r) for r in k_refs[c * per:(c + 1) * per]], axis=0) for c in range(NCH)],
           [jnp.concatenate([page_rows(r) for r in v_refs[c * per:(c + 1) * per]], axis=0) for c in range(NCH)],
           [bias_ref[:, c * nkc:(c + 1) * nkc] for c in range(NCH)])

    @pl.when(step_id == pl.num_programs(1) - 1)
    def _():
        pad = jnp.zeros((page - ts, W), BF16)
        update([0], [jnp.concatenate([kn_ref[...].astype(BF16), pad], axis=0)],
               [jnp.concatenate([vn_ref[...].astype(BF16), pad], axis=0)], [biasn_ref[...]])
        m = m_scr[0][...]
        for c in range(1, NCH):
            m = jnp.maximum(m, m_scr[c][...])
        w = [jnp.exp(m_scr[c][...] - m) for c in range(NCH)]
        ln_all = sum(w[c] * l_scr[c][...] for c in range(NCH))
        acc_all = sum(w[c] * acc_scr[c][...] for c in range(NCH))
        lam = _lambda(lam_ref, lam_init)
        rows = 2 * ts
        for h in range(N_HEADS):
            a = acc_all[h * rows:(h + 1) * rows, h * DV:(h + 1) * DV]
            ln = ln_all[h * rows:(h + 1) * rows, :]
            o_ref[:, h * DV:(h + 1) * DV] = _diff_finish(
                a[:ts] / ln[:ts], a[ts:] / ln[ts:], lam, lam_init,
                g_ref[:, h * DV:(h + 1) * DV], sw_ref[...]).astype(o_ref.dtype)


def sample_bias_tiles(rel_bias, ts, n_pages, page):
    n_steps = n_pages // DECODE_PAGES
    past = n_pages * page
    nk = DECODE_PAGES * page
    assert nk >= REL_MAX_DIST
    t_of_row = jnp.tile(jnp.arange(ts, dtype=jnp.int32), N_HEADS * 2)
    h_of_row = jnp.repeat(jnp.arange(N_HEADS, dtype=jnp.int32), 2 * ts)
    head_sel = h_of_row[None, :, None] == jnp.arange(N_HEADS, dtype=jnp.int32)[:, None, None]

    def rows_bias(k_pos):
        rel = past + t_of_row[:, None] - k_pos[None, :]
        return jnp.sum(jnp.where(head_sel, _rel_bias_tile(rel_bias, rel), 0.0), axis=0)

    kk = jnp.arange(nk, dtype=jnp.int32)
    bias_steps = jnp.stack([rows_bias(kk), rows_bias((n_steps - 1) * nk + kk)])
    kn = jnp.arange(page, dtype=jnp.int32)
    bias_new = jnp.where(kn[None, :] < ts, rows_bias(past + kn), NEG_BIG)
    return bias_steps, bias_new


def diff_attn_sample(cols, batch, ts, cache_k, cache_v, layer, page_table, lp, bias_tiles, lam_init):
    page = cache_k.shape[2] // N_HEADS
    W = N_HEADS * DV
    n_pages = page_table.shape[1]
    PP = DECODE_PAGES
    n_steps = n_pages // PP
    nk = PP * page
    rows = N_HEADS * 2 * ts
    bias_steps, bias_new = bias_tiles
    lam4 = jnp.stack([lp['diff_lam_q1'], lp['diff_lam_k1'], lp['diff_lam_q2'], lp['diff_lam_k2']]).astype(F32)

    def page_spec(i):
        return pl.BlockSpec((None, None, page * N_HEADS, DV),
                            lambda b, s, pt: (layer, pt[b * n_pages + s * PP + i], 0, 0))

    col_spec = lambda blk: pl.BlockSpec((ts, GROUP_W), lambda b, s, pt: (b, blk))
    const = lambda a: pl.BlockSpec(a.shape, lambda b, s, pt: (0,) * a.ndim)
    grid_spec = pltpu.PrefetchScalarGridSpec(
        num_scalar_prefetch=1,
        grid=(batch, n_steps),
        in_specs=[col_spec(BLK_DIFF_Q), col_spec(BLK_DIFF_K), col_spec(BLK_DIFF_V), col_spec(BLK_DIFF_G)]
                 + [page_spec(i) for i in range(PP)] + [page_spec(i) for i in range(PP)]
                 + [pl.BlockSpec((None, rows, nk), lambda b, s, pt: (jnp.where(s == n_steps - 1, 1, 0), 0, 0)),
                    const(bias_new), const(lam4), pl.BlockSpec((1, DV), lambda b, s, pt: (0, 0))],
        out_specs=pl.BlockSpec((ts, GROUP_W), lambda b, s, pt: (b, 0)),
        scratch_shapes=[pltpu.VMEM((rows, W), BF16)]
                       + [pltpu.VMEM((rows, 1), F32), pltpu.VMEM((rows, 1), F32), pltpu.VMEM((rows, W), F32)] * DECODE_CHAINS)
    return pl.pallas_call(
        functools.partial(_diffdec_kernel, lam_init, PP, ts, page),
        grid_spec=grid_spec,
        out_shape=jax.ShapeDtypeStruct((batch * ts, GROUP_W), BF16),
        compiler_params=_params(("parallel", "arbitrary")),
        name="diff_attn_sample",
    )(page_table.reshape(-1), cols, cols, cols, cols, *([cache_k] * PP), *([cache_v] * PP),
      bias_steps, bias_new, lam4, lp['diff_subln'].reshape(1, DV))


RWKV_PAIRS = RWKV_HEADS // 2
SHIFT_PAD = 4 * GROUP_W


def _rwkv_kernel(C, tb, r_ref, k_ref, v_ref, misc_ref, g_ref, shift_ref, s0_ref,
                 mu_ref, w0_ref, ww2_ref, a0_ref, aw2_ref, kk_ref, ka_ref, rk_ref, nw_ref,
                 o_ref, sout_ref, S_scr, carry_scr):
    G = GROUP_W
    tblk = pl.program_id(1)

    @pl.when(tblk == 0)
    def _():
        S_scr[...] = s0_ref[...]
        carry_scr[...] = shift_ref[...]

    first_row = _iota((tb, 1), 0) == 0

    def token_shift(p, lo, width):
        prev = jnp.where(first_row, carry_scr[:, lo:lo + width], pltpu.roll(p, 1, axis=0))
        carry_scr[:, lo:lo + width] = p[tb - 1:tb, :]
        return p + mu_ref[:, lo:lo + width] * (prev - p)

    r = token_shift(r_ref[...], 0, G)
    k = token_shift(k_ref[...], G, G)
    v = token_shift(v_ref[...], 2 * G, G)
    lora = token_shift(misc_ref[:, :LANE], 3 * G, LANE)

    w_pre = w0_ref[...] + jnp.dot(jnp.tanh(lora).astype(BF16), ww2_ref[...], preferred_element_type=F32)
    lw = -jnp.exp(-_softplus(-w_pre) - 0.5)
    a = _sigmoid(a0_ref[...] + jnp.dot(lora.astype(BF16), aw2_ref[...], preferred_element_type=F32))

    kk = k * kk_ref[...]
    kk = kk / jnp.maximum(jnp.sqrt(_head_sums(kk * kk)), 1e-12)
    k2 = k * (1.0 + (a - 1.0) * ka_ref[...])
    bb = kk * a
    bonus = _head_sums(r * k2 * rk_ref[...]) * v

    row = _iota((tb, tb), 0)
    col = _iota((tb, tb), 1)
    same = (row // C) == (col // C)
    cum = _mm_split(jnp.where(same & (col <= row), 1.0, 0.0).astype(BF16), lw)
    tot = _mm_split(jnp.where(same, 1.0, 0.0).astype(BF16), lw)
    inv = jnp.exp(-cum)
    rest = jnp.exp(tot - cum)
    rt = r * jnp.exp(cum)
    kt = kk * jnp.exp(cum - lw)
    kd = k2 * inv
    bd = bb * inv
    kh = k2 * rest
    bh = bb * rest

    C2 = 2 * C
    nc = tb // C
    lane0 = _iota((C, LANE), 1) < RWKV_N
    rr = _iota((C2, C2), 0)
    cc = _iota((C2, C2), 1)
    strict = (rr % C) > (cc % C)
    incl = (rr % C) >= (cc % C)
    eye = jnp.where(rr == cc, 1.0, 0.0)
    eye_lane = _iota((LANE, LANE), 0) == _iota((LANE, LANE), 1)
    n_double = max(int(math.log2(C)) - 1, 0)
    probs = [(c, p) for c in range(nc) for p in range(RWKV_PAIRS)]

    def split(x, c, p):
        xs = x[c * C:(c + 1) * C, p * LANE:(p + 1) * LANE]
        return jnp.concatenate([jnp.where(lane0, xs, 0.0), jnp.where(lane0, 0.0, xs)], axis=0).astype(BF16)

    Lk = [split(kt, c, p) for c, p in probs]
    Lr = [split(rt, c, p) for c, p in probs]
    Rk = [split(kd, c, p) for c, p in probs]
    Rb = [split(bd, c, p) for c, p in probs]
    Vb = [split(v, c, p) for c, p in probs]
    Kh = [split(kh, c, p) for c, p in probs]
    Bh = [split(bh, c, p) for c, p in probs]
    if C2 % LANE == 0:
        G = [_mm_nt(jnp.concatenate([x, y], axis=0), jnp.concatenate([z, u], axis=0))
             for x, y, z, u in zip(Lk, Lr, Rk, Rb)]
        g_kk, g_kb = [g[:C2, :C2] for g in G], [g[:C2, C2:] for g in G]
        g_rk, g_rb = [g[C2:, :C2] for g in G], [g[C2:, C2:] for g in G]
    else:
        g_kk, g_kb = [_mm_nt(x, y) for x, y in zip(Lk, Rk)], [_mm_nt(x, y) for x, y in zip(Lk, Rb)]
        g_rk, g_rb = [_mm_nt(x, y) for x, y in zip(Lr, Rk)], [_mm_nt(x, y) for x, y in zip(Lr, Rb)]
    A_kr = [jnp.concatenate([jnp.where(strict, x, 0.0), jnp.where(incl, y, 0.0)], axis=0).astype(BF16)
            for x, y in zip(g_kk, g_rk)]
    A_rb = [jnp.where(incl, x, 0.0).astype(BF16) for x in g_rb]
    P = [jnp.where(strict, -x, 0.0) for x in g_kb]
    T = [eye + x for x in P]
    if n_double > 0:
        P = [_mm(x, x) for x in P]
    for it in range(n_double):
        if it < n_double - 1:
            TP = [_mm(jnp.concatenate([t, x], axis=0), x) for t, x in zip(T, P)]
            T = [t + y[:C2] for t, y in zip(T, TP)]
            P = [y[C2:] for y in TP]
        else:
            T = [t + _mm(t, x) for t, x in zip(T, P)]
    T = [t.astype(BF16) for t in T]
    AV = [_mm(x, y) for x, y in zip(A_kr, Vb)]
    TW = [_mm(t, jnp.concatenate([x, y[:C2].astype(BF16)], axis=1)).astype(BF16)
          for t, x, y in zip(T, Lk, AV)]
    AW = [_mm(x, y) for x, y in zip(A_rb, TW)]
    Wr = [x.astype(F32) - y[:, :LANE] for x, y in zip(Lr, AW)]
    O0 = [x[C2:] - y[:, LANE:] for x, y in zip(AV, AW)]
    BW = [_mm_tn(x, y) for x, y in zip(Bh, TW)]
    D0 = [_mm_tn(x, y) - z[:, LANE:] for x, y, z in zip(Kh, Vb, BW)]
    WN = [jnp.concatenate([x, y[:, :LANE]], axis=0).astype(BF16) for x, y in zip(Wr, BW)]
    gcol = [jnp.exp(jnp.sum(jnp.where(eye_lane, tot[c * C:c * C + 1, p * LANE:(p + 1) * LANE], 0.0),
                            axis=1, keepdims=True)) for c, p in probs]

    S = [S_scr[p] for p in range(RWKV_PAIRS)]
    o_rows = []
    for c in range(nc):
        o_lanes = []
        for p in range(RWKV_PAIRS):
            i = c * RWKV_PAIRS + p
            Rm = jnp.dot(WN[i], S[p].astype(BF16), preferred_element_type=F32)
            O = Rm[:C2] + O0[i]
            o_lanes.append(O[:C] + O[C:])
            S[p] = S[p] * gcol[i] - Rm[C2:] + D0[i]
        o_rows.append(jnp.concatenate(o_lanes, axis=1))
    for p in range(RWKV_PAIRS):
        S_scr[p] = S[p]

    o = jnp.concatenate(o_rows, axis=0)
    ms = _head_sums(o * o) * (1.0 / RWKV_N)
    y = o * lax.rsqrt(ms + EPS) * nw_ref[...] + bonus
    g = g_ref[...]
    o_ref[...] = (y * (g * _sigmoid(g))).astype(o_ref.dtype)

    @pl.when(tblk == pl.num_programs(1) - 1)
    def _():
        sout_ref[...] = S_scr[...]


def rwkv_mixer(cols, batch, seq, lp, s0_pairs, shift_pad):
    C = min(64, seq)
    tb = min(SEQ_BLOCK, seq)
    nt = seq // tb
    G = GROUP_W
    mu = lp['rwkv_mu']
    mu_pad = jnp.concatenate([mu, jnp.zeros((SHIFT_PAD - mu.shape[0],), F32)]).reshape(1, SHIFT_PAD)
    ww2 = jnp.zeros((LANE, G), F32).at[:RWKV_LORA].set(lp['rwkv_w_w2']).astype(BF16)
    aw2 = jnp.zeros((LANE, G), F32).at[RWKV_LORA:].set(lp['rwkv_a_w2']).astype(BF16)
    vec = lambda a: a.reshape(1, G).astype(F32)
    params = [mu_pad, vec(lp['rwkv_w0']), ww2, vec(lp['rwkv_a0']), aw2, vec(lp['rwkv_k_k']), vec(lp['rwkv_k_a']),
              vec(lp['rwkv_r_k']), jnp.tile(lp['rwkv_norm'], RWKV_HEADS).reshape(1, G)]
    row_map = lambda bl: (lambda b, t: (b * nt + t, bl))
    full = lambda a: pl.BlockSpec(a.shape, lambda b, t: (0,) * a.ndim)
    return pl.pallas_call(
        functools.partial(_rwkv_kernel, C, tb),
        grid=(batch, nt),
        in_specs=[pl.BlockSpec((tb, G), row_map(BLK_RWKV_R)), pl.BlockSpec((tb, G), row_map(BLK_RWKV_K)),
                  pl.BlockSpec((tb, G), row_map(BLK_RWKV_V)), pl.BlockSpec((tb, G), row_map(BLK_MISC)),
                  pl.BlockSpec((tb, G), row_map(BLK_RWKV_G)),
                  pl.BlockSpec((None, 1, SHIFT_PAD), lambda b, t: (b, 0, 0)),
                  pl.BlockSpec((None, RWKV_PAIRS, LANE, LANE), lambda b, t: (b, 0, 0, 0))]
                 + [full(a) for a in params],
        out_specs=[pl.BlockSpec((tb, G), lambda b, t: (b * nt + t, 0)),
                   pl.BlockSpec((None, RWKV_PAIRS, LANE, LANE), lambda b, t: (b, 0, 0, 0))],
        out_shape=[jax.ShapeDtypeStruct((batch * seq, G), BF16),
                   jax.ShapeDtypeStruct((batch, RWKV_PAIRS, LANE, LANE), F32)],
        scratch_shapes=[pltpu.VMEM((RWKV_PAIRS, LANE, LANE), F32), pltpu.VMEM((1, SHIFT_PAD), F32)],
        compiler_params=_params(("parallel", "arbitrary")),
        name="rwkv7",
    )(cols, cols, cols, cols, cols, shift_pad, s0_pairs, *params)


def rwkv_state_to_pairs(s):
    b = s.shape[0]
    st = jnp.swapaxes(s, -1, -2).reshape(b, RWKV_PAIRS, 2, RWKV_N, RWKV_N)
    z = jnp.zeros_like(st[:, :, 0])
    top = jnp.concatenate([st[:, :, 0], z], axis=-1)
    bot = jnp.concatenate([z, st[:, :, 1]], axis=-1)
    return jnp.concatenate([top, bot], axis=-2)


def rwkv_pairs_to_state(sp):
    b = sp.shape[0]
    h0 = sp[:, :, :RWKV_N, :RWKV_N]
    h1 = sp[:, :, RWKV_N:, RWKV_N:]
    st = jnp.stack([h0, h1], axis=2).reshape(b, RWKV_HEADS, RWKV_N, RWKV_N)
    return jnp.swapaxes(st, -1, -2)


def _group_layer(x, scale, shift, gate, rows_per_mod, batch, seq, lp, w_pad, w_o_bf, final_w, final,
                 s_gla, s_ret, s_rwkv, s_shift, pos, diff_fn, tm_in, tm_out, k_all, v_all, layer):
    cols, k_all, v_all = in_proj(x, scale, shift, lp['norm_w'], w_pad, k_all, v_all, layer, rows_per_mod, tm_in, IN_TN)
    oa, n_gla = gla_mixer(cols, batch, seq, lp, s_gla)
    ob = diff_fn(cols)
    oc, n_ret = ret_mixer(cols, batch, seq, lp, s_ret, pos)
    od, n_rwkv = rwkv_mixer(cols, batch, seq, lp, s_rwkv, s_shift)
    y = out_proj(x, gate, oa, ob, oc, od, w_o_bf, layer, final_w, rows_per_mod, tm_out, final)
    last = cols.reshape(batch, seq, COLS_W)[:, -1]
    n_shift = jnp.concatenate([last[:, BLK_RWKV_R * GROUP_W:(BLK_RWKV_V + 1) * GROUP_W],
                               last[:, BLK_MISC * GROUP_W:BLK_MISC * GROUP_W + 2 * RWKV_LORA]], axis=-1)
    return y, k_all, v_all, (n_gla.reshape(batch, N_HEADS, DK, DV), n_ret.reshape(batch, N_HEADS, DK, DV),
                             rwkv_pairs_to_state(n_rwkv), n_shift)


def kernel(x_prompt, x_sample, cache_k, cache_v, state_gla, state_ret, state_rwkv, state_shift, page_table,
           c_prompt, c_sample, rel_bias, w_ada, b_ada, norm_w, w_in, w_o, gla_w_a2, gla_b_a, gla_norm,
           diff_lam_q1, diff_lam_k1, diff_lam_q2, diff_lam_k2, diff_subln, ret_norm, rwkv_mu, rwkv_w0, rwkv_w_w2,
           rwkv_a0, rwkv_a_w2, rwkv_k_k, rwkv_k_a, rwkv_r_k, rwkv_norm, final_norm):
    Bp, Sp, D = x_prompt.shape
    Bs, Ts, _ = x_sample.shape
    depth = w_in.shape[0]
    n_pool, page = cache_k.shape[1], cache_k.shape[2]
    past_len = page_table.shape[1] * page
    ck = cache_k.reshape(depth, n_pool, page * N_HEADS, DV)
    cv = cache_v.reshape(depth, n_pool, page * N_HEADS, DV)
    pos_p = jnp.arange(Sp, dtype=jnp.int32)
    pos_s = past_len + jnp.arange(Ts, dtype=jnp.int32)
    c_all = jnp.concatenate([c_prompt, c_sample], axis=0)
    xp = x_prompt.reshape(Bp * Sp, D)
    xs = x_sample.reshape(Bs * Ts, D)
    zeros_dec = jnp.zeros((Bp, N_HEADS * DK, DV), F32)
    zeros_rwkv = jnp.zeros((Bp, RWKV_PAIRS, LANE, LANE), F32)
    zeros_shift = jnp.zeros((Bp, 1, SHIFT_PAD), F32)
    bias_p = prompt_bias_tiles(rel_bias, Sp)
    bias_s = sample_bias_tiles(rel_bias, Ts, page_table.shape[1], page)
    kp = jnp.zeros((depth, Bp * Sp * N_HEADS, DV), F32)
    vp = jnp.zeros((depth, Bp * Sp * N_HEADS, DV), F32)
    ks = jnp.zeros((depth, Bs * Ts * N_HEADS, DV), F32)
    vs = jnp.zeros((depth, Bs * Ts * N_HEADS, DV), F32)
    w_pad_all = pad_in_cols(w_in.astype(BF16))
    w_o_all = w_o.astype(BF16)
    new_p, new_s = [], []
    for l in range(depth):
        lp = dict(norm_w=norm_w[l], gla_w_a2=gla_w_a2[l], gla_b_a=gla_b_a[l], gla_norm=gla_norm[l],
                  diff_lam_q1=diff_lam_q1[l], diff_lam_k1=diff_lam_k1[l], diff_lam_q2=diff_lam_q2[l],
                  diff_lam_k2=diff_lam_k2[l], diff_subln=diff_subln[l], ret_norm=ret_norm[l],
                  rwkv_mu=rwkv_mu[l], rwkv_w0=rwkv_w0[l], rwkv_w_w2=rwkv_w_w2[l], rwkv_a0=rwkv_a0[l],
                  rwkv_a_w2=rwkv_a_w2[l], rwkv_k_k=rwkv_k_k[l], rwkv_k_a=rwkv_k_a[l], rwkv_r_k=rwkv_r_k[l],
                  rwkv_norm=rwkv_norm[l])
        final = l == depth - 1
        lam_init = 0.8 - 0.6 * math.exp(-0.3 * l)
        mod = ada_mod(c_all, w_ada, b_ada, l)
        shift, scale, gate = mod[:, :D], mod[:, D:2 * D], mod[:, 2 * D:]
        w_pad, w_o_bf = w_pad_all, w_o_all

        per_batch = lambda a: a[:Bp].reshape(Bp, 1, D)
        xp, kp, vp, st_p = _group_layer(
            xp, per_batch(scale), per_batch(shift), per_batch(gate), Sp, Bp, Sp, lp, w_pad, w_o_bf, final_norm, final,
            zeros_dec, zeros_dec, zeros_rwkv, zeros_shift, pos_p,
            lambda cols: diff_attn_prompt(cols, Bp, Sp, lp, bias_p, lam_init), min(IN_TM, Sp), min(OUT_TM, Sp),
            kp, vp, l)

        per_row = lambda a: jnp.repeat(a[Bp:], Ts, axis=0)
        sh = state_shift[l]
        sh_pad = jnp.concatenate([sh, jnp.zeros((Bs, SHIFT_PAD - sh.shape[1]), F32)], axis=1).reshape(Bs, 1, SHIFT_PAD)
        xs, ks, vs, st_s = _group_layer(
            xs, per_row(scale), per_row(shift), per_row(gate), 1, Bs, Ts, lp, w_pad, w_o_bf, final_norm, final,
            state_gla[l].reshape(Bs, N_HEADS * DK, DV), state_ret[l].reshape(Bs, N_HEADS * DK, DV),
            rwkv_state_to_pairs(state_rwkv[l]), sh_pad, pos_s,
            lambda cols: diff_attn_sample(cols, Bs, Ts, ck, cv, l, page_table, lp, bias_s, lam_init),
            Bs * Ts, Bs * Ts, ks, vs, l)
        new_p.append(st_p)
        new_s.append(st_s)
    stack = lambda states, i: jnp.stack([s[i] for s in states])
    heads = lambda a, b, t: a.reshape(depth, b, t, N_HEADS, DV)
    return (xp.reshape(Bp, Sp, D), xs.reshape(Bs, Ts, D),
            heads(kp, Bp, Sp), heads(vp, Bp, Sp), heads(ks, Bs, Ts), heads(vs, Bs, Ts),
            stack(new_p, 0), stack(new_s, 0), stack(new_p, 1), stack(new_s, 1),
            stack(new_p, 2), stack(new_s, 2), stack(new_p, 3), stack(new_s, 3))
```

```python
import functools
import math

import jax
import jax.numpy as jnp
from jax import lax
from jax.experimental import pallas as pl
from jax.experimental.pallas import tpu as pltpu

F32 = jnp.float32
BF16 = jnp.bfloat16
HIGHEST = lax.Precision.HIGHEST

D_MODEL = 2048
GROUP_W = 512
N_HEADS = 4
DK = 64
DV = 128
GLA_RANK = 16
GLA_TAU = 16.0
RWKV_N = 64
RWKV_HEADS = 8
RWKV_LORA = 64
ROPE_BASE = 10000.0
REL_BUCKETS = 32
REL_MAX_DIST = 128
CHUNK = 16
GLA_DIAG = 8
EPS = 1e-6
NEG_BIG = -1e30
LOG2E = math.log2(math.e)

LANE = 128
VMEM_LIMIT = 48 * 1024 * 1024
ATTN_BLOCK = 512
ATTN_HEADS = 4
DECODE_PAGES = 16
DECODE_CHAINS = 2
SEQ_BLOCK = 256
IN_TM, IN_TN = 1024, 1280
BIG_VMEM_LIMIT = 56 * 1024 * 1024
NORM_ROWS = 128
OUT_TM = 256
ADA_TN = 1536

COLS_W = 15 * GROUP_W
BLK_GLA_QK, BLK_GLA_V, BLK_GLA_G = 0, 1, 2
BLK_DIFF_Q, BLK_DIFF_K, BLK_DIFF_V, BLK_DIFF_G = 3, 4, 5, 6
BLK_RET_QK, BLK_RET_V, BLK_RET_G = 7, 8, 9
BLK_RWKV_R, BLK_RWKV_K, BLK_RWKV_V, BLK_RWKV_G = 10, 11, 12, 13
BLK_MISC = 14


def _mm(a, b):
    return jnp.dot(a.astype(BF16), b.astype(BF16), preferred_element_type=F32)


def _mm_nt(a, b):
    return lax.dot_general(a.astype(BF16), b.astype(BF16), (((1,), (1,)), ((), ())), preferred_element_type=F32)


def _mm_tn(a, b):
    return lax.dot_general(a.astype(BF16), b.astype(BF16), (((0,), (0,)), ((), ())), preferred_element_type=F32)


def _mm_exact(a, b):
    return jnp.dot(a, b, precision=HIGHEST, preferred_element_type=F32)


def _mm_tn_exact(a, b):
    return lax.dot_general(a, b, (((0,), (0,)), ((), ())), precision=HIGHEST, preferred_element_type=F32)


def _split2(x):
    hi = x.astype(BF16)
    return hi, (x - hi.astype(F32)).astype(BF16)


def _mm_split(a, x):
    hi, lo = _split2(x)
    return jnp.dot(a, hi, preferred_element_type=F32) + jnp.dot(a, lo, preferred_element_type=F32)


def _head_sums(x):
    ind = jnp.where((_iota((LANE, LANE), 0) // RWKV_N) == (_iota((LANE, LANE), 1) // RWKV_N), 1.0, 0.0).astype(BF16)
    hi, lo = _split2(x)
    parts = []
    for i in range(x.shape[1] // LANE):
        sl = slice(i * LANE, (i + 1) * LANE)
        parts.append(jnp.dot(hi[:, sl], ind, preferred_element_type=F32) + jnp.dot(lo[:, sl], ind, preferred_element_type=F32))
    return jnp.concatenate(parts, axis=1)


def _sigmoid(x):
    return 1.0 / (1.0 + jnp.exp(-x))


def _softplus(x):
    return jnp.maximum(x, 0.0) + jnp.log1p(jnp.exp(-jnp.abs(x)))


def _iota(shape, dim):
    return lax.broadcasted_iota(jnp.int32, shape, dim)


def _params(sem, vmem_limit=VMEM_LIMIT):
    return pltpu.CompilerParams(dimension_semantics=sem, vmem_limit_bytes=vmem_limit)


def _ada_kernel(c_ref, w_ref, b_ref, o_ref):
    c = c_ref[...]
    o_ref[...] = _mm(c * _sigmoid(c), w_ref[...]) + b_ref[...]


def ada_mod(c, w_ada, b_ada, layer, tn=ADA_TN):
    rows, d = c.shape
    depth, _, n = w_ada.shape
    return pl.pallas_call(
        _ada_kernel,
        grid=(n // tn,),
        in_specs=[pl.BlockSpec((rows, d), lambda j: (0, 0)),
                  pl.BlockSpec((None, d, tn), lambda j: (layer, 0, j)),
                  pl.BlockSpec((None, 1, tn), lambda j: (layer, 0, j))],
        out_specs=pl.BlockSpec((rows, tn), lambda j: (0, j)),
        out_shape=jax.ShapeDtypeStruct((rows, n), F32),
        compiler_params=_params(("arbitrary",)),
        name="ada_mod",
    )(c, w_ada, b_ada.reshape(depth, 1, n))


def _inproj_kernel(k_tile, k_lo, v_tile, v_lo, x_ref, scale_ref, shift_ref, nw_ref, w_ref, kin_ref, vin_ref,
                   o_ref, k_ref, v_ref, h_ref):
    del kin_ref, vin_ref
    j = pl.program_id(1)

    tm = o_ref.shape[0]
    rows = min(NORM_ROWS, tm)
    per_row_mod = scale_ref.shape[0] == tm

    @pl.when(j == 0)
    def _():
        def norm_rows(i, carry):
            sl = pl.ds(pl.multiple_of(i * rows, rows), rows)
            x = x_ref[sl, :]
            y = x * lax.rsqrt(jnp.mean(x * x, axis=-1, keepdims=True) + EPS) * nw_ref[...]
            scale = scale_ref[sl, :] if per_row_mod else scale_ref[...]
            shift = shift_ref[sl, :] if per_row_mod else shift_ref[...]
            h_ref[sl, :] = (y * (1.0 + scale) + shift).astype(BF16)
            return carry

        lax.fori_loop(0, tm // rows, norm_rows, 0)

    o_ref[...] = jnp.dot(h_ref[...], w_ref[...], preferred_element_type=F32)

    def heads_to_rows(dst_ref, lo):
        for h in range(N_HEADS):
            dst_ref[pl.ds(h, tm, stride=N_HEADS), :] = o_ref[:, lo + h * DV:lo + (h + 1) * DV]

    @pl.when(j == k_tile)
    def _():
        heads_to_rows(k_ref, k_lo)

    @pl.when(j == v_tile)
    def _():
        heads_to_rows(v_ref, v_lo)


def in_proj(x, scale, shift, norm_w, w_pad, k_all, v_all, layer, rows_per_mod, tm, tn):
    m, d = x.shape
    n = w_pad.shape[2]
    k_lo, v_lo = BLK_DIFF_K * GROUP_W, BLK_DIFF_V * GROUP_W
    k_tile, v_tile = k_lo // tn, v_lo // tn
    assert (k_lo + GROUP_W - 1) // tn == k_tile and (v_lo + GROUP_W - 1) // tn == v_tile
    if rows_per_mod == 1:
        mod_spec = pl.BlockSpec((tm, d), lambda i, j: (i, 0))
    else:
        per = rows_per_mod // tm
        mod_spec = pl.BlockSpec((None, 1, d), lambda i, j: (i // per, 0, 0))
    kv_spec = pl.BlockSpec((None, tm * N_HEADS, DV), lambda i, j: (layer, i, 0), pipeline_mode=pl.Buffered(1))
    return pl.pallas_call(
        functools.partial(_inproj_kernel, k_tile, k_lo - k_tile * tn, v_tile, v_lo - v_tile * tn),
        grid=(m // tm, n // tn),
        in_specs=[pl.BlockSpec((tm, d), lambda i, j: (i, 0)),
                  mod_spec, mod_spec,
                  pl.BlockSpec((1, d), lambda i, j: (0, 0)),
                  pl.BlockSpec((None, d, tn), lambda i, j: (layer, 0, j)),
                  pl.BlockSpec(memory_space=pl.ANY), pl.BlockSpec(memory_space=pl.ANY)],
        out_specs=[pl.BlockSpec((tm, tn), lambda i, j: (i, j)), kv_spec, kv_spec],
        out_shape=[jax.ShapeDtypeStruct((m, n), F32), jax.ShapeDtypeStruct(k_all.shape, F32),
                   jax.ShapeDtypeStruct(v_all.shape, F32)],
        input_output_aliases={5: 1, 6: 2},
        scratch_shapes=[pltpu.VMEM((tm, d), BF16)],
        compiler_params=_params(("parallel", "arbitrary"), BIG_VMEM_LIMIT),
        name="in_proj",
    )(x, scale, shift, norm_w.reshape(1, d), w_pad, k_all, v_all)


def _outproj_kernel(final, x_ref, gate_ref, oa_ref, ob_ref, oc_ref, od_ref, w_ref, fw_ref, y_ref):
    acc = jnp.dot(oa_ref[...], w_ref[0 * GROUP_W:1 * GROUP_W, :], preferred_element_type=F32)
    acc += jnp.dot(ob_ref[...], w_ref[1 * GROUP_W:2 * GROUP_W, :], preferred_element_type=F32)
    acc += jnp.dot(oc_ref[...], w_ref[2 * GROUP_W:3 * GROUP_W, :], preferred_element_type=F32)
    acc += jnp.dot(od_ref[...], w_ref[3 * GROUP_W:4 * GROUP_W, :], preferred_element_type=F32)
    y = x_ref[...] + gate_ref[...] * acc
    if final:
        y = y * lax.rsqrt(jnp.mean(y * y, axis=-1, keepdims=True) + EPS) * fw_ref[...]
    y_ref[...] = y


def out_proj(x, gate, oa, ob, oc, od, w_o, layer, final_w, rows_per_mod, tm, final):
    m, d = x.shape
    if rows_per_mod == 1:
        mod_spec = pl.BlockSpec((tm, d), lambda i: (i, 0))
    else:
        per = rows_per_mod // tm
        mod_spec = pl.BlockSpec((None, 1, d), lambda i: (i // per, 0, 0))
    o_spec = pl.BlockSpec((tm, GROUP_W), lambda i: (i, 0))
    return pl.pallas_call(
        functools.partial(_outproj_kernel, final),
        grid=(m // tm,),
        in_specs=[pl.BlockSpec((tm, d), lambda i: (i, 0)), mod_spec,
                  o_spec, o_spec, o_spec, o_spec,
                  pl.BlockSpec((None, d, d), lambda i: (layer, 0, 0)),
                  pl.BlockSpec((1, d), lambda i: (0, 0))],
        out_specs=pl.BlockSpec((tm, d), lambda i: (i, 0)),
        out_shape=jax.ShapeDtypeStruct((m, d), F32),
        compiler_params=_params(("parallel",)),
        name="out_proj",
    )(x, gate, oa, ob, oc, od, w_o, final_w.reshape(1, d))


def _head_rms_gate(o, g, nw):
    y = o * lax.rsqrt(jnp.mean(o * o, axis=-1, keepdims=True) + EPS) * nw
    return y * (g * _sigmoid(g))


def _rope128(x, cos, sin_signed):
    half = DK // 2
    up = pltpu.roll(x, LANE - half, axis=1)
    down = pltpu.roll(x, half, axis=1)
    first = (_iota(x.shape, 1) % DK) < half
    return x * cos + jnp.where(first, up, down) * sin_signed


def _decay_kernel(mode, C, tb, qk_ref, v_ref, g_ref, aux_ref, p1_ref, p2_ref, nw_ref, s0_ref,
                  o_ref, sout_ref, S_scr, k_scr, b_scr):
    HK = N_HEADS * DK
    tblk = pl.program_id(1)

    @pl.when(tblk == 0)
    def _():
        S_scr[...] = s0_ref[...]

    q = qk_ref[:, :HK]
    k = qk_ref[:, HK:]
    if mode == "gla":
        x = jnp.dot(aux_ref[:, LANE:2 * LANE].astype(BF16), p1_ref[...], preferred_element_type=F32) + p2_ref[...]
        la = -_softplus(-x) * (1.0 / GLA_TAU)
        q = q * (DK ** -0.5)
    else:
        cos = aux_ref[:, :HK]
        sin = aux_ref[:, HK:]
        q = jnp.concatenate([_rope128(q[:, i * LANE:(i + 1) * LANE], cos[:, i * LANE:(i + 1) * LANE],
                                      sin[:, i * LANE:(i + 1) * LANE]) for i in range(HK // LANE)], axis=1)
        k = jnp.concatenate([_rope128(k[:, i * LANE:(i + 1) * LANE], cos[:, i * LANE:(i + 1) * LANE],
                                      sin[:, i * LANE:(i + 1) * LANE]) for i in range(HK // LANE)], axis=1)
        k = k * (DK ** -0.5)

    nc = tb // C
    lane_head = _iota((C, HK), 1) // DK
    heads_rows = lambda x: jnp.concatenate([jnp.where(lane_head == h, x, 0.0) for h in range(N_HEADS)],
                                           axis=0).astype(BF16)
    diag_blocks = lambda d: jnp.concatenate([d[h * DK:(h + 1) * DK, h * DV:(h + 1) * DV] for h in range(N_HEADS)], axis=0)
    heads_lanes = lambda x: jnp.concatenate([x[h * C:(h + 1) * C, :] for h in range(N_HEADS)], axis=1)
    row = _iota((tb, tb), 0)
    col = _iota((tb, tb), 1)

    if mode == "gla":
        same = (row // C) == (col // C)
        b = _mm_split(jnp.where(same & (col <= row), 1.0, 0.0).astype(BF16), la)
        btot = _mm_split(jnp.where(same, 1.0, 0.0).astype(BF16), la)
        k_scr[...] = k
        b_scr[...] = b
        qe = q * jnp.exp(b)
        ke = k * jnp.exp(btot - b)
        sel = jnp.where(_iota((tb, nc * DV), 0) // C == _iota((tb, nc * DV), 1) // DV, 1.0, 0.0).astype(BF16)
        la_hi, la_lo = _split2(la)
        tn = (((0,), (0,)), ((), ()))
        gcols = jnp.exp(lax.dot_general(la_hi, sel, tn, preferred_element_type=F32)
                        + lax.dot_general(la_lo, sel, tn, preferred_element_type=F32))
        ind = jnp.where((_iota((HK, N_HEADS * DV), 0) // DK) == (_iota((HK, N_HEADS * DV), 1) // DV), 1.0, 0.0).astype(BF16)
        D = min(GLA_DIAG, C)
        halves = C // D
        trow = _iota((D, HK), 0)
        chunks = range(nc)
        X = []
        for c in chunks:
            xs = []
            for a in range(halves):
                r0 = c * C + a * D
                qd, bd = q[r0:r0 + D], b[r0:r0 + D]
                for s in range(D):
                    r = r0 + s
                    e = jnp.where(trow >= s, jnp.exp(jnp.minimum(bd - b_scr[r:r + 1, :], 0.0)), 0.0)
                    xs.append((qd * k_scr[r:r + 1, :] * e).astype(BF16))
            X.append(jnp.concatenate(xs, axis=0))
        R = [jnp.dot(x, ind, preferred_element_type=F32) for x in X]
        o_blocks = []
        for c in chunks:
            blocks = []
            for a in range(halves):
                r0 = c * C + a * D
                base = a * D * D
                o_d = R[c][base:base + D, :] * v_ref[r0:r0 + 1, :]
                for s in range(1, D):
                    o_d = o_d + R[c][base + s * D:base + (s + 1) * D, :] * v_ref[r0 + s:r0 + s + 1, :]
                blocks.append(o_d)
            o_blocks.append(blocks)
        if halves == 2:
            lane_head_d = _iota((D, HK), 1) // DK
            qa, ka = [], []
            for c in chunks:
                r0 = c * C
                anchor = b_scr[r0 + D - 1:r0 + D, :]
                qs = q[r0 + D:r0 + C] * jnp.exp(b[r0 + D:r0 + C] - anchor)
                qa.append(jnp.concatenate([jnp.where(lane_head_d == h, qs, 0.0) for h in range(N_HEADS)], axis=0))
                ka.append(k[r0:r0 + D] * jnp.exp(anchor - b[r0:r0 + D]))
            rows_q = nc * N_HEADS * D
            p_off = _mm_nt(jnp.concatenate(qa, axis=0), jnp.concatenate(ka, axis=0))
            own = (_iota((rows_q, nc * D), 0) // (N_HEADS * D)) == (_iota((rows_q, nc * D), 1) // D)
            v_first = jnp.concatenate([v_ref[c * C:c * C + D, :] for c in chunks], axis=0)
            o_off = _mm(jnp.where(own, p_off, 0.0), v_first)
            for c in chunks:
                base = c * N_HEADS * D
                o_blocks[c][1] = o_blocks[c][1] + jnp.concatenate(
                    [o_off[base + h * D:base + (h + 1) * D, h * DV:(h + 1) * DV] for h in range(N_HEADS)], axis=1)
        o_chunks = [jnp.concatenate(blk, axis=0) for blk in o_blocks]
        v_rows = lambda c: jnp.concatenate([v_ref[c * C:(c + 1) * C, h * DV:(h + 1) * DV] for h in range(N_HEADS)], axis=0)
        dS = [_mm_tn(heads_rows(ke[c * C:(c + 1) * C]), v_rows(c)) for c in chunks]
        L = [heads_rows(qe[c * C:(c + 1) * C]) for c in chunks]
        S = S_scr[...]
        for c in chunks:
            o_chunks[c] = o_chunks[c] + heads_lanes(jnp.dot(L[c], S.astype(BF16), preferred_element_type=F32))
            S = S * gcols[:, c * DV:(c + 1) * DV] + dS[c]
        S_scr[...] = S
        o = jnp.concatenate(o_chunks, axis=0)
    else:
        lg = p2_ref[...]
        tcol = _iota((tb, 1), 0).astype(F32)
        qe = q * jnp.exp((tcol + 1.0) * lg)
        ke = k * jnp.exp((tb - 1.0 - tcol) * lg)
        dist = (row - col).astype(F32)
        kb = k.astype(BF16)
        Q = heads_rows(q)
        o_heads = []
        for h in range(N_HEADS):
            s = lax.dot_general(Q[h * tb:(h + 1) * tb], kb, (((1,), (1,)), ((), ())), preferred_element_type=F32)
            dec = jnp.where(row >= col, jnp.exp(jnp.minimum(dist * lg[:, h * DK:h * DK + 1], 0.0)), 0.0)
            o_heads.append(_mm(s * dec, v_ref[:, h * DV:(h + 1) * DV]))
        S = S_scr[...]
        o = jnp.concatenate(o_heads, axis=1) + heads_lanes(jnp.dot(heads_rows(qe), S.astype(BF16),
                                                                   preferred_element_type=F32))
        S_scr[...] = S * jnp.exp(tb * p1_ref[...]) + diag_blocks(_mm_tn(ke, v_ref[...]))

    for h in range(N_HEADS):
        sl = slice(h * DV, (h + 1) * DV)
        o_ref[:, sl] = _head_rms_gate(o[:, sl], g_ref[:, sl], nw_ref[...]).astype(o_ref.dtype)

    @pl.when(tblk == pl.num_programs(1) - 1)
    def _():
        sout_ref[...] = S_scr[...]


def decay_mixer(mode, cols, batch, seq, blk_qk, blk_v, blk_g, aux, aux_is_cols, p1, p2, norm_w, s0):
    tb = min(SEQ_BLOCK, seq)
    C = min(CHUNK, seq) if mode == "gla" else tb
    nt = seq // tb
    HK = N_HEADS * DK
    row_map = lambda bl: (lambda b, t: (b * nt + t, bl))
    if aux_is_cols:
        aux_spec = pl.BlockSpec((tb, GROUP_W), row_map(BLK_MISC))
    else:
        aux_spec = pl.BlockSpec((tb, aux.shape[1]), lambda b, t: (t, 0))
    full = lambda a: pl.BlockSpec(a.shape, lambda b, t: (0,) * a.ndim)
    return pl.pallas_call(
        functools.partial(_decay_kernel, mode, C, tb),
        grid=(batch, nt),
        in_specs=[pl.BlockSpec((tb, GROUP_W), row_map(blk_qk)),
                  pl.BlockSpec((tb, GROUP_W), row_map(blk_v)),
                  pl.BlockSpec((tb, GROUP_W), row_map(blk_g)),
                  aux_spec, full(p1), full(p2), full(norm_w),
                  pl.BlockSpec((None, HK, DV), lambda b, t: (b, 0, 0))],
        out_specs=[pl.BlockSpec((tb, GROUP_W), lambda b, t: (b * nt + t, 0)),
                   pl.BlockSpec((None, HK, DV), lambda b, t: (b, 0, 0))],
        out_shape=[jax.ShapeDtypeStruct((batch * seq, GROUP_W), BF16),
                   jax.ShapeDtypeStruct((batch, HK, DV), F32)],
        scratch_shapes=[pltpu.VMEM((HK, DV), F32)] + [pltpu.VMEM((tb, HK), F32)] * 2,
        compiler_params=_params(("parallel", "arbitrary")),
        name="decay_" + mode,
    )(cols, cols, cols, aux, p1, p2, norm_w, s0)


def pad_in_cols(w):
    z = jnp.zeros(w.shape[:-1] + (COLS_W - 7312,), w.dtype)
    return jnp.concatenate([w[..., 0:1024], w[..., 1040:1552], w[..., 1552:5136], w[..., 5136:6672],
                            w[..., 6800:7312], w[..., 6672:6800], w[..., 1024:1040], z], axis=-1)


IN_W = 7312
ALPHA_LO = 1024
RWKV_LORA_LO, RWKV_G_LO = 6672, 6800


def _wprep_kernel(w_ref, o_ref):
    x = w_ref[...]
    rows = x.shape[0]
    z = x[:, ALPHA_LO + GLA_RANK:IN_W].astype(BF16)
    n_main = RWKV_LORA_LO - ALPHA_LO - GLA_RANK
    o_ref[:, :ALPHA_LO] = x[:, :ALPHA_LO].astype(BF16)
    o_ref[:, ALPHA_LO:ALPHA_LO + n_main] = z[:, :n_main]
    g_dst = BLK_RWKV_G * GROUP_W
    o_ref[:, g_dst:g_dst + GROUP_W] = z[:, RWKV_G_LO - ALPHA_LO - GLA_RANK:]
    misc = BLK_MISC * GROUP_W
    o_ref[:, misc:misc + LANE] = z[:, n_main:n_main + LANE]
    alpha = jnp.concatenate([x[:, ALPHA_LO:ALPHA_LO + GLA_RANK], jnp.zeros((rows, LANE - GLA_RANK), F32)], axis=1)
    o_ref[:, misc + LANE:misc + 2 * LANE] = alpha.astype(BF16)
    o_ref[:, misc + 2 * LANE:] = jnp.zeros((rows, GROUP_W - 2 * LANE), BF16)


def prep_in_weights(w_in, rows=256):
    depth, d, n = w_in.shape
    assert n == IN_W
    return pl.pallas_call(
        _wprep_kernel,
        grid=(depth, d // rows),
        in_specs=[pl.BlockSpec((None, rows, n), lambda l, i: (l, i, 0))],
        out_specs=pl.BlockSpec((None, rows, COLS_W), lambda l, i: (l, i, 0)),
        out_shape=jax.ShapeDtypeStruct((depth, d, COLS_W), BF16),
        compiler_params=_params(("parallel", "parallel")),
        name="prep_in_weights",
    )(w_in)


def gla_mixer(cols, batch, seq, lp, s0):
    HK = N_HEADS * DK
    w_a2 = jnp.zeros((LANE, HK), F32).at[:GLA_RANK].set(lp['gla_w_a2']).astype(BF16)
    b_a = lp['gla_b_a'].reshape(1, HK)
    return decay_mixer("gla", cols, batch, seq, BLK_GLA_QK, BLK_GLA_V, BLK_GLA_G, cols, True,
                       w_a2, b_a, lp['gla_norm'].reshape(1, DV), s0)


def ret_mixer(cols, batch, seq, lp, s0, pos):
    HK = N_HEADS * DK
    half = DK // 2
    inv = ROPE_BASE ** (-jnp.arange(half, dtype=F32) / half)
    ang = pos.astype(F32)[:, None] * inv[None, :]
    cos = jnp.tile(jnp.cos(ang), (1, 2 * N_HEADS))
    sin = jnp.tile(jnp.concatenate([-jnp.sin(ang), jnp.sin(ang)], axis=1), (1, N_HEADS))
    tables = jnp.concatenate([cos, sin], axis=1)
    log_gamma = jnp.log(1.0 - 2.0 ** (-5.0 - jnp.arange(N_HEADS, dtype=F32)))
    la = jnp.repeat(log_gamma, DK)
    return decay_mixer("ret", cols, batch, seq, BLK_RET_QK, BLK_RET_V, BLK_RET_G, tables, False,
                       jnp.broadcast_to(la[:, None], (HK, DV)), la.reshape(1, HK), lp['ret_norm'].reshape(1, DV), s0)


def _t5_bucket(rel):
    n = jnp.maximum(rel, 0)
    max_exact = REL_BUCKETS // 2
    nf = jnp.maximum(n, 1).astype(F32)
    large = max_exact + (jnp.log(nf / max_exact) / math.log(REL_MAX_DIST / max_exact)
                         * (REL_BUCKETS - max_exact)).astype(jnp.int32)
    large = jnp.minimum(large, REL_BUCKETS - 1)
    return jnp.where(n < max_exact, n, large)


def _rel_bias_tile(rel_bias, rel):
    onehot = (_t5_bucket(rel)[..., None] == jnp.arange(REL_BUCKETS, dtype=jnp.int32)).astype(F32)
    b = jnp.einsum('qkb,bh->hqk', onehot, rel_bias.astype(F32), precision=HIGHEST)
    return jnp.where(rel[None] >= 0, b, NEG_BIG)


def _lambda(lam_ref, lam_init):
    s1 = jnp.sum(lam_ref[0:1, :] * lam_ref[1:2, :], axis=-1, keepdims=True)
    s2 = jnp.sum(lam_ref[2:3, :] * lam_ref[3:4, :], axis=-1, keepdims=True)
    return jnp.exp(s1) - jnp.exp(s2) + lam_init


def _diff_finish(o1, o2, lam, lam_init, g, sw):
    o = o1 - lam * o2
    y = o * lax.rsqrt(jnp.mean(o * o, axis=-1, keepdims=True) + EPS) * sw * (1.0 - lam_init)
    return y * (g * _sigmoid(g))


def _diffattn_kernel(lam_init, tq, tk, q_ref, k_ref, v_ref, g_ref, bias_ref, lam_ref, sw_ref,
                     o_ref, kb_scr, vt_scr):
    qi = pl.program_id(2)
    nh = q_ref.shape[1] // DV
    heads = range(nh)
    hs = lambda h: slice(h * DV, (h + 1) * DV)

    @pl.when(qi == 0)
    def _():
        for h in heads:
            kb_scr[h] = k_ref[:, hs(h)].astype(BF16)
            for j in range(vt_scr.shape[1]):
                vt_scr[h, j] = v_ref[j * tk:(j + 1) * tk, hs(h)].T.astype(BF16)

    first_map = _iota((tq, DV), 1) < DK
    Q = []
    for h in heads:
        q = q_ref[:, hs(h)] * (DK ** -0.5 * LOG2E)
        Q.append(jnp.concatenate([jnp.where(first_map, q, 0.0), jnp.where(first_map, 0.0, q)], axis=0).astype(BF16))

    def step(j, carry, bias_idx):
        m, l, acc = carry
        r0 = pl.multiple_of(j * tk, tk)
        st = [lax.dot_general(kb_scr[h, pl.ds(r0, tk), :], Q[h], (((1,), (1,)), ((), ())),
                              preferred_element_type=F32) for h in heads]
        if bias_idx is not None:
            st = [st[h] + jnp.concatenate([bias_ref[h, bias_idx]] * 2, axis=1) for h in heads]
        m_new = [jnp.maximum(m[h], jnp.max(st[h], axis=0, keepdims=True)) for h in heads]
        p = [jnp.exp2(st[h] - m_new[h]) for h in heads]
        alpha = [jnp.exp2(m[h] - m_new[h]) for h in heads]
        l = [alpha[h] * l[h] + jnp.sum(p[h], axis=0, keepdims=True) for h in heads]
        acc = [alpha[h] * acc[h] + jnp.dot(vt_scr[h, j], p[h].astype(BF16), preferred_element_type=F32)
               for h in heads]
        return m_new, l, acc

    carry = ([jnp.full((1, 2 * tq), NEG_BIG, F32)] * nh, [jnp.zeros((1, 2 * tq), F32)] * nh,
             [jnp.zeros((DV, 2 * tq), F32)] * nh)
    first_near = jnp.maximum(qi - 1, 0)
    carry = lax.fori_loop(0, first_near, lambda j, c: step(j, c, None), carry)
    _, l, acc = lax.fori_loop(first_near, qi + 1, lambda j, c: step(j, c, j - qi + 1), carry)

    lam = _lambda(lam_ref, lam_init)
    for h in heads:
        ot = acc[h][:, :tq] / l[h][:, :tq] - lam * (acc[h][:, tq:] / l[h][:, tq:])
        yt = ot * lax.rsqrt(jnp.mean(ot * ot, axis=0, keepdims=True) + EPS) * sw_ref[...] * (1.0 - lam_init)
        g = g_ref[:, hs(h)]
        o_ref[:, hs(h)] = (yt.T * (g * _sigmoid(g))).astype(o_ref.dtype)


def prompt_bias_tiles(rel_bias, seq):
    tq = tk = min(ATTN_BLOCK, seq)
    assert tk >= REL_MAX_DIST
    d = _iota((tk, tq), 1) - _iota((tk, tq), 0)
    far = rel_bias.astype(F32)[REL_BUCKETS - 1][:, None, None]
    return jnp.stack([_rel_bias_tile(rel_bias, d + tk) - far, _rel_bias_tile(rel_bias, d) - far], axis=1) * LOG2E


def diff_attn_prompt(cols, batch, seq, lp, bias_tiles, lam_init):
    tq = tk = min(ATTN_BLOCK, seq)
    nq = seq // tq
    lam4 = jnp.stack([lp['diff_lam_q1'], lp['diff_lam_k1'], lp['diff_lam_q2'], lp['diff_lam_k2']]).astype(F32)
    const = lambda a: pl.BlockSpec(a.shape, lambda b, hg, i: (0,) * a.ndim)
    sw = lp['diff_subln'].reshape(DV, 1)
    nh = ATTN_HEADS
    wg = nh * DV
    per = GROUP_W // wg
    return pl.pallas_call(
        functools.partial(_diffattn_kernel, lam_init, tq, tk),
        grid=(batch, N_HEADS // nh, nq),
        in_specs=[pl.BlockSpec((tq, wg), lambda b, hg, i: (b * nq + i, BLK_DIFF_Q * per + hg)),
                  pl.BlockSpec((seq, wg), lambda b, hg, i: (b, BLK_DIFF_K * per + hg)),
                  pl.BlockSpec((seq, wg), lambda b, hg, i: (b, BLK_DIFF_V * per + hg)),
                  pl.BlockSpec((tq, wg), lambda b, hg, i: (b * nq + i, BLK_DIFF_G * per + hg)),
                  pl.BlockSpec((nh, 2, tk, tq), lambda b, hg, i: (hg, 0, 0, 0)), const(lam4), const(sw)],
        out_specs=pl.BlockSpec((tq, wg), lambda b, hg, i: (b * nq + i, hg)),
        out_shape=jax.ShapeDtypeStruct((batch * seq, GROUP_W), BF16),
        scratch_shapes=[pltpu.VMEM((nh, seq, DV), BF16), pltpu.VMEM((nh, seq // tk, DV, tk), BF16)],
        compiler_params=_params(("arbitrary", "arbitrary", "arbitrary"), BIG_VMEM_LIMIT),
        name="diff_attn_prompt",
    )(cols, cols, cols, cols, bias_tiles, lam4, sw)


def _diffdec_kernel(lam_init, PP, ts, page, pt_ref, q_ref, kn_ref, vn_ref, g_ref, *rest):
    k_refs = rest[:PP]
    v_refs = rest[PP:2 * PP]
    bias_ref, biasn_ref, lam_ref, sw_ref, o_ref, q_scr = rest[2 * PP:2 * PP + 6]
    state = rest[2 * PP + 6:]
    m_scr, l_scr, acc_scr = state[0::3], state[1::3], state[2::3]
    NCH = len(m_scr)
    per = PP // NCH
    nkc = per * page
    step_id = pl.program_id(1)
    W = N_HEADS * DV

    @pl.when(step_id == 0)
    def _():
        q = q_ref[...] * (DK ** -0.5)
        lane = _iota(q.shape, 1)
        pieces = []
        for h in range(N_HEADS):
            for mp in range(2):
                lo = h * DV + mp * DK
                pieces.append(jnp.where((lane >= lo) & (lane < lo + DK), q, 0.0))
        q_scr[...] = jnp.concatenate(pieces, axis=0).astype(BF16)
        for c in range(NCH):
            m_scr[c][...] = jnp.full(m_scr[c].shape, NEG_BIG, F32)
            l_scr[c][...] = jnp.zeros(l_scr[c].shape, F32)
            acc_scr[c][...] = jnp.zeros(acc_scr[c].shape, F32)

    def update(chains, kb, vb, bias):
        qb = q_scr[...]
        n = range(len(chains))
        s = [lax.dot_general(qb, kb[i], (((1,), (1,)), ((), ())), preferred_element_type=F32) + bias[i] for i in n]
        m_old = [m_scr[c][...] for c in chains]
        m_new = [jnp.maximum(m_old[i], jnp.max(s[i], axis=-1, keepdims=True)) for i in n]
        p = [jnp.exp(s[i] - m_new[i]) for i in n]
        alpha = [jnp.exp(m_old[i] - m_new[i]) for i in n]
        for i, c in enumerate(chains):
            l_scr[c][...] = alpha[i] * l_scr[c][...] + jnp.sum(p[i], axis=-1, keepdims=True)
            acc_scr[c][...] = alpha[i] * acc_scr[c][...] + jnp.dot(p[i].astype(BF16), vb[i], preferred_element_type=F32)
            m_scr[c][...] = m_new[i]

    def page_rows(ref):
        return jnp.concatenate([ref[pl.ds(h, page, stride=N_HEADS), :] for h in range(N_HEADS)], axis=1).astype(BF16)

    update(list(range(NCH)),
           [jnp.concatenate([page_rows(r) for r in k_refs[c * per:(c + 1) * per]], axis=0) for c in range(NCH)],
           [jnp.concatenate([page_rows(r) for r in v_refs[c * per:(c + 1) * per]], axis=0) for c in range(NCH)],
           [bias_ref[:, c * nkc:(c + 1) * nkc] for c in range(NCH)])

    @pl.when(step_id == pl.num_programs(1) - 1)
    def _():
        pad = jnp.zeros((page - ts, W), BF16)
        update([0], [jnp.concatenate([kn_ref[...].astype(BF16), pad], axis=0)],
               [jnp.concatenate([vn_ref[...].astype(BF16), pad], axis=0)], [biasn_ref[...]])
        m = m_scr[0][...]
        for c in range(1, NCH):
            m = jnp.maximum(m, m_scr[c][...])
        w = [jnp.exp(m_scr[c][...] - m) for c in range(NCH)]
        ln_all = sum(w[c] * l_scr[c][...] for c in range(NCH))
        acc_all = sum(w[c] * acc_scr[c][...] for c in range(NCH))
        lam = _lambda(lam_ref, lam_init)
        rows = 2 * ts
        for h in range(N_HEADS):
            a = acc_all[h * rows:(h + 1) * rows, h * DV:(h + 1) * DV]
            ln = ln_all[h * rows:(h + 1) * rows, :]
            o_ref[:, h * DV:(h + 1) * DV] = _diff_finish(
                a[:ts] / ln[:ts], a[ts:] / ln[ts:], lam, lam_init,
                g_ref[:, h * DV:(h + 1) * DV], sw_ref[...]).astype(o_ref.dtype)


def sample_bias_tiles(rel_bias, ts, n_pages, page):
    n_steps = n_pages // DECODE_PAGES
    past = n_pages * page
    nk = DECODE_PAGES * page
    assert nk >= REL_MAX_DIST
    t_of_row = jnp.tile(jnp.arange(ts, dtype=jnp.int32), N_HEADS * 2)
    h_of_row = jnp.repeat(jnp.arange(N_HEADS, dtype=jnp.int32), 2 * ts)
    head_sel = h_of_row[None, :, None] == jnp.arange(N_HEADS, dtype=jnp.int32)[:, None, None]

    def rows_bias(k_pos):
        rel = past + t_of_row[:, None] - k_pos[None, :]
        return jnp.sum(jnp.where(head_sel, _rel_bias_tile(rel_bias, rel), 0.0), axis=0)

    kk = jnp.arange(nk, dtype=jnp.int32)
    bias_steps = jnp.stack([rows_bias(kk), rows_bias((n_steps - 1) * nk + kk)])
    kn = jnp.arange(page, dtype=jnp.int32)
    bias_new = jnp.where(kn[None, :] < ts, rows_bias(past + kn), NEG_BIG)
    return bias_steps, bias_new


def diff_attn_sample(cols, batch, ts, cache_k, cache_v, layer, page_table, lp, bias_tiles, lam_init):
    page = cache_k.shape[2] // N_HEADS
    W = N_HEADS * DV
    n_pages = page_table.shape[1]
    PP = DECODE_PAGES
    n_steps = n_pages // PP
    nk = PP * page
    rows = N_HEADS * 2 * ts
    bias_steps, bias_new = bias_tiles
    lam4 = jnp.stack([lp['diff_lam_q1'], lp['diff_lam_k1'], lp['diff_lam_q2'], lp['diff_lam_k2']]).astype(F32)

    def page_spec(i):
        return pl.BlockSpec((None, None, page * N_HEADS, DV),
                            lambda b, s, pt: (layer, pt[b * n_pages + s * PP + i], 0, 0))

    col_spec = lambda blk: pl.BlockSpec((ts, GROUP_W), lambda b, s, pt: (b, blk))
    const = lambda a: pl.BlockSpec(a.shape, lambda b, s, pt: (0,) * a.ndim)
    grid_spec = pltpu.PrefetchScalarGridSpec(
        num_scalar_prefetch=1,
        grid=(batch, n_steps),
        in_specs=[col_spec(BLK_DIFF_Q), col_spec(BLK_DIFF_K), col_spec(BLK_DIFF_V), col_spec(BLK_DIFF_G)]
                 + [page_spec(i) for i in range(PP)] + [page_spec(i) for i in range(PP)]
                 + [pl.BlockSpec((None, rows, nk), lambda b, s, pt: (jnp.where(s == n_steps - 1, 1, 0), 0, 0)),
                    const(bias_new), const(lam4), pl.BlockSpec((1, DV), lambda b, s, pt: (0, 0))],
        out_specs=pl.BlockSpec((ts, GROUP_W), lambda b, s, pt: (b, 0)),
        scratch_shapes=[pltpu.VMEM((rows, W), BF16)]
                       + [pltpu.VMEM((rows, 1), F32), pltpu.VMEM((rows, 1), F32), pltpu.VMEM((rows, W), F32)] * DECODE_CHAINS)
    return pl.pallas_call(
        functools.partial(_diffdec_kernel, lam_init, PP, ts, page),
        grid_spec=grid_spec,
        out_shape=jax.ShapeDtypeStruct((batch * ts, GROUP_W), BF16),
        compiler_params=_params(("parallel", "arbitrary")),
        name="diff_attn_sample",
    )(page_table.reshape(-1), cols, cols, cols, cols, *([cache_k] * PP), *([cache_v] * PP),
      bias_steps, bias_new, lam4, lp['diff_subln'].reshape(1, DV))


RWKV_PAIRS = RWKV_HEADS // 2
SHIFT_PAD = 4 * GROUP_W


def _rwkv_kernel(C, tb, r_ref, k_ref, v_ref, misc_ref, g_ref, shift_ref, s0_ref,
                 mu_ref, w0_ref, ww2_ref, a0_ref, aw2_ref, kk_ref, ka_ref, rk_ref, nw_ref,
                 o_ref, sout_ref, S_scr, carry_scr):
    G = GROUP_W
    tblk = pl.program_id(1)

    @pl.when(tblk == 0)
    def _():
        S_scr[...] = s0_ref[...]
        carry_scr[...] = shift_ref[...]

    first_row = _iota((tb, 1), 0) == 0

    def token_shift(p, lo, width):
        prev = jnp.where(first_row, carry_scr[:, lo:lo + width], pltpu.roll(p, 1, axis=0))
        carry_scr[:, lo:lo + width] = p[tb - 1:tb, :]
        return p + mu_ref[:, lo:lo + width] * (prev - p)

    r = token_shift(r_ref[...], 0, G)
    k = token_shift(k_ref[...], G, G)
    v = token_shift(v_ref[...], 2 * G, G)
    lora = token_shift(misc_ref[:, :LANE], 3 * G, LANE)

    w_pre = w0_ref[...] + jnp.dot(jnp.tanh(lora).astype(BF16), ww2_ref[...], preferred_element_type=F32)
    lw = -jnp.exp(-_softplus(-w_pre) - 0.5)
    a = _sigmoid(a0_ref[...] + jnp.dot(lora.astype(BF16), aw2_ref[...], preferred_element_type=F32))

    kk = k * kk_ref[...]
    kk = kk / jnp.maximum(jnp.sqrt(_head_sums(kk * kk)), 1e-12)
    k2 = k * (1.0 + (a - 1.0) * ka_ref[...])
    bb = kk * a
    bonus = _head_sums(r * k2 * rk_ref[...]) * v

    row = _iota((tb, tb), 0)
    col = _iota((tb, tb), 1)
    same = (row // C) == (col // C)
    cum = _mm_split(jnp.where(same & (col <= row), 1.0, 0.0).astype(BF16), lw)
    tot = _mm_split(jnp.where(same, 1.0, 0.0).astype(BF16), lw)
    inv = jnp.exp(-cum)
    rest = jnp.exp(tot - cum)
    rt = r * jnp.exp(cum)
    kt = kk * jnp.exp(cum - lw)
    kd = k2 * inv
    bd = bb * inv
    kh = k2 * rest
    bh = bb * rest

    C2 = 2 * C
    nc = tb // C
    lane0 = _iota((C, LANE), 1) < RWKV_N
    rr = _iota((C2, C2), 0)
    cc = _iota((C2, C2), 1)
    strict = (rr % C) > (cc % C)
    incl = (rr % C) >= (cc % C)
    eye = jnp.where(rr == cc, 1.0, 0.0)
    eye_lane = _iota((LANE, LANE), 0) == _iota((LANE, LANE), 1)
    n_double = max(int(math.log2(C)) - 1, 0)
    probs = [(c, p) for c in range(nc) for p in range(RWKV_PAIRS)]

    def split(x, c, p):
        xs = x[c * C:(c + 1) * C, p * LANE:(p + 1) * LANE]
        return jnp.concatenate([jnp.where(lane0, xs, 0.0), jnp.where(lane0, 0.0, xs)], axis=0).astype(BF16)

    Lk = [split(kt, c, p) for c, p in probs]
    Lr = [split(rt, c, p) for c, p in probs]
    Rk = [split(kd, c, p) for c, p in probs]
    Rb = [split(bd, c, p) for c, p in probs]
    Vb = [split(v, c, p) for c, p in probs]
    Kh = [split(kh, c, p) for c, p in probs]
    Bh = [split(bh, c, p) for c, p in probs]
    if C2 % LANE == 0:
        G = [_mm_nt(jnp.concatenate([x, y], axis=0), jnp.concatenate([z, u], axis=0))
             for x, y, z, u in zip(Lk, Lr, Rk, Rb)]
        g_kk, g_kb = [g[:C2, :C2] for g in G], [g[:C2, C2:] for g in G]
        g_rk, g_rb = [g[C2:, :C2] for g in G], [g[C2:, C2:] for g in G]
    else:
        g_kk, g_kb = [_mm_nt(x, y) for x, y in zip(Lk, Rk)], [_mm_nt(x, y) for x, y in zip(Lk, Rb)]
        g_rk, g_rb = [_mm_nt(x, y) for x, y in zip(Lr, Rk)], [_mm_nt(x, y) for x, y in zip(Lr, Rb)]
    A_kr = [jnp.concatenate([jnp.where(strict, x, 0.0), jnp.where(incl, y, 0.0)], axis=0).astype(BF16)
            for x, y in zip(g_kk, g_rk)]
    A_rb = [jnp.where(incl, x, 0.0).astype(BF16) for x in g_rb]
    P = [jnp.where(strict, -x, 0.0) for x in g_kb]
    T = [eye + x for x in P]
    if n_double > 0:
        P = [_mm(x, x) for x in P]
    for it in range(n_double):
        if it < n_double - 1:
            TP = [_mm(jnp.concatenate([t, x], axis=0), x) for t, x in zip(T, P)]
            T = [t + y[:C2] for t, y in zip(T, TP)]
            P = [y[C2:] for y in TP]
        else:
            T = [t + _mm(t, x) for t, x in zip(T, P)]
    T = [t.astype(BF16) for t in T]
    AV = [_mm(x, y) for x, y in zip(A_kr, Vb)]
    TW = [_mm(t, jnp.concatenate([x, y[:C2].astype(BF16)], axis=1)).astype(BF16)
          for t, x, y in zip(T, Lk, AV)]
    AW = [_mm(x, y) for x, y in zip(A_rb, TW)]
    Wr = [x.astype(F32) - y[:, :LANE] for x, y in zip(Lr, AW)]
    O0 = [x[C2:] - y[:, LANE:] for x, y in zip(AV, AW)]
    BW = [_mm_tn(x, y) for x, y in zip(Bh, TW)]
    D0 = [_mm_tn(x, y) - z[:, LANE:] for x, y, z in zip(Kh, Vb, BW)]
    WN = [jnp.concatenate([x, y[:, :LANE]], axis=0).astype(BF16) for x, y in zip(Wr, BW)]
    gcol = [jnp.exp(jnp.sum(jnp.where(eye_lane, tot[c * C:c * C + 1, p * LANE:(p + 1) * LANE], 0.0),
                            axis=1, keepdims=True)) for c, p in probs]

    S = [S_scr[p] for p in range(RWKV_PAIRS)]
    o_rows = []
    for c in range(nc):
        o_lanes = []
        for p in range(RWKV_PAIRS):
            i = c * RWKV_PAIRS + p
            Rm = jnp.dot(WN[i], S[p].astype(BF16), preferred_element_type=F32)
            O = Rm[:C2] + O0[i]
            o_lanes.append(O[:C] + O[C:])
            S[p] = S[p] * gcol[i] - Rm[C2:] + D0[i]
        o_rows.append(jnp.concatenate(o_lanes, axis=1))
    for p in range(RWKV_PAIRS):
        S_scr[p] = S[p]

    o = jnp.concatenate(o_rows, axis=0)
    ms = _head_sums(o * o) * (1.0 / RWKV_N)
    y = o * lax.rsqrt(ms + EPS) * nw_ref[...] + bonus
    g = g_ref[...]
    o_ref[...] = (y * (g * _sigmoid(g))).astype(o_ref.dtype)

    @pl.when(tblk == pl.num_programs(1) - 1)
    def _():
        sout_ref[...] = S_scr[...]


def rwkv_mixer(cols, batch, seq, lp, s0_pairs, shift_pad):
    C = min(64, seq)
    tb = min(SEQ_BLOCK, seq)
    nt = seq // tb
    G = GROUP_W
    mu = lp['rwkv_mu']
    mu_pad = jnp.concatenate([mu, jnp.zeros((SHIFT_PAD - mu.shape[0],), F32)]).reshape(1, SHIFT_PAD)
    ww2 = jnp.zeros((LANE, G), F32).at[:RWKV_LORA].set(lp['rwkv_w_w2']).astype(BF16)
    aw2 = jnp.zeros((LANE, G), F32).at[RWKV_LORA:].set(lp['rwkv_a_w2']).astype(BF16)
    vec = lambda a: a.reshape(1, G).astype(F32)
    params = [mu_pad, vec(lp['rwkv_w0']), ww2, vec(lp['rwkv_a0']), aw2, vec(lp['rwkv_k_k']), vec(lp['rwkv_k_a']),
              vec(lp['rwkv_r_k']), jnp.tile(lp['rwkv_norm'], RWKV_HEADS).reshape(1, G)]
    row_map = lambda bl: (lambda b, t: (b * nt + t, bl))
    full = lambda a: pl.BlockSpec(a.shape, lambda b, t: (0,) * a.ndim)
    return pl.pallas_call(
        functools.partial(_rwkv_kernel, C, tb),
        grid=(batch, nt),
        in_specs=[pl.BlockSpec((tb, G), row_map(BLK_RWKV_R)), pl.BlockSpec((tb, G), row_map(BLK_RWKV_K)),
                  pl.BlockSpec((tb, G), row_map(BLK_RWKV_V)), pl.BlockSpec((tb, G), row_map(BLK_MISC)),
                  pl.BlockSpec((tb, G), row_map(BLK_RWKV_G)),
                  pl.BlockSpec((None, 1, SHIFT_PAD), lambda b, t: (b, 0, 0)),
                  pl.BlockSpec((None, RWKV_PAIRS, LANE, LANE), lambda b, t: (b, 0, 0, 0))]
                 + [full(a) for a in params],
        out_specs=[pl.BlockSpec((tb, G), lambda b, t: (b * nt + t, 0)),
                   pl.BlockSpec((None, RWKV_PAIRS, LANE, LANE), lambda b, t: (b, 0, 0, 0))],
        out_shape=[jax.ShapeDtypeStruct((batch * seq, G), BF16),
                   jax.ShapeDtypeStruct((batch, RWKV_PAIRS, LANE, LANE), F32)],
        scratch_shapes=[pltpu.VMEM((RWKV_PAIRS, LANE, LANE), F32), pltpu.VMEM((1, SHIFT_PAD), F32)],
        compiler_params=_params(("parallel", "arbitrary")),
        name="rwkv7",
    )(cols, cols, cols, cols, cols, shift_pad, s0_pairs, *params)


def rwkv_state_to_pairs(s):
    b = s.shape[0]
    st = jnp.swapaxes(s, -1, -2).reshape(b, RWKV_PAIRS, 2, RWKV_N, RWKV_N)
    z = jnp.zeros_like(st[:, :, 0])
    top = jnp.concatenate([st[:, :, 0], z], axis=-1)
    bot = jnp.concatenate([z, st[:, :, 1]], axis=-1)
    return jnp.concatenate([top, bot], axis=-2)


def rwkv_pairs_to_state(sp):
    b = sp.shape[0]
    h0 = sp[:, :, :RWKV_N, :RWKV_N]
    h1 = sp[:, :, RWKV_N:, RWKV_N:]
    st = jnp.stack([h0, h1], axis=2).reshape(b, RWKV_HEADS, RWKV_N, RWKV_N)
    return jnp.swapaxes(st, -1, -2)


def _group_layer(x, scale, shift, gate, rows_per_mod, batch, seq, lp, w_pad, w_o_bf, final_w, final,
                 s_gla, s_ret, s_rwkv, s_shift, pos, diff_fn, tm_in, tm_out, k_all, v_all, layer):
    cols, k_all, v_all = in_proj(x, scale, shift, lp['norm_w'], w_pad, k_all, v_all, layer, rows_per_mod, tm_in, IN_TN)
    oa, n_gla = gla_mixer(cols, batch, seq, lp, s_gla)
    ob = diff_fn(cols)
    oc, n_ret = ret_mixer(cols, batch, seq, lp, s_ret, pos)
    od, n_rwkv = rwkv_mixer(cols, batch, seq, lp, s_rwkv, s_shift)
    y = out_proj(x, gate, oa, ob, oc, od, w_o_bf, layer, final_w, rows_per_mod, tm_out, final)
    last = cols.reshape(batch, seq, COLS_W)[:, -1]
    n_shift = jnp.concatenate([last[:, BLK_RWKV_R * GROUP_W:(BLK_RWKV_V + 1) * GROUP_W],
                               last[:, BLK_MISC * GROUP_W:BLK_MISC * GROUP_W + 2 * RWKV_LORA]], axis=-1)
    return y, k_all, v_all, (n_gla.reshape(batch, N_HEADS, DK, DV), n_ret.reshape(batch, N_HEADS, DK, DV),
                             rwkv_pairs_to_state(n_rwkv), n_shift)


def kernel(x_prompt, x_sample, cache_k, cache_v, state_gla, state_ret, state_rwkv, state_shift, page_table,
           c_prompt, c_sample, rel_bias, w_ada, b_ada, norm_w, w_in, w_o, gla_w_a2, gla_b_a, gla_norm,
           diff_lam_q1, diff_lam_k1, diff_lam_q2, diff_lam_k2, diff_subln, ret_norm, rwkv_mu, rwkv_w0, rwkv_w_w2,
           rwkv_a0, rwkv_a_w2, rwkv_k_k, rwkv_k_a, rwkv_r_k, rwkv_norm, final_norm):
    Bp, Sp, D = x_prompt.shape
    Bs, Ts, _ = x_sample.shape
    depth = w_in.shape[0]
    n_pool, page = cache_k.shape[1], cache_k.shape[2]
    past_len = page_table.shape[1] * page
    ck = cache_k.reshape(depth, n_pool, page * N_HEADS, DV)
    cv = cache_v.reshape(depth, n_pool, page * N_HEADS, DV)
    pos_p = jnp.arange(Sp, dtype=jnp.int32)
    pos_s = past_len + jnp.arange(Ts, dtype=jnp.int32)
    c_all = jnp.concatenate([c_prompt, c_sample], axis=0)
    xp = x_prompt.reshape(Bp * Sp, D)
    xs = x_sample.reshape(Bs * Ts, D)
    zeros_dec = jnp.zeros((Bp, N_HEADS * DK, DV), F32)
    zeros_rwkv = jnp.zeros((Bp, RWKV_PAIRS, LANE, LANE), F32)
    zeros_shift = jnp.zeros((Bp, 1, SHIFT_PAD), F32)
    bias_p = prompt_bias_tiles(rel_bias, Sp)
    bias_s = sample_bias_tiles(rel_bias, Ts, page_table.shape[1], page)
    kp = jnp.zeros((depth, Bp * Sp * N_HEADS, DV), F32)
    vp = jnp.zeros((depth, Bp * Sp * N_HEADS, DV), F32)
    ks = jnp.zeros((depth, Bs * Ts * N_HEADS, DV), F32)
    vs = jnp.zeros((depth, Bs * Ts * N_HEADS, DV), F32)
    w_pad_all = prep_in_weights(w_in)
    w_o_all = w_o.astype(BF16)
    new_p, new_s = [], []
    for l in range(depth):
        lp = dict(norm_w=norm_w[l], gla_w_a2=gla_w_a2[l], gla_b_a=gla_b_a[l], gla_norm=gla_norm[l],
                  diff_lam_q1=diff_lam_q1[l], diff_lam_k1=diff_lam_k1[l], diff_lam_q2=diff_lam_q2[l],
                  diff_lam_k2=diff_lam_k2[l], diff_subln=diff_subln[l], ret_norm=ret_norm[l],
                  rwkv_mu=rwkv_mu[l], rwkv_w0=rwkv_w0[l], rwkv_w_w2=rwkv_w_w2[l], rwkv_a0=rwkv_a0[l],
                  rwkv_a_w2=rwkv_a_w2[l], rwkv_k_k=rwkv_k_k[l], rwkv_k_a=rwkv_k_a[l], rwkv_r_k=rwkv_r_k[l],
                  rwkv_norm=rwkv_norm[l])
        final = l == depth - 1
        lam_init = 0.8 - 0.6 * math.exp(-0.3 * l)
        mod = ada_mod(c_all, w_ada, b_ada, l)
        shift, scale, gate = mod[:, :D], mod[:, D:2 * D], mod[:, 2 * D:]
        w_pad, w_o_bf = w_pad_all, w_o_all

        per_batch = lambda a: a[:Bp].reshape(Bp, 1, D)
        xp, kp, vp, st_p = _group_layer(
            xp, per_batch(scale), per_batch(shift), per_batch(gate), Sp, Bp, Sp, lp, w_pad, w_o_bf, final_norm, final,
            zeros_dec, zeros_dec, zeros_rwkv, zeros_shift, pos_p,
            lambda cols: diff_attn_prompt(cols, Bp, Sp, lp, bias_p, lam_init), min(IN_TM, Sp), min(OUT_TM, Sp),
            kp, vp, l)

        per_row = lambda a: jnp.repeat(a[Bp:], Ts, axis=0)
        sh = state_shift[l]
        sh_pad = jnp.concatenate([sh, jnp.zeros((Bs, SHIFT_PAD - sh.shape[1]), F32)], axis=1).reshape(Bs, 1, SHIFT_PAD)
        xs, ks, vs, st_s = _group_layer(
            xs, per_row(scale), per_row(shift), per_row(gate), 1, Bs, Ts, lp, w_pad, w_o_bf, final_norm, final,
            state_gla[l].reshape(Bs, N_HEADS * DK, DV), state_ret[l].reshape(Bs, N_HEADS * DK, DV),
            rwkv_state_to_pairs(state_rwkv[l]), sh_pad, pos_s,
            lambda cols: diff_attn_sample(cols, Bs, Ts, ck, cv, l, page_table, lp, bias_s, lam_init),
            Bs * Ts, Bs * Ts, ks, vs, l)
        new_p.append(st_p)
        new_s.append(st_s)
    stack = lambda states, i: jnp.stack([s[i] for s in states])
    heads = lambda a, b, t: a.reshape(depth, b, t, N_HEADS, DV)
    return (xp.reshape(Bp, Sp, D), xs.reshape(Bs, Ts, D),
            heads(kp, Bp, Sp), heads(vp, Bp, Sp), heads(ks, Bs, Ts), heads(vs, Bs, Ts),
            stack(new_p, 0), stack(new_s, 0), stack(new_p, 1), stack(new_s, 1),
            stack(new_p, 2), stack(new_s, 2), stack(new_p, 3), stack(new_s, 3))
```

```python
import functools
import math

import jax
import jax.numpy as jnp
from jax import lax
from jax.experimental import pallas as pl
from jax.experimental.pallas import tpu as pltpu

F32 = jnp.float32
BF16 = jnp.bfloat16
HIGHEST = lax.Precision.HIGHEST

D_MODEL = 2048
GROUP_W = 512
N_HEADS = 4
DK = 64
DV = 128
GLA_RANK = 16
GLA_TAU = 16.0
RWKV_N = 64
RWKV_HEADS = 8
RWKV_LORA = 64
ROPE_BASE = 10000.0
REL_BUCKETS = 32
REL_MAX_DIST = 128
CHUNK = 16
GLA_DIAG = 8
EPS = 1e-6
NEG_BIG = -1e30
LOG2E = math.log2(math.e)

LANE = 128
VMEM_LIMIT = 48 * 1024 * 1024
ATTN_BLOCK = 512
ATTN_HEADS = 4
DECODE_PAGES = 16
DECODE_CHAINS = 2
SEQ_BLOCK = 256
IN_TM, IN_TN = 1024, 1280
BIG_VMEM_LIMIT = 56 * 1024 * 1024
NORM_ROWS = 128
OUT_TM = 256
ADA_TN = 1536

COLS_W = 15 * GROUP_W
BLK_GLA_QK, BLK_GLA_V, BLK_GLA_G = 0, 1, 2
BLK_DIFF_Q, BLK_DIFF_K, BLK_DIFF_V, BLK_DIFF_G = 3, 4, 5, 6
BLK_RET_QK, BLK_RET_V, BLK_RET_G = 7, 8, 9
BLK_RWKV_R, BLK_RWKV_K, BLK_RWKV_V, BLK_RWKV_G = 10, 11, 12, 13
BLK_MISC = 14


def _mm(a, b):
    return jnp.dot(a.astype(BF16), b.astype(BF16), preferred_element_type=F32)


def _mm_nt(a, b):
    return lax.dot_general(a.astype(BF16), b.astype(BF16), (((1,), (1,)), ((), ())), preferred_element_type=F32)


def _mm_tn(a, b):
    return lax.dot_general(a.astype(BF16), b.astype(BF16), (((0,), (0,)), ((), ())), preferred_element_type=F32)


def _mm_exact(a, b):
    return jnp.dot(a, b, precision=HIGHEST, preferred_element_type=F32)


def _mm_tn_exact(a, b):
    return lax.dot_general(a, b, (((0,), (0,)), ((), ())), precision=HIGHEST, preferred_element_type=F32)


def _split2(x):
    hi = x.astype(BF16)
    return hi, (x - hi.astype(F32)).astype(BF16)


def _mm_split(a, x):
    hi, lo = _split2(x)
    return jnp.dot(a, hi, preferred_element_type=F32) + jnp.dot(a, lo, preferred_element_type=F32)


def _head_sums(x):
    ind = jnp.where((_iota((LANE, LANE), 0) // RWKV_N) == (_iota((LANE, LANE), 1) // RWKV_N), 1.0, 0.0).astype(BF16)
    hi, lo = _split2(x)
    parts = []
    for i in range(x.shape[1] // LANE):
        sl = slice(i * LANE, (i + 1) * LANE)
        parts.append(jnp.dot(hi[:, sl], ind, preferred_element_type=F32) + jnp.dot(lo[:, sl], ind, preferred_element_type=F32))
    return jnp.concatenate(parts, axis=1)


def _sigmoid(x):
    return 1.0 / (1.0 + jnp.exp(-x))


def _softplus(x):
    return jnp.maximum(x, 0.0) + jnp.log1p(jnp.exp(-jnp.abs(x)))


def _iota(shape, dim):
    return lax.broadcasted_iota(jnp.int32, shape, dim)


def _params(sem, vmem_limit=VMEM_LIMIT):
    return pltpu.CompilerParams(dimension_semantics=sem, vmem_limit_bytes=vmem_limit)


def _ada_kernel(c_ref, w_ref, b_ref, o_ref):
    c = c_ref[...]
    o_ref[...] = _mm(c * _sigmoid(c), w_ref[...]) + b_ref[...]


def ada_mod(c, w_ada, b_ada, layer, tn=ADA_TN):
    rows, d = c.shape
    depth, _, n = w_ada.shape
    return pl.pallas_call(
        _ada_kernel,
        grid=(n // tn,),
        in_specs=[pl.BlockSpec((rows, d), lambda j: (0, 0)),
                  pl.BlockSpec((None, d, tn), lambda j: (layer, 0, j)),
                  pl.BlockSpec((None, 1, tn), lambda j: (layer, 0, j))],
        out_specs=pl.BlockSpec((rows, tn), lambda j: (0, j)),
        out_shape=jax.ShapeDtypeStruct((rows, n), F32),
        compiler_params=_params(("arbitrary",)),
        name="ada_mod",
    )(c, w_ada, b_ada.reshape(depth, 1, n))


def _inproj_kernel(k_tile, k_lo, v_tile, v_lo, x_ref, scale_ref, shift_ref, nw_ref, w_ref, kin_ref, vin_ref,
                   o_ref, k_ref, v_ref, h_ref):
    del kin_ref, vin_ref
    j = pl.program_id(1)

    tm = o_ref.shape[0]
    rows = min(NORM_ROWS, tm)
    per_row_mod = scale_ref.shape[0] == tm

    @pl.when(j == 0)
    def _():
        def norm_rows(i, carry):
            sl = pl.ds(pl.multiple_of(i * rows, rows), rows)
            x = x_ref[sl, :]
            y = x * lax.rsqrt(jnp.mean(x * x, axis=-1, keepdims=True) + EPS) * nw_ref[...]
            scale = scale_ref[sl, :] if per_row_mod else scale_ref[...]
            shift = shift_ref[sl, :] if per_row_mod else shift_ref[...]
            h_ref[sl, :] = (y * (1.0 + scale) + shift).astype(BF16)
            return carry

        lax.fori_loop(0, tm // rows, norm_rows, 0)

    o_ref[...] = lax.dot_general(h_ref[...], w_ref[...], (((1,), (1,)), ((), ())), preferred_element_type=F32)

    def heads_to_rows(dst_ref, lo):
        for h in range(N_HEADS):
            dst_ref[pl.ds(h, tm, stride=N_HEADS), :] = o_ref[:, lo + h * DV:lo + (h + 1) * DV]

    @pl.when(j == k_tile)
    def _():
        heads_to_rows(k_ref, k_lo)

    @pl.when(j == v_tile)
    def _():
        heads_to_rows(v_ref, v_lo)


def in_proj(x, scale, shift, norm_w, w_pad, k_all, v_all, layer, rows_per_mod, tm, tn):
    m, d = x.shape
    n = w_pad.shape[1]
    k_lo, v_lo = BLK_DIFF_K * GROUP_W, BLK_DIFF_V * GROUP_W
    k_tile, v_tile = k_lo // tn, v_lo // tn
    assert (k_lo + GROUP_W - 1) // tn == k_tile and (v_lo + GROUP_W - 1) // tn == v_tile
    if rows_per_mod == 1:
        mod_spec = pl.BlockSpec((tm, d), lambda i, j: (i, 0))
    else:
        per = rows_per_mod // tm
        mod_spec = pl.BlockSpec((None, 1, d), lambda i, j: (i // per, 0, 0))
    kv_spec = pl.BlockSpec((None, tm * N_HEADS, DV), lambda i, j: (layer, i, 0), pipeline_mode=pl.Buffered(1))
    return pl.pallas_call(
        functools.partial(_inproj_kernel, k_tile, k_lo - k_tile * tn, v_tile, v_lo - v_tile * tn),
        grid=(m // tm, n // tn),
        in_specs=[pl.BlockSpec((tm, d), lambda i, j: (i, 0)),
                  mod_spec, mod_spec,
                  pl.BlockSpec((1, d), lambda i, j: (0, 0)),
                  pl.BlockSpec((None, tn, d), lambda i, j: (layer, j, 0)),
                  pl.BlockSpec(memory_space=pl.ANY), pl.BlockSpec(memory_space=pl.ANY)],
        out_specs=[pl.BlockSpec((tm, tn), lambda i, j: (i, j)), kv_spec, kv_spec],
        out_shape=[jax.ShapeDtypeStruct((m, n), F32), jax.ShapeDtypeStruct(k_all.shape, F32),
                   jax.ShapeDtypeStruct(v_all.shape, F32)],
        input_output_aliases={5: 1, 6: 2},
        scratch_shapes=[pltpu.VMEM((tm, d), BF16)],
        compiler_params=_params(("parallel", "arbitrary"), BIG_VMEM_LIMIT),
        name="in_proj",
    )(x, scale, shift, norm_w.reshape(1, d), w_pad, k_all, v_all)


def _outproj_kernel(final, x_ref, gate_ref, oa_ref, ob_ref, oc_ref, od_ref, w_ref, fw_ref, y_ref):
    acc = jnp.dot(oa_ref[...], w_ref[0 * GROUP_W:1 * GROUP_W, :], preferred_element_type=F32)
    acc += jnp.dot(ob_ref[...], w_ref[1 * GROUP_W:2 * GROUP_W, :], preferred_element_type=F32)
    acc += jnp.dot(oc_ref[...], w_ref[2 * GROUP_W:3 * GROUP_W, :], preferred_element_type=F32)
    acc += jnp.dot(od_ref[...], w_ref[3 * GROUP_W:4 * GROUP_W, :], preferred_element_type=F32)
    y = x_ref[...] + gate_ref[...] * acc
    if final:
        y = y * lax.rsqrt(jnp.mean(y * y, axis=-1, keepdims=True) + EPS) * fw_ref[...]
    y_ref[...] = y


def out_proj(x, gate, oa, ob, oc, od, w_o, layer, final_w, rows_per_mod, tm, final):
    m, d = x.shape
    if rows_per_mod == 1:
        mod_spec = pl.BlockSpec((tm, d), lambda i: (i, 0))
    else:
        per = rows_per_mod // tm
        mod_spec = pl.BlockSpec((None, 1, d), lambda i: (i // per, 0, 0))
    o_spec = pl.BlockSpec((tm, GROUP_W), lambda i: (i, 0))
    return pl.pallas_call(
        functools.partial(_outproj_kernel, final),
        grid=(m // tm,),
        in_specs=[pl.BlockSpec((tm, d), lambda i: (i, 0)), mod_spec,
                  o_spec, o_spec, o_spec, o_spec,
                  pl.BlockSpec((None, d, d), lambda i: (layer, 0, 0)),
                  pl.BlockSpec((1, d), lambda i: (0, 0))],
        out_specs=pl.BlockSpec((tm, d), lambda i: (i, 0)),
        out_shape=jax.ShapeDtypeStruct((m, d), F32),
        compiler_params=_params(("parallel",)),
        name="out_proj",
    )(x, gate, oa, ob, oc, od, w_o, final_w.reshape(1, d))


def _head_rms_gate(o, g, nw):
    y = o * lax.rsqrt(jnp.mean(o * o, axis=-1, keepdims=True) + EPS) * nw
    return y * (g * _sigmoid(g))


def _rope128(x, cos, sin_signed):
    half = DK // 2
    up = pltpu.roll(x, LANE - half, axis=1)
    down = pltpu.roll(x, half, axis=1)
    first = (_iota(x.shape, 1) % DK) < half
    return x * cos + jnp.where(first, up, down) * sin_signed


def _decay_kernel(mode, C, tb, qk_ref, v_ref, g_ref, aux_ref, p1_ref, p2_ref, nw_ref, s0_ref,
                  o_ref, sout_ref, S_scr, k_scr, b_scr):
    HK = N_HEADS * DK
    tblk = pl.program_id(1)

    @pl.when(tblk == 0)
    def _():
        S_scr[...] = s0_ref[...]

    q = qk_ref[:, :HK]
    k = qk_ref[:, HK:]
    if mode == "gla":
        x = jnp.dot(aux_ref[:, LANE:2 * LANE].astype(BF16), p1_ref[...], preferred_element_type=F32) + p2_ref[...]
        la = -_softplus(-x) * (1.0 / GLA_TAU)
        q = q * (DK ** -0.5)
    else:
        cos = aux_ref[:, :HK]
        sin = aux_ref[:, HK:]
        q = jnp.concatenate([_rope128(q[:, i * LANE:(i + 1) * LANE], cos[:, i * LANE:(i + 1) * LANE],
                                      sin[:, i * LANE:(i + 1) * LANE]) for i in range(HK // LANE)], axis=1)
        k = jnp.concatenate([_rope128(k[:, i * LANE:(i + 1) * LANE], cos[:, i * LANE:(i + 1) * LANE],
                                      sin[:, i * LANE:(i + 1) * LANE]) for i in range(HK // LANE)], axis=1)
        k = k * (DK ** -0.5)

    nc = tb // C
    lane_head = _iota((C, HK), 1) // DK
    heads_rows = lambda x: jnp.concatenate([jnp.where(lane_head == h, x, 0.0) for h in range(N_HEADS)],
                                           axis=0).astype(BF16)
    diag_blocks = lambda d: jnp.concatenate([d[h * DK:(h + 1) * DK, h * DV:(h + 1) * DV] for h in range(N_HEADS)], axis=0)
    heads_lanes = lambda x: jnp.concatenate([x[h * C:(h + 1) * C, :] for h in range(N_HEADS)], axis=1)
    row = _iota((tb, tb), 0)
    col = _iota((tb, tb), 1)

    if mode == "gla":
        same = (row // C) == (col // C)
        b = _mm_split(jnp.where(same & (col <= row), 1.0, 0.0).astype(BF16), la)
        btot = _mm_split(jnp.where(same, 1.0, 0.0).astype(BF16), la)
        k_scr[...] = k
        b_scr[...] = b
        qe = q * jnp.exp(b)
        ke = k * jnp.exp(btot - b)
        sel = jnp.where(_iota((tb, nc * DV), 0) // C == _iota((tb, nc * DV), 1) // DV, 1.0, 0.0).astype(BF16)
        la_hi, la_lo = _split2(la)
        tn = (((0,), (0,)), ((), ()))
        gcols = jnp.exp(lax.dot_general(la_hi, sel, tn, preferred_element_type=F32)
                        + lax.dot_general(la_lo, sel, tn, preferred_element_type=F32))
        ind = jnp.where((_iota((HK, N_HEADS * DV), 0) // DK) == (_iota((HK, N_HEADS * DV), 1) // DV), 1.0, 0.0).astype(BF16)
        D = min(GLA_DIAG, C)
        halves = C // D
        trow = _iota((D, HK), 0)
        chunks = range(nc)
        X = []
        for c in chunks:
            xs = []
            for a in range(halves):
                r0 = c * C + a * D
                qd, bd = q[r0:r0 + D], b[r0:r0 + D]
                for s in range(D):
                    r = r0 + s
                    e = jnp.where(trow >= s, jnp.exp(jnp.minimum(bd - b_scr[r:r + 1, :], 0.0)), 0.0)
                    xs.append((qd * k_scr[r:r + 1, :] * e).astype(BF16))
            X.append(jnp.concatenate(xs, axis=0))
        R = [jnp.dot(x, ind, preferred_element_type=F32) for x in X]
        o_blocks = []
        for c in chunks:
            blocks = []
            for a in range(halves):
                r0 = c * C + a * D
                base = a * D * D
                o_d = R[c][base:base + D, :] * v_ref[r0:r0 + 1, :]
                for s in range(1, D):
                    o_d = o_d + R[c][base + s * D:base + (s + 1) * D, :] * v_ref[r0 + s:r0 + s + 1, :]
                blocks.append(o_d)
            o_blocks.append(blocks)
        if halves == 2:
            lane_head_d = _iota((D, HK), 1) // DK
            qa, ka = [], []
            for c in chunks:
                r0 = c * C
                anchor = b_scr[r0 + D - 1:r0 + D, :]
                qs = q[r0 + D:r0 + C] * jnp.exp(b[r0 + D:r0 + C] - anchor)
                qa.append(jnp.concatenate([jnp.where(lane_head_d == h, qs, 0.0) for h in range(N_HEADS)], axis=0))
                ka.append(k[r0:r0 + D] * jnp.exp(anchor - b[r0:r0 + D]))
            rows_q = nc * N_HEADS * D
            p_off = _mm_nt(jnp.concatenate(qa, axis=0), jnp.concatenate(ka, axis=0))
            own = (_iota((rows_q, nc * D), 0) // (N_HEADS * D)) == (_iota((rows_q, nc * D), 1) // D)
            v_first = jnp.concatenate([v_ref[c * C:c * C + D, :] for c in chunks], axis=0)
            o_off = _mm(jnp.where(own, p_off, 0.0), v_first)
            for c in chunks:
                base = c * N_HEADS * D
                o_blocks[c][1] = o_blocks[c][1] + jnp.concatenate(
                    [o_off[base + h * D:base + (h + 1) * D, h * DV:(h + 1) * DV] for h in range(N_HEADS)], axis=1)
        o_chunks = [jnp.concatenate(blk, axis=0) for blk in o_blocks]
        v_rows = lambda c: jnp.concatenate([v_ref[c * C:(c + 1) * C, h * DV:(h + 1) * DV] for h in range(N_HEADS)], axis=0)
        dS = [_mm_tn(heads_rows(ke[c * C:(c + 1) * C]), v_rows(c)) for c in chunks]
        L = [heads_rows(qe[c * C:(c + 1) * C]) for c in chunks]
        S = S_scr[...]
        for c in chunks:
            o_chunks[c] = o_chunks[c] + heads_lanes(jnp.dot(L[c], S.astype(BF16), preferred_element_type=F32))
            S = S * gcols[:, c * DV:(c + 1) * DV] + dS[c]
        S_scr[...] = S
        o = jnp.concatenate(o_chunks, axis=0)
    else:
        lg = p2_ref[...]
        tcol = _iota((tb, 1), 0).astype(F32)
        qe = q * jnp.exp((tcol + 1.0) * lg)
        ke = k * jnp.exp((tb - 1.0 - tcol) * lg)
        dist = (row - col).astype(F32)
        kb = k.astype(BF16)
        Q = heads_rows(q)
        o_heads = []
        for h in range(N_HEADS):
            s = lax.dot_general(Q[h * tb:(h + 1) * tb], kb, (((1,), (1,)), ((), ())), preferred_element_type=F32)
            dec = jnp.where(row >= col, jnp.exp(jnp.minimum(dist * lg[:, h * DK:h * DK + 1], 0.0)), 0.0)
            o_heads.append(_mm(s * dec, v_ref[:, h * DV:(h + 1) * DV]))
        S = S_scr[...]
        o = jnp.concatenate(o_heads, axis=1) + heads_lanes(jnp.dot(heads_rows(qe), S.astype(BF16),
                                                                   preferred_element_type=F32))
        S_scr[...] = S * jnp.exp(tb * p1_ref[...]) + diag_blocks(_mm_tn(ke, v_ref[...]))

    for h in range(N_HEADS):
        sl = slice(h * DV, (h + 1) * DV)
        o_ref[:, sl] = _head_rms_gate(o[:, sl], g_ref[:, sl], nw_ref[...]).astype(o_ref.dtype)

    @pl.when(tblk == pl.num_programs(1) - 1)
    def _():
        sout_ref[...] = S_scr[...]


def decay_mixer(mode, cols, batch, seq, blk_qk, blk_v, blk_g, aux, aux_is_cols, p1, p2, norm_w, s0):
    tb = min(SEQ_BLOCK, seq)
    C = min(CHUNK, seq) if mode == "gla" else tb
    nt = seq // tb
    HK = N_HEADS * DK
    row_map = lambda bl: (lambda b, t: (b * nt + t, bl))
    if aux_is_cols:
        aux_spec = pl.BlockSpec((tb, GROUP_W), row_map(BLK_MISC))
    else:
        aux_spec = pl.BlockSpec((tb, aux.shape[1]), lambda b, t: (t, 0))
    full = lambda a: pl.BlockSpec(a.shape, lambda b, t: (0,) * a.ndim)
    return pl.pallas_call(
        functools.partial(_decay_kernel, mode, C, tb),
        grid=(batch, nt),
        in_specs=[pl.BlockSpec((tb, GROUP_W), row_map(blk_qk)),
                  pl.BlockSpec((tb, GROUP_W), row_map(blk_v)),
                  pl.BlockSpec((tb, GROUP_W), row_map(blk_g)),
                  aux_spec, full(p1), full(p2), full(norm_w),
                  pl.BlockSpec((None, HK, DV), lambda b, t: (b, 0, 0))],
        out_specs=[pl.BlockSpec((tb, GROUP_W), lambda b, t: (b * nt + t, 0)),
                   pl.BlockSpec((None, HK, DV), lambda b, t: (b, 0, 0))],
        out_shape=[jax.ShapeDtypeStruct((batch * seq, GROUP_W), BF16),
                   jax.ShapeDtypeStruct((batch, HK, DV), F32)],
        scratch_shapes=[pltpu.VMEM((HK, DV), F32)] + [pltpu.VMEM((tb, HK), F32)] * 2,
        compiler_params=_params(("parallel", "arbitrary")),
        name="decay_" + mode,
    )(cols, cols, cols, aux, p1, p2, norm_w, s0)


def pad_in_cols(w):
    z = jnp.zeros(w.shape[:-1] + (COLS_W - 7312,), w.dtype)
    return jnp.concatenate([w[..., 0:1024], w[..., 1040:1552], w[..., 1552:5136], w[..., 5136:6672],
                            w[..., 6800:7312], w[..., 6672:6800], w[..., 1024:1040], z], axis=-1)


IN_W = 7312
ALPHA_LO = 1024
RWKV_LORA_LO, RWKV_G_LO = 6672, 6800


def _wprep_kernel(w_ref, o_ref):
    lanes = w_ref.shape[1]
    n_main = RWKV_LORA_LO - ALPHA_LO - GLA_RANK
    g_dst = BLK_RWKV_G * GROUP_W
    misc = BLK_MISC * GROUP_W
    o_ref[:ALPHA_LO, :] = w_ref[:ALPHA_LO, :].astype(BF16)
    o_ref[ALPHA_LO:ALPHA_LO + n_main, :] = w_ref[ALPHA_LO + GLA_RANK:RWKV_LORA_LO, :].astype(BF16)
    o_ref[g_dst:g_dst + GROUP_W, :] = w_ref[RWKV_G_LO:IN_W, :].astype(BF16)
    o_ref[misc:misc + LANE, :] = w_ref[RWKV_LORA_LO:RWKV_G_LO, :].astype(BF16)
    o_ref[misc + LANE:misc + LANE + GLA_RANK, :] = w_ref[ALPHA_LO:ALPHA_LO + GLA_RANK, :].astype(BF16)
    o_ref[misc + LANE + GLA_RANK:, :] = jnp.zeros((GROUP_W - LANE - GLA_RANK, lanes), BF16)


def prep_in_weights(w_in_t, lanes=256):
    depth, n, d = w_in_t.shape
    assert n == IN_W
    return pl.pallas_call(
        _wprep_kernel,
        grid=(depth, d // lanes),
        in_specs=[pl.BlockSpec((None, n, lanes), lambda l, i: (l, 0, i))],
        out_specs=pl.BlockSpec((None, COLS_W, lanes), lambda l, i: (l, 0, i)),
        out_shape=jax.ShapeDtypeStruct((depth, COLS_W, d), BF16),
        compiler_params=_params(("parallel", "parallel")),
        name="prep_in_weights",
    )(w_in_t)


def gla_mixer(cols, batch, seq, lp, s0):
    HK = N_HEADS * DK
    w_a2 = jnp.zeros((LANE, HK), F32).at[:GLA_RANK].set(lp['gla_w_a2']).astype(BF16)
    b_a = lp['gla_b_a'].reshape(1, HK)
    return decay_mixer("gla", cols, batch, seq, BLK_GLA_QK, BLK_GLA_V, BLK_GLA_G, cols, True,
                       w_a2, b_a, lp['gla_norm'].reshape(1, DV), s0)


def ret_mixer(cols, batch, seq, lp, s0, pos):
    HK = N_HEADS * DK
    half = DK // 2
    inv = ROPE_BASE ** (-jnp.arange(half, dtype=F32) / half)
    ang = pos.astype(F32)[:, None] * inv[None, :]
    cos = jnp.tile(jnp.cos(ang), (1, 2 * N_HEADS))
    sin = jnp.tile(jnp.concatenate([-jnp.sin(ang), jnp.sin(ang)], axis=1), (1, N_HEADS))
    tables = jnp.concatenate([cos, sin], axis=1)
    log_gamma = jnp.log(1.0 - 2.0 ** (-5.0 - jnp.arange(N_HEADS, dtype=F32)))
    la = jnp.repeat(log_gamma, DK)
    return decay_mixer("ret", cols, batch, seq, BLK_RET_QK, BLK_RET_V, BLK_RET_G, tables, False,
                       jnp.broadcast_to(la[:, None], (HK, DV)), la.reshape(1, HK), lp['ret_norm'].reshape(1, DV), s0)


def _t5_bucket(rel):
    n = jnp.maximum(rel, 0)
    max_exact = REL_BUCKETS // 2
    nf = jnp.maximum(n, 1).astype(F32)
    large = max_exact + (jnp.log(nf / max_exact) / math.log(REL_MAX_DIST / max_exact)
                         * (REL_BUCKETS - max_exact)).astype(jnp.int32)
    large = jnp.minimum(large, REL_BUCKETS - 1)
    return jnp.where(n < max_exact, n, large)


def _rel_bias_tile(rel_bias, rel):
    onehot = (_t5_bucket(rel)[..., None] == jnp.arange(REL_BUCKETS, dtype=jnp.int32)).astype(F32)
    b = jnp.einsum('qkb,bh->hqk', onehot, rel_bias.astype(F32), precision=HIGHEST)
    return jnp.where(rel[None] >= 0, b, NEG_BIG)


def _lambda(lam_ref, lam_init):
    s1 = jnp.sum(lam_ref[0:1, :] * lam_ref[1:2, :], axis=-1, keepdims=True)
    s2 = jnp.sum(lam_ref[2:3, :] * lam_ref[3:4, :], axis=-1, keepdims=True)
    return jnp.exp(s1) - jnp.exp(s2) + lam_init


def _diff_finish(o1, o2, lam, lam_init, g, sw):
    o = o1 - lam * o2
    y = o * lax.rsqrt(jnp.mean(o * o, axis=-1, keepdims=True) + EPS) * sw * (1.0 - lam_init)
    return y * (g * _sigmoid(g))


def _diffattn_kernel(lam_init, tq, tk, q_ref, k_ref, v_ref, g_ref, bias_ref, lam_ref, sw_ref,
                     o_ref, kb_scr, vt_scr):
    qi = pl.program_id(2)
    nh = q_ref.shape[1] // DV
    heads = range(nh)
    hs = lambda h: slice(h * DV, (h + 1) * DV)

    @pl.when(qi == 0)
    def _():
        for h in heads:
            kb_scr[h] = k_ref[:, hs(h)].astype(BF16)
            for j in range(vt_scr.shape[1]):
                vt_scr[h, j] = v_ref[j * tk:(j + 1) * tk, hs(h)].T.astype(BF16)

    first_map = _iota((tq, DV), 1) < DK
    Q = []
    for h in heads:
        q = q_ref[:, hs(h)] * (DK ** -0.5 * LOG2E)
        Q.append(jnp.concatenate([jnp.where(first_map, q, 0.0), jnp.where(first_map, 0.0, q)], axis=0).astype(BF16))

    def step(j, carry, bias_idx):
        m, l, acc = carry
        r0 = pl.multiple_of(j * tk, tk)
        st = [lax.dot_general(kb_scr[h, pl.ds(r0, tk), :], Q[h], (((1,), (1,)), ((), ())),
                              preferred_element_type=F32) for h in heads]
        if bias_idx is not None:
            st = [st[h] + jnp.concatenate([bias_ref[h, bias_idx]] * 2, axis=1) for h in heads]
        m_new = [jnp.maximum(m[h], jnp.max(st[h], axis=0, keepdims=True)) for h in heads]
        p = [jnp.exp2(st[h] - m_new[h]) for h in heads]
        alpha = [jnp.exp2(m[h] - m_new[h]) for h in heads]
        l = [alpha[h] * l[h] + jnp.sum(p[h], axis=0, keepdims=True) for h in heads]
        acc = [alpha[h] * acc[h] + jnp.dot(vt_scr[h, j], p[h].astype(BF16), preferred_element_type=F32)
               for h in heads]
        return m_new, l, acc

    carry = ([jnp.full((1, 2 * tq), NEG_BIG, F32)] * nh, [jnp.zeros((1, 2 * tq), F32)] * nh,
             [jnp.zeros((DV, 2 * tq), F32)] * nh)
    first_near = jnp.maximum(qi - 1, 0)
    carry = lax.fori_loop(0, first_near, lambda j, c: step(j, c, None), carry)
    _, l, acc = lax.fori_loop(first_near, qi + 1, lambda j, c: step(j, c, j - qi + 1), carry)

    lam = _lambda(lam_ref, lam_init)
    for h in heads:
        ot = acc[h][:, :tq] / l[h][:, :tq] - lam * (acc[h][:, tq:] / l[h][:, tq:])
        yt = ot * lax.rsqrt(jnp.mean(ot * ot, axis=0, keepdims=True) + EPS) * sw_ref[...] * (1.0 - lam_init)
        g = g_ref[:, hs(h)]
        o_ref[:, hs(h)] = (yt.T * (g * _sigmoid(g))).astype(o_ref.dtype)


def prompt_bias_tiles(rel_bias, seq):
    tq = tk = min(ATTN_BLOCK, seq)
    assert tk >= REL_MAX_DIST
    d = _iota((tk, tq), 1) - _iota((tk, tq), 0)
    far = rel_bias.astype(F32)[REL_BUCKETS - 1][:, None, None]
    return jnp.stack([_rel_bias_tile(rel_bias, d + tk) - far, _rel_bias_tile(rel_bias, d) - far], axis=1) * LOG2E


def diff_attn_prompt(cols, batch, seq, lp, bias_tiles, lam_init):
    tq = tk = min(ATTN_BLOCK, seq)
    nq = seq // tq
    lam4 = jnp.stack([lp['diff_lam_q1'], lp['diff_lam_k1'], lp['diff_lam_q2'], lp['diff_lam_k2']]).astype(F32)
    const = lambda a: pl.BlockSpec(a.shape, lambda b, hg, i: (0,) * a.ndim)
    sw = lp['diff_subln'].reshape(DV, 1)
    nh = ATTN_HEADS
    wg = nh * DV
    per = GROUP_W // wg
    return pl.pallas_call(
        functools.partial(_diffattn_kernel, lam_init, tq, tk),
        grid=(batch, N_HEADS // nh, nq),
        in_specs=[pl.BlockSpec((tq, wg), lambda b, hg, i: (b * nq + i, BLK_DIFF_Q * per + hg)),
                  pl.BlockSpec((seq, wg), lambda b, hg, i: (b, BLK_DIFF_K * per + hg)),
                  pl.BlockSpec((seq, wg), lambda b, hg, i: (b, BLK_DIFF_V * per + hg)),
                  pl.BlockSpec((tq, wg), lambda b, hg, i: (b * nq + i, BLK_DIFF_G * per + hg)),
                  pl.BlockSpec((nh, 2, tk, tq), lambda b, hg, i: (hg, 0, 0, 0)), const(lam4), const(sw)],
        out_specs=pl.BlockSpec((tq, wg), lambda b, hg, i: (b * nq + i, hg)),
        out_shape=jax.ShapeDtypeStruct((batch * seq, GROUP_W), BF16),
        scratch_shapes=[pltpu.VMEM((nh, seq, DV), BF16), pltpu.VMEM((nh, seq // tk, DV, tk), BF16)],
        compiler_params=_params(("arbitrary", "arbitrary", "arbitrary"), BIG_VMEM_LIMIT),
        name="diff_attn_prompt",
    )(cols, cols, cols, cols, bias_tiles, lam4, sw)


def _diffdec_kernel(lam_init, PP, ts, page, pt_ref, q_ref, kn_ref, vn_ref, g_ref, *rest):
    k_refs = rest[:PP]
    v_refs = rest[PP:2 * PP]
    bias_ref, biasn_ref, lam_ref, sw_ref, o_ref, q_scr = rest[2 * PP:2 * PP + 6]
    state = rest[2 * PP + 6:]
    m_scr, l_scr, acc_scr = state[0::3], state[1::3], state[2::3]
    NCH = len(m_scr)
    per = PP // NCH
    nkc = per * page
    step_id = pl.program_id(1)
    W = N_HEADS * DV

    @pl.when(step_id == 0)
    def _():
        q = q_ref[...] * (DK ** -0.5)
        lane = _iota(q.shape, 1)
        pieces = []
        for h in range(N_HEADS):
            for mp in range(2):
                lo = h * DV + mp * DK
                pieces.append(jnp.where((lane >= lo) & (lane < lo + DK), q, 0.0))
        q_scr[...] = jnp.concatenate(pieces, axis=0).astype(BF16)
        for c in range(NCH):
            m_scr[c][...] = jnp.full(m_scr[c].shape, NEG_BIG, F32)
            l_scr[c][...] = jnp.zeros(l_scr[c].shape, F32)
            acc_scr[c][...] = jnp.zeros(acc_scr[c].shape, F32)

    def update(chains, kb, vb, bias):
        qb = q_scr[...]
        n = range(len(chains))
        s = [lax.dot_general(qb, kb[i], (((1,), (1,)), ((), ())), preferred_element_type=F32) + bias[i] for i in n]
        m_old = [m_scr[c][...] for c in chains]
        m_new = [jnp.maximum(m_old[i], jnp.max(s[i], axis=-1, keepdims=True)) for i in n]
        p = [jnp.exp(s[i] - m_new[i]) for i in n]
        alpha = [jnp.exp(m_old[i] - m_new[i]) for i in n]
        for i, c in enumerate(chains):
            l_scr[c][...] = alpha[i] * l_scr[c][...] + jnp.sum(p[i], axis=-1, keepdims=True)
            acc_scr[c][...] = alpha[i] * acc_scr[c][...] + jnp.dot(p[i].astype(BF16), vb[i], preferred_element_type=F32)
            m_scr[c][...] = m_new[i]

    def page_rows(ref):
        return jnp.concatenate([ref[pl.ds(h, page, stride=N_HEADS), :] for h in range(N_HEADS)], axis=1).astype(BF16)

    update(list(range(NCH)),
           [jnp.concatenate([page_rows(r) for r in k_refs[c * per:(c + 1) * per]], axis=0) for c in range(NCH)],
           [jnp.concatenate([page_rows(r) for r in v_refs[c * per:(c + 1) * per]], axis=0) for c in range(NCH)],
           [bias_ref[:, c * nkc:(c + 1) * nkc] for c in range(NCH)])

    @pl.when(step_id == pl.num_programs(1) - 1)
    def _():
        pad = jnp.zeros((page - ts, W), BF16)
        update([0], [jnp.concatenate([kn_ref[...].astype(BF16), pad], axis=0)],
               [jnp.concatenate([vn_ref[...].astype(BF16), pad], axis=0)], [biasn_ref[...]])
        m = m_scr[0][...]
        for c in range(1, NCH):
            m = jnp.maximum(m, m_scr[c][...])
        w = [jnp.exp(m_scr[c][...] - m) for c in range(NCH)]
        ln_all = sum(w[c] * l_scr[c][...] for c in range(NCH))
        acc_all = sum(w[c] * acc_scr[c][...] for c in range(NCH))
        lam = _lambda(lam_ref, lam_init)
        rows = 2 * ts
        for h in range(N_HEADS):
            a = acc_all[h * rows:(h + 1) * rows, h * DV:(h + 1) * DV]
            ln = ln_all[h * rows:(h + 1) * rows, :]
            o_ref[:, h * DV:(h + 1) * DV] = _diff_finish(
                a[:ts] / ln[:ts], a[ts:] / ln[ts:], lam, lam_init,
                g_ref[:, h * DV:(h + 1) * DV], sw_ref[...]).astype(o_ref.dtype)


def sample_bias_tiles(rel_bias, ts, n_pages, page):
    n_steps = n_pages // DECODE_PAGES
    past = n_pages * page
    nk = DECODE_PAGES * page
    assert nk >= REL_MAX_DIST
    t_of_row = jnp.tile(jnp.arange(ts, dtype=jnp.int32), N_HEADS * 2)
    h_of_row = jnp.repeat(jnp.arange(N_HEADS, dtype=jnp.int32), 2 * ts)
    head_sel = h_of_row[None, :, None] == jnp.arange(N_HEADS, dtype=jnp.int32)[:, None, None]

    def rows_bias(k_pos):
        rel = past + t_of_row[:, None] - k_pos[None, :]
        return jnp.sum(jnp.where(head_sel, _rel_bias_tile(rel_bias, rel), 0.0), axis=0)

    kk = jnp.arange(nk, dtype=jnp.int32)
    bias_steps = jnp.stack([rows_bias(kk), rows_bias((n_steps - 1) * nk + kk)])
    kn = jnp.arange(page, dtype=jnp.int32)
    bias_new = jnp.where(kn[None, :] < ts, rows_bias(past + kn), NEG_BIG)
    return bias_steps, bias_new


def diff_attn_sample(cols, batch, ts, cache_k, cache_v, layer, page_table, lp, bias_tiles, lam_init):
    page = cache_k.shape[2] // N_HEADS
    W = N_HEADS * DV
    n_pages = page_table.shape[1]
    PP = DECODE_PAGES
    n_steps = n_pages // PP
    nk = PP * page
    rows = N_HEADS * 2 * ts
    bias_steps, bias_new = bias_tiles
    lam4 = jnp.stack([lp['diff_lam_q1'], lp['diff_lam_k1'], lp['diff_lam_q2'], lp['diff_lam_k2']]).astype(F32)

    def page_spec(i):
        return pl.BlockSpec((None, None, page * N_HEADS, DV),
                            lambda b, s, pt: (layer, pt[b * n_pages + s * PP + i], 0, 0))

    col_spec = lambda blk: pl.BlockSpec((ts, GROUP_W), lambda b, s, pt: (b, blk))
    const = lambda a: pl.BlockSpec(a.shape, lambda b, s, pt: (0,) * a.ndim)
    grid_spec = pltpu.PrefetchScalarGridSpec(
        num_scalar_prefetch=1,
        grid=(batch, n_steps),
        in_specs=[col_spec(BLK_DIFF_Q), col_spec(BLK_DIFF_K), col_spec(BLK_DIFF_V), col_spec(BLK_DIFF_G)]
                 + [page_spec(i) for i in range(PP)] + [page_spec(i) for i in range(PP)]
                 + [pl.BlockSpec((None, rows, nk), lambda b, s, pt: (jnp.where(s == n_steps - 1, 1, 0), 0, 0)),
                    const(bias_new), const(lam4), pl.BlockSpec((1, DV), lambda b, s, pt: (0, 0))],
        out_specs=pl.BlockSpec((ts, GROUP_W), lambda b, s, pt: (b, 0)),
        scratch_shapes=[pltpu.VMEM((rows, W), BF16)]
                       + [pltpu.VMEM((rows, 1), F32), pltpu.VMEM((rows, 1), F32), pltpu.VMEM((rows, W), F32)] * DECODE_CHAINS)
    return pl.pallas_call(
        functools.partial(_diffdec_kernel, lam_init, PP, ts, page),
        grid_spec=grid_spec,
        out_shape=jax.ShapeDtypeStruct((batch * ts, GROUP_W), BF16),
        compiler_params=_params(("parallel", "arbitrary")),
        name="diff_attn_sample",
    )(page_table.reshape(-1), cols, cols, cols, cols, *([cache_k] * PP), *([cache_v] * PP),
      bias_steps, bias_new, lam4, lp['diff_subln'].reshape(1, DV))


RWKV_PAIRS = RWKV_HEADS // 2
SHIFT_PAD = 4 * GROUP_W


def _rwkv_kernel(C, tb, r_ref, k_ref, v_ref, misc_ref, g_ref, shift_ref, s0_ref,
                 mu_ref, w0_ref, ww2_ref, a0_ref, aw2_ref, kk_ref, ka_ref, rk_ref, nw_ref,
                 o_ref, sout_ref, S_scr, carry_scr):
    G = GROUP_W
    tblk = pl.program_id(1)

    @pl.when(tblk == 0)
    def _():
        S_scr[...] = s0_ref[...]
        carry_scr[...] = shift_ref[...]

    first_row = _iota((tb, 1), 0) == 0

    def token_shift(p, lo, width):
        prev = jnp.where(first_row, carry_scr[:, lo:lo + width], pltpu.roll(p, 1, axis=0))
        carry_scr[:, lo:lo + width] = p[tb - 1:tb, :]
        return p + mu_ref[:, lo:lo + width] * (prev - p)

    r = token_shift(r_ref[...], 0, G)
    k = token_shift(k_ref[...], G, G)
    v = token_shift(v_ref[...], 2 * G, G)
    lora = token_shift(misc_ref[:, :LANE], 3 * G, LANE)

    w_pre = w0_ref[...] + jnp.dot(jnp.tanh(lora).astype(BF16), ww2_ref[...], preferred_element_type=F32)
    lw = -jnp.exp(-_softplus(-w_pre) - 0.5)
    a = _sigmoid(a0_ref[...] + jnp.dot(lora.astype(BF16), aw2_ref[...], preferred_element_type=F32))

    kk = k * kk_ref[...]
    kk = kk / jnp.maximum(jnp.sqrt(_head_sums(kk * kk)), 1e-12)
    k2 = k * (1.0 + (a - 1.0) * ka_ref[...])
    bb = kk * a
    bonus = _head_sums(r * k2 * rk_ref[...]) * v

    row = _iota((tb, tb), 0)
    col = _iota((tb, tb), 1)
    same = (row // C) == (col // C)
    cum = _mm_split(jnp.where(same & (col <= row), 1.0, 0.0).astype(BF16), lw)
    tot = _mm_split(jnp.where(same, 1.0, 0.0).astype(BF16), lw)
    inv = jnp.exp(-cum)
    rest = jnp.exp(tot - cum)
    rt = r * jnp.exp(cum)
    kt = kk * jnp.exp(cum - lw)
    kd = k2 * inv
    bd = bb * inv
    kh = k2 * rest
    bh = bb * rest

    C2 = 2 * C
    nc = tb // C
    lane0 = _iota((C, LANE), 1) < RWKV_N
    rr = _iota((C2, C2), 0)
    cc = _iota((C2, C2), 1)
    strict = (rr % C) > (cc % C)
    incl = (rr % C) >= (cc % C)
    eye = jnp.where(rr == cc, 1.0, 0.0)
    eye_lane = _iota((LANE, LANE), 0) == _iota((LANE, LANE), 1)
    n_double = max(int(math.log2(C)) - 1, 0)
    probs = [(c, p) for c in range(nc) for p in range(RWKV_PAIRS)]

    def split(x, c, p):
        xs = x[c * C:(c + 1) * C, p * LANE:(p + 1) * LANE]
        return jnp.concatenate([jnp.where(lane0, xs, 0.0), jnp.where(lane0, 0.0, xs)], axis=0).astype(BF16)

    Lk = [split(kt, c, p) for c, p in probs]
    Lr = [split(rt, c, p) for c, p in probs]
    Rk = [split(kd, c, p) for c, p in probs]
    Rb = [split(bd, c, p) for c, p in probs]
    Vb = [split(v, c, p) for c, p in probs]
    Kh = [split(kh, c, p) for c, p in probs]
    Bh = [split(bh, c, p) for c, p in probs]
    if C2 % LANE == 0:
        G = [_mm_nt(jnp.concatenate([x, y], axis=0), jnp.concatenate([z, u], axis=0))
             for x, y, z, u in zip(Lk, Lr, Rk, Rb)]
        g_kk, g_kb = [g[:C2, :C2] for g in G], [g[:C2, C2:] for g in G]
        g_rk, g_rb = [g[C2:, :C2] for g in G], [g[C2:, C2:] for g in G]
    else:
        g_kk, g_kb = [_mm_nt(x, y) for x, y in zip(Lk, Rk)], [_mm_nt(x, y) for x, y in zip(Lk, Rb)]
        g_rk, g_rb = [_mm_nt(x, y) for x, y in zip(Lr, Rk)], [_mm_nt(x, y) for x, y in zip(Lr, Rb)]
    A_kr = [jnp.concatenate([jnp.where(strict, x, 0.0), jnp.where(incl, y, 0.0)], axis=0).astype(BF16)
            for x, y in zip(g_kk, g_rk)]
    A_rb = [jnp.where(incl, x, 0.0).astype(BF16) for x in g_rb]
    P = [jnp.where(strict, -x, 0.0) for x in g_kb]
    T = [eye + x for x in P]
    if n_double > 0:
        P = [_mm(x, x) for x in P]
    for it in range(n_double):
        if it < n_double - 1:
            TP = [_mm(jnp.concatenate([t, x], axis=0), x) for t, x in zip(T, P)]
            T = [t + y[:C2] for t, y in zip(T, TP)]
            P = [y[C2:] for y in TP]
        else:
            T = [t + _mm(t, x) for t, x in zip(T, P)]
    T = [t.astype(BF16) for t in T]
    AV = [_mm(x, y) for x, y in zip(A_kr, Vb)]
    TW = [_mm(t, jnp.concatenate([x, y[:C2].astype(BF16)], axis=1)).astype(BF16)
          for t, x, y in zip(T, Lk, AV)]
    AW = [_mm(x, y) for x, y in zip(A_rb, TW)]
    Wr = [x.astype(F32) - y[:, :LANE] for x, y in zip(Lr, AW)]
    O0 = [x[C2:] - y[:, LANE:] for x, y in zip(AV, AW)]
    BW = [_mm_tn(x, y) for x, y in zip(Bh, TW)]
    D0 = [_mm_tn(x, y) - z[:, LANE:] for x, y, z in zip(Kh, Vb, BW)]
    WN = [jnp.concatenate([x, y[:, :LANE]], axis=0).astype(BF16) for x, y in zip(Wr, BW)]
    gcol = [jnp.exp(jnp.sum(jnp.where(eye_lane, tot[c * C:c * C + 1, p * LANE:(p + 1) * LANE], 0.0),
                            axis=1, keepdims=True)) for c, p in probs]

    S = [S_scr[p] for p in range(RWKV_PAIRS)]
    o_rows = []
    for c in range(nc):
        o_lanes = []
        for p in range(RWKV_PAIRS):
            i = c * RWKV_PAIRS + p
            Rm = jnp.dot(WN[i], S[p].astype(BF16), preferred_element_type=F32)
            O = Rm[:C2] + O0[i]
            o_lanes.append(O[:C] + O[C:])
            S[p] = S[p] * gcol[i] - Rm[C2:] + D0[i]
        o_rows.append(jnp.concatenate(o_lanes, axis=1))
    for p in range(RWKV_PAIRS):
        S_scr[p] = S[p]

    o = jnp.concatenate(o_rows, axis=0)
    ms = _head_sums(o * o) * (1.0 / RWKV_N)
    y = o * lax.rsqrt(ms + EPS) * nw_ref[...] + bonus
    g = g_ref[...]
    o_ref[...] = (y * (g * _sigmoid(g))).astype(o_ref.dtype)

    @pl.when(tblk == pl.num_programs(1) - 1)
    def _():
        sout_ref[...] = S_scr[...]


def rwkv_mixer(cols, batch, seq, lp, s0_pairs, shift_pad):
    C = min(64, seq)
    tb = min(SEQ_BLOCK, seq)
    nt = seq // tb
    G = GROUP_W
    mu = lp['rwkv_mu']
    mu_pad = jnp.concatenate([mu, jnp.zeros((SHIFT_PAD - mu.shape[0],), F32)]).reshape(1, SHIFT_PAD)
    ww2 = jnp.zeros((LANE, G), F32).at[:RWKV_LORA].set(lp['rwkv_w_w2']).astype(BF16)
    aw2 = jnp.zeros((LANE, G), F32).at[RWKV_LORA:].set(lp['rwkv_a_w2']).astype(BF16)
    vec = lambda a: a.reshape(1, G).astype(F32)
    params = [mu_pad, vec(lp['rwkv_w0']), ww2, vec(lp['rwkv_a0']), aw2, vec(lp['rwkv_k_k']), vec(lp['rwkv_k_a']),
              vec(lp['rwkv_r_k']), jnp.tile(lp['rwkv_norm'], RWKV_HEADS).reshape(1, G)]
    row_map = lambda bl: (lambda b, t: (b * nt + t, bl))
    full = lambda a: pl.BlockSpec(a.shape, lambda b, t: (0,) * a.ndim)
    return pl.pallas_call(
        functools.partial(_rwkv_kernel, C, tb),
        grid=(batch, nt),
        in_specs=[pl.BlockSpec((tb, G), row_map(BLK_RWKV_R)), pl.BlockSpec((tb, G), row_map(BLK_RWKV_K)),
                  pl.BlockSpec((tb, G), row_map(BLK_RWKV_V)), pl.BlockSpec((tb, G), row_map(BLK_MISC)),
                  pl.BlockSpec((tb, G), row_map(BLK_RWKV_G)),
                  pl.BlockSpec((None, 1, SHIFT_PAD), lambda b, t: (b, 0, 0)),
                  pl.BlockSpec((None, RWKV_PAIRS, LANE, LANE), lambda b, t: (b, 0, 0, 0))]
                 + [full(a) for a in params],
        out_specs=[pl.BlockSpec((tb, G), lambda b, t: (b * nt + t, 0)),
                   pl.BlockSpec((None, RWKV_PAIRS, LANE, LANE), lambda b, t: (b, 0, 0, 0))],
        out_shape=[jax.ShapeDtypeStruct((batch * seq, G), BF16),
                   jax.ShapeDtypeStruct((batch, RWKV_PAIRS, LANE, LANE), F32)],
        scratch_shapes=[pltpu.VMEM((RWKV_PAIRS, LANE, LANE), F32), pltpu.VMEM((1, SHIFT_PAD), F32)],
        compiler_params=_params(("parallel", "arbitrary")),
        name="rwkv7",
    )(cols, cols, cols, cols, cols, shift_pad, s0_pairs, *params)


def rwkv_state_to_pairs(s):
    b = s.shape[0]
    st = jnp.swapaxes(s, -1, -2).reshape(b, RWKV_PAIRS, 2, RWKV_N, RWKV_N)
    z = jnp.zeros_like(st[:, :, 0])
    top = jnp.concatenate([st[:, :, 0], z], axis=-1)
    bot = jnp.concatenate([z, st[:, :, 1]], axis=-1)
    return jnp.concatenate([top, bot], axis=-2)


def rwkv_pairs_to_state(sp):
    b = sp.shape[0]
    h0 = sp[:, :, :RWKV_N, :RWKV_N]
    h1 = sp[:, :, RWKV_N:, RWKV_N:]
    st = jnp.stack([h0, h1], axis=2).reshape(b, RWKV_HEADS, RWKV_N, RWKV_N)
    return jnp.swapaxes(st, -1, -2)


def _group_layer(x, scale, shift, gate, rows_per_mod, batch, seq, lp, w_pad, w_o_bf, final_w, final,
                 s_gla, s_ret, s_rwkv, s_shift, pos, diff_fn, tm_in, tm_out, k_all, v_all, layer):
    cols, k_all, v_all = in_proj(x, scale, shift, lp['norm_w'], w_pad, k_all, v_all, layer, rows_per_mod, tm_in, IN_TN)
    oa, n_gla = gla_mixer(cols, batch, seq, lp, s_gla)
    ob = diff_fn(cols)
    oc, n_ret = ret_mixer(cols, batch, seq, lp, s_ret, pos)
    od, n_rwkv = rwkv_mixer(cols, batch, seq, lp, s_rwkv, s_shift)
    y = out_proj(x, gate, oa, ob, oc, od, w_o_bf, layer, final_w, rows_per_mod, tm_out, final)
    last = cols.reshape(batch, seq, COLS_W)[:, -1]
    n_shift = jnp.concatenate([last[:, BLK_RWKV_R * GROUP_W:(BLK_RWKV_V + 1) * GROUP_W],
                               last[:, BLK_MISC * GROUP_W:BLK_MISC * GROUP_W + 2 * RWKV_LORA]], axis=-1)
    return y, k_all, v_all, (n_gla.reshape(batch, N_HEADS, DK, DV), n_ret.reshape(batch, N_HEADS, DK, DV),
                             rwkv_pairs_to_state(n_rwkv), n_shift)


def kernel(x_prompt, x_sample, cache_k, cache_v, state_gla, state_ret, state_rwkv, state_shift, page_table,
           c_prompt, c_sample, rel_bias, w_ada, b_ada, norm_w, w_in, w_o, gla_w_a2, gla_b_a, gla_norm,
           diff_lam_q1, diff_lam_k1, diff_lam_q2, diff_lam_k2, diff_subln, ret_norm, rwkv_mu, rwkv_w0, rwkv_w_w2,
           rwkv_a0, rwkv_a_w2, rwkv_k_k, rwkv_k_a, rwkv_r_k, rwkv_norm, final_norm):
    Bp, Sp, D = x_prompt.shape
    Bs, Ts, _ = x_sample.shape
    depth = w_in.shape[0]
    n_pool, page = cache_k.shape[1], cache_k.shape[2]
    past_len = page_table.shape[1] * page
    ck = cache_k.reshape(depth, n_pool, page * N_HEADS, DV)
    cv = cache_v.reshape(depth, n_pool, page * N_HEADS, DV)
    pos_p = jnp.arange(Sp, dtype=jnp.int32)
    pos_s = past_len + jnp.arange(Ts, dtype=jnp.int32)
    c_all = jnp.concatenate([c_prompt, c_sample], axis=0)
    xp = x_prompt.reshape(Bp * Sp, D)
    xs = x_sample.reshape(Bs * Ts, D)
    zeros_dec = jnp.zeros((Bp, N_HEADS * DK, DV), F32)
    zeros_rwkv = jnp.zeros((Bp, RWKV_PAIRS, LANE, LANE), F32)
    zeros_shift = jnp.zeros((Bp, 1, SHIFT_PAD), F32)
    bias_p = prompt_bias_tiles(rel_bias, Sp)
    bias_s = sample_bias_tiles(rel_bias, Ts, page_table.shape[1], page)
    kp = jnp.zeros((depth, Bp * Sp * N_HEADS, DV), F32)
    vp = jnp.zeros((depth, Bp * Sp * N_HEADS, DV), F32)
    ks = jnp.zeros((depth, Bs * Ts * N_HEADS, DV), F32)
    vs = jnp.zeros((depth, Bs * Ts * N_HEADS, DV), F32)
    w_pad_all = prep_in_weights(jnp.swapaxes(w_in, 1, 2))
    w_o_all = w_o.astype(BF16)
    new_p, new_s = [], []
    for l in range(depth):
        lp = dict(norm_w=norm_w[l], gla_w_a2=gla_w_a2[l], gla_b_a=gla_b_a[l], gla_norm=gla_norm[l],
                  diff_lam_q1=diff_lam_q1[l], diff_lam_k1=diff_lam_k1[l], diff_lam_q2=diff_lam_q2[l],
                  diff_lam_k2=diff_lam_k2[l], diff_subln=diff_subln[l], ret_norm=ret_norm[l],
                  rwkv_mu=rwkv_mu[l], rwkv_w0=rwkv_w0[l], rwkv_w_w2=rwkv_w_w2[l], rwkv_a0=rwkv_a0[l],
                  rwkv_a_w2=rwkv_a_w2[l], rwkv_k_k=rwkv_k_k[l], rwkv_k_a=rwkv_k_a[l], rwkv_r_k=rwkv_r_k[l],
                  rwkv_norm=rwkv_norm[l])
        final = l == depth - 1
        lam_init = 0.8 - 0.6 * math.exp(-0.3 * l)
        mod = ada_mod(c_all, w_ada, b_ada, l)
        shift, scale, gate = mod[:, :D], mod[:, D:2 * D], mod[:, 2 * D:]
        w_pad, w_o_bf = w_pad_all, w_o_all

        per_batch = lambda a: a[:Bp].reshape(Bp, 1, D)
        xp, kp, vp, st_p = _group_layer(
            xp, per_batch(scale), per_batch(shift), per_batch(gate), Sp, Bp, Sp, lp, w_pad, w_o_bf, final_norm, final,
            zeros_dec, zeros_dec, zeros_rwkv, zeros_shift, pos_p,
            lambda cols: diff_attn_prompt(cols, Bp, Sp, lp, bias_p, lam_init), min(IN_TM, Sp), min(OUT_TM, Sp),
            kp, vp, l)

        per_row = lambda a: jnp.repeat(a[Bp:], Ts, axis=0)
        sh = state_shift[l]
        sh_pad = jnp.concatenate([sh, jnp.zeros((Bs, SHIFT_PAD - sh.shape[1]), F32)], axis=1).reshape(Bs, 1, SHIFT_PAD)
        xs, ks, vs, st_s = _group_layer(
            xs, per_row(scale), per_row(shift), per_row(gate), 1, Bs, Ts, lp, w_pad, w_o_bf, final_norm, final,
            state_gla[l].reshape(Bs, N_HEADS * DK, DV), state_ret[l].reshape(Bs, N_HEADS * DK, DV),
            rwkv_state_to_pairs(state_rwkv[l]), sh_pad, pos_s,
            lambda cols: diff_attn_sample(cols, Bs, Ts, ck, cv, l, page_table, lp, bias_s, lam_init),
            Bs * Ts, Bs * Ts, ks, vs, l)
        new_p.append(st_p)
        new_s.append(st_s)
    stack = lambda states, i: jnp.stack([s[i] for s in states])
    heads = lambda a, b, t: a.reshape(depth, b, t, N_HEADS, DV)
    return (xp.reshape(Bp, Sp, D), xs.reshape(Bs, Ts, D),
            heads(kp, Bp, Sp), heads(vp, Bp, Sp), heads(ks, Bs, Ts), heads(vs, Bs, Ts),
            stack(new_p, 0), stack(new_s, 0), stack(new_p, 1), stack(new_s, 1),
            stack(new_p, 2), stack(new_s, 2), stack(new_p, 3), stack(new_s, 3))
```

```python
import functools
import math

import jax
import jax.numpy as jnp
from jax import lax
from jax.experimental import pallas as pl
from jax.experimental.pallas import tpu as pltpu

F32 = jnp.float32
BF16 = jnp.bfloat16
HIGHEST = lax.Precision.HIGHEST

D_MODEL = 2048
GROUP_W = 512
N_HEADS = 4
DK = 64
DV = 128
GLA_RANK = 16
GLA_TAU = 16.0
RWKV_N = 64
RWKV_HEADS = 8
RWKV_LORA = 64
ROPE_BASE = 10000.0
REL_BUCKETS = 32
REL_MAX_DIST = 128
CHUNK = 16
GLA_DIAG = 8
EPS = 1e-6
NEG_BIG = -1e30
LOG2E = math.log2(math.e)

LANE = 128
VMEM_LIMIT = 48 * 1024 * 1024
ATTN_BLOCK = 512
ATTN_HEADS = 4
DECODE_PAGES = 32
DECODE_CHAINS = 4
SEQ_BLOCK = 256
IN_TM, IN_TN = 1024, 1280
BIG_VMEM_LIMIT = 56 * 1024 * 1024
NORM_ROWS = 128
OUT_TM = 512
ADA_TN = 1536

COLS_W = 15 * GROUP_W
BLK_GLA_QK, BLK_GLA_V, BLK_GLA_G = 0, 1, 2
BLK_DIFF_Q, BLK_DIFF_K, BLK_DIFF_V, BLK_DIFF_G = 3, 4, 5, 6
BLK_RET_QK, BLK_RET_V, BLK_RET_G = 7, 8, 9
BLK_RWKV_R, BLK_RWKV_K, BLK_RWKV_V, BLK_RWKV_G = 10, 11, 12, 13
BLK_MISC = 14


def _mm(a, b):
    return jnp.dot(a.astype(BF16), b.astype(BF16), preferred_element_type=F32)


def _mm_nt(a, b):
    return lax.dot_general(a.astype(BF16), b.astype(BF16), (((1,), (1,)), ((), ())), preferred_element_type=F32)


def _mm_tn(a, b):
    return lax.dot_general(a.astype(BF16), b.astype(BF16), (((0,), (0,)), ((), ())), preferred_element_type=F32)


def _mm_exact(a, b):
    return jnp.dot(a, b, precision=HIGHEST, preferred_element_type=F32)


def _mm_tn_exact(a, b):
    return lax.dot_general(a, b, (((0,), (0,)), ((), ())), precision=HIGHEST, preferred_element_type=F32)


def _split2(x):
    hi = x.astype(BF16)
    return hi, (x - hi.astype(F32)).astype(BF16)


def _mm_split(a, x):
    hi, lo = _split2(x)
    return jnp.dot(a, hi, preferred_element_type=F32) + jnp.dot(a, lo, preferred_element_type=F32)


def _head_sums(x):
    ind = jnp.where((_iota((LANE, LANE), 0) // RWKV_N) == (_iota((LANE, LANE), 1) // RWKV_N), 1.0, 0.0).astype(BF16)
    hi, lo = _split2(x)
    parts = []
    for i in range(x.shape[1] // LANE):
        sl = slice(i * LANE, (i + 1) * LANE)
        parts.append(jnp.dot(hi[:, sl], ind, preferred_element_type=F32) + jnp.dot(lo[:, sl], ind, preferred_element_type=F32))
    return jnp.concatenate(parts, axis=1)


def _sigmoid(x):
    return 1.0 / (1.0 + jnp.exp(-x))


def _softplus(x):
    return jnp.maximum(x, 0.0) + jnp.log1p(jnp.exp(-jnp.abs(x)))


def _iota(shape, dim):
    return lax.broadcasted_iota(jnp.int32, shape, dim)


def _params(sem, vmem_limit=VMEM_LIMIT):
    return pltpu.CompilerParams(dimension_semantics=sem, vmem_limit_bytes=vmem_limit)


def _ada_kernel(c_ref, w_ref, b_ref, o_ref):
    c = c_ref[...]
    o_ref[...] = _mm(c * _sigmoid(c), w_ref[...]) + b_ref[...]


def ada_mod(c, w_ada, b_ada, layer, tn=ADA_TN):
    rows, d = c.shape
    depth, _, n = w_ada.shape
    return pl.pallas_call(
        _ada_kernel,
        grid=(n // tn,),
        in_specs=[pl.BlockSpec((rows, d), lambda j: (0, 0)),
                  pl.BlockSpec((None, d, tn), lambda j: (layer, 0, j)),
                  pl.BlockSpec((None, 1, tn), lambda j: (layer, 0, j))],
        out_specs=pl.BlockSpec((rows, tn), lambda j: (0, j)),
        out_shape=jax.ShapeDtypeStruct((rows, n), F32),
        compiler_params=_params(("arbitrary",)),
        name="ada_mod",
    )(c, w_ada, b_ada.reshape(depth, 1, n))


def _inproj_kernel(k_tile, k_lo, v_tile, v_lo, x_ref, scale_ref, shift_ref, nw_ref, w_ref, kin_ref, vin_ref,
                   o_ref, k_ref, v_ref, h_ref):
    del kin_ref, vin_ref
    j = pl.program_id(1)

    tm = o_ref.shape[0]
    rows = min(NORM_ROWS, tm)
    per_row_mod = scale_ref.shape[0] == tm

    @pl.when(j == 0)
    def _():
        def norm_rows(i, carry):
            sl = pl.ds(pl.multiple_of(i * rows, rows), rows)
            x = x_ref[sl, :]
            y = x * lax.rsqrt(jnp.mean(x * x, axis=-1, keepdims=True) + EPS) * nw_ref[...]
            scale = scale_ref[sl, :] if per_row_mod else scale_ref[...]
            shift = shift_ref[sl, :] if per_row_mod else shift_ref[...]
            h_ref[sl, :] = (y * (1.0 + scale) + shift).astype(BF16)
            return carry

        lax.fori_loop(0, tm // rows, norm_rows, 0)

    o_ref[...] = lax.dot_general(h_ref[...], w_ref[...], (((1,), (1,)), ((), ())), preferred_element_type=F32)

    def heads_to_rows(dst_ref, lo):
        for h in range(N_HEADS):
            dst_ref[pl.ds(h, tm, stride=N_HEADS), :] = o_ref[:, lo + h * DV:lo + (h + 1) * DV]

    @pl.when(j == k_tile)
    def _():
        heads_to_rows(k_ref, k_lo)

    @pl.when(j == v_tile)
    def _():
        heads_to_rows(v_ref, v_lo)


def in_proj(x, scale, shift, norm_w, w_pad, k_all, v_all, layer, rows_per_mod, tm, tn):
    m, d = x.shape
    n = w_pad.shape[1]
    k_lo, v_lo = BLK_DIFF_K * GROUP_W, BLK_DIFF_V * GROUP_W
    k_tile, v_tile = k_lo // tn, v_lo // tn
    assert (k_lo + GROUP_W - 1) // tn == k_tile and (v_lo + GROUP_W - 1) // tn == v_tile
    if rows_per_mod == 1:
        mod_spec = pl.BlockSpec((tm, d), lambda i, j: (i, 0))
    else:
        per = rows_per_mod // tm
        mod_spec = pl.BlockSpec((None, 1, d), lambda i, j: (i // per, 0, 0))
    kv_spec = pl.BlockSpec((None, tm * N_HEADS, DV), lambda i, j: (layer, i, 0), pipeline_mode=pl.Buffered(1))
    return pl.pallas_call(
        functools.partial(_inproj_kernel, k_tile, k_lo - k_tile * tn, v_tile, v_lo - v_tile * tn),
        grid=(m // tm, n // tn),
        in_specs=[pl.BlockSpec((tm, d), lambda i, j: (i, 0)),
                  mod_spec, mod_spec,
                  pl.BlockSpec((1, d), lambda i, j: (0, 0)),
                  pl.BlockSpec((None, tn, d), lambda i, j: (layer, j, 0)),
                  pl.BlockSpec(memory_space=pl.ANY), pl.BlockSpec(memory_space=pl.ANY)],
        out_specs=[pl.BlockSpec((tm, tn), lambda i, j: (i, j)), kv_spec, kv_spec],
        out_shape=[jax.ShapeDtypeStruct((m, n), F32), jax.ShapeDtypeStruct(k_all.shape, F32),
                   jax.ShapeDtypeStruct(v_all.shape, F32)],
        input_output_aliases={5: 1, 6: 2},
        scratch_shapes=[pltpu.VMEM((tm, d), BF16)],
        compiler_params=_params(("parallel", "arbitrary"), BIG_VMEM_LIMIT),
        name="in_proj",
    )(x, scale, shift, norm_w.reshape(1, d), w_pad, k_all, v_all)


def _outproj_kernel(final, x_ref, gate_ref, oa_ref, ob_ref, oc_ref, od_ref, w_ref, fw_ref, y_ref):
    acc = jnp.dot(oa_ref[...], w_ref[0 * GROUP_W:1 * GROUP_W, :], preferred_element_type=F32)
    acc += jnp.dot(ob_ref[...], w_ref[1 * GROUP_W:2 * GROUP_W, :], preferred_element_type=F32)
    acc += jnp.dot(oc_ref[...], w_ref[2 * GROUP_W:3 * GROUP_W, :], preferred_element_type=F32)
    acc += jnp.dot(od_ref[...], w_ref[3 * GROUP_W:4 * GROUP_W, :], preferred_element_type=F32)
    y = x_ref[...] + gate_ref[...] * acc
    if final:
        y = y * lax.rsqrt(jnp.mean(y * y, axis=-1, keepdims=True) + EPS) * fw_ref[...]
    y_ref[...] = y


def out_proj(x, gate, oa, ob, oc, od, w_o, layer, final_w, rows_per_mod, tm, final):
    m, d = x.shape
    if rows_per_mod == 1:
        mod_spec = pl.BlockSpec((tm, d), lambda i: (i, 0))
    else:
        per = rows_per_mod // tm
        mod_spec = pl.BlockSpec((None, 1, d), lambda i: (i // per, 0, 0))
    o_spec = pl.BlockSpec((tm, GROUP_W), lambda i: (i, 0))
    return pl.pallas_call(
        functools.partial(_outproj_kernel, final),
        grid=(m // tm,),
        in_specs=[pl.BlockSpec((tm, d), lambda i: (i, 0)), mod_spec,
                  o_spec, o_spec, o_spec, o_spec,
                  pl.BlockSpec((None, d, d), lambda i: (layer, 0, 0)),
                  pl.BlockSpec((1, d), lambda i: (0, 0))],
        out_specs=pl.BlockSpec((tm, d), lambda i: (i, 0)),
        out_shape=jax.ShapeDtypeStruct((m, d), F32),
        compiler_params=_params(("parallel",)),
        name="out_proj",
    )(x, gate, oa, ob, oc, od, w_o, final_w.reshape(1, d))


def _head_rms_gate(o, g, nw):
    y = o * lax.rsqrt(jnp.mean(o * o, axis=-1, keepdims=True) + EPS) * nw
    return y * (g * _sigmoid(g))


def _rope128(x, cos, sin_signed):
    half = DK // 2
    up = pltpu.roll(x, LANE - half, axis=1)
    down = pltpu.roll(x, half, axis=1)
    first = (_iota(x.shape, 1) % DK) < half
    return x * cos + jnp.where(first, up, down) * sin_signed


def _decay_kernel(mode, C, tb, qk_ref, v_ref, g_ref, aux_ref, p1_ref, p2_ref, nw_ref, s0_ref,
                  o_ref, sout_ref, S_scr, k_scr, b_scr):
    HK = N_HEADS * DK
    tblk = pl.program_id(1)

    @pl.when(tblk == 0)
    def _():
        S_scr[...] = s0_ref[...]

    q = qk_ref[:, :HK]
    k = qk_ref[:, HK:]
    if mode == "gla":
        x = jnp.dot(aux_ref[:, LANE:2 * LANE].astype(BF16), p1_ref[...], preferred_element_type=F32) + p2_ref[...]
        la = -_softplus(-x) * (1.0 / GLA_TAU)
        q = q * (DK ** -0.5)
    else:
        cos = aux_ref[:, :HK]
        sin = aux_ref[:, HK:]
        q = jnp.concatenate([_rope128(q[:, i * LANE:(i + 1) * LANE], cos[:, i * LANE:(i + 1) * LANE],
                                      sin[:, i * LANE:(i + 1) * LANE]) for i in range(HK // LANE)], axis=1)
        k = jnp.concatenate([_rope128(k[:, i * LANE:(i + 1) * LANE], cos[:, i * LANE:(i + 1) * LANE],
                                      sin[:, i * LANE:(i + 1) * LANE]) for i in range(HK // LANE)], axis=1)
        k = k * (DK ** -0.5)

    nc = tb // C
    lane_head = _iota((C, HK), 1) // DK
    heads_rows = lambda x: jnp.concatenate([jnp.where(lane_head == h, x, 0.0) for h in range(N_HEADS)],
                                           axis=0).astype(BF16)
    diag_blocks = lambda d: jnp.concatenate([d[h * DK:(h + 1) * DK, h * DV:(h + 1) * DV] for h in range(N_HEADS)], axis=0)
    heads_lanes = lambda x: jnp.concatenate([x[h * C:(h + 1) * C, :] for h in range(N_HEADS)], axis=1)
    row = _iota((tb, tb), 0)
    col = _iota((tb, tb), 1)

    if mode == "gla":
        same = (row // C) == (col // C)
        b = _mm_split(jnp.where(same & (col <= row), 1.0, 0.0).astype(BF16), la)
        btot = _mm_split(jnp.where(same, 1.0, 0.0).astype(BF16), la)
        k_scr[...] = k
        b_scr[...] = b
        qe = q * jnp.exp(b)
        ke = k * jnp.exp(btot - b)
        sel = jnp.where(_iota((tb, nc * DV), 0) // C == _iota((tb, nc * DV), 1) // DV, 1.0, 0.0).astype(BF16)
        la_hi, la_lo = _split2(la)
        tn = (((0,), (0,)), ((), ()))
        gcols = jnp.exp(lax.dot_general(la_hi, sel, tn, preferred_element_type=F32)
                        + lax.dot_general(la_lo, sel, tn, preferred_element_type=F32))
        ind = jnp.where((_iota((HK, N_HEADS * DV), 0) // DK) == (_iota((HK, N_HEADS * DV), 1) // DV), 1.0, 0.0).astype(BF16)
        D = min(GLA_DIAG, C)
        halves = C // D
        trow = _iota((D, HK), 0)
        chunks = range(nc)
        X = []
        for c in chunks:
            xs = []
            for a in range(halves):
                r0 = c * C + a * D
                qd, bd = q[r0:r0 + D], b[r0:r0 + D]
                for s in range(D):
                    r = r0 + s
                    e = jnp.where(trow >= s, jnp.exp(jnp.minimum(bd - b_scr[r:r + 1, :], 0.0)), 0.0)
                    xs.append((qd * k_scr[r:r + 1, :] * e).astype(BF16))
            X.append(jnp.concatenate(xs, axis=0))
        R = [jnp.dot(x, ind, preferred_element_type=F32) for x in X]
        o_blocks = []
        for c in chunks:
            blocks = []
            for a in range(halves):
                r0 = c * C + a * D
                base = a * D * D
                o_d = R[c][base:base + D, :] * v_ref[r0:r0 + 1, :]
                for s in range(1, D):
                    o_d = o_d + R[c][base + s * D:base + (s + 1) * D, :] * v_ref[r0 + s:r0 + s + 1, :]
                blocks.append(o_d)
            o_blocks.append(blocks)
        if halves == 2:
            lane_head_d = _iota((D, HK), 1) // DK
            qa, ka = [], []
            for c in chunks:
                r0 = c * C
                anchor = b_scr[r0 + D - 1:r0 + D, :]
                qs = q[r0 + D:r0 + C] * jnp.exp(b[r0 + D:r0 + C] - anchor)
                qa.append(jnp.concatenate([jnp.where(lane_head_d == h, qs, 0.0) for h in range(N_HEADS)], axis=0))
                ka.append(k[r0:r0 + D] * jnp.exp(anchor - b[r0:r0 + D]))
            rows_q = nc * N_HEADS * D
            p_off = _mm_nt(jnp.concatenate(qa, axis=0), jnp.concatenate(ka, axis=0))
            own = (_iota((rows_q, nc * D), 0) // (N_HEADS * D)) == (_iota((rows_q, nc * D), 1) // D)
            v_first = jnp.concatenate([v_ref[c * C:c * C + D, :] for c in chunks], axis=0)
            o_off = _mm(jnp.where(own, p_off, 0.0), v_first)
            for c in chunks:
                base = c * N_HEADS * D
                o_blocks[c][1] = o_blocks[c][1] + jnp.concatenate(
                    [o_off[base + h * D:base + (h + 1) * D, h * DV:(h + 1) * DV] for h in range(N_HEADS)], axis=1)
        o_chunks = [jnp.concatenate(blk, axis=0) for blk in o_blocks]
        v_rows = lambda c: jnp.concatenate([v_ref[c * C:(c + 1) * C, h * DV:(h + 1) * DV] for h in range(N_HEADS)], axis=0)
        dS = [_mm_tn(heads_rows(ke[c * C:(c + 1) * C]), v_rows(c)) for c in chunks]
        L = [heads_rows(qe[c * C:(c + 1) * C]) for c in chunks]
        S = S_scr[...]
        for c in chunks:
            o_chunks[c] = o_chunks[c] + heads_lanes(jnp.dot(L[c], S.astype(BF16), preferred_element_type=F32))
            S = S * gcols[:, c * DV:(c + 1) * DV] + dS[c]
        S_scr[...] = S
        o = jnp.concatenate(o_chunks, axis=0)
    else:
        lg = p2_ref[...]
        tcol = _iota((tb, 1), 0).astype(F32)
        qe = q * jnp.exp((tcol + 1.0) * lg)
        ke = k * jnp.exp((tb - 1.0 - tcol) * lg)
        dist = (row - col).astype(F32)
        kb = k.astype(BF16)
        Q = heads_rows(q)
        o_heads = []
        for h in range(N_HEADS):
            s = lax.dot_general(Q[h * tb:(h + 1) * tb], kb, (((1,), (1,)), ((), ())), preferred_element_type=F32)
            dec = jnp.where(row >= col, jnp.exp(jnp.minimum(dist * lg[:, h * DK:h * DK + 1], 0.0)), 0.0)
            o_heads.append(_mm(s * dec, v_ref[:, h * DV:(h + 1) * DV]))
        S = S_scr[...]
        o = jnp.concatenate(o_heads, axis=1) + heads_lanes(jnp.dot(heads_rows(qe), S.astype(BF16),
                                                                   preferred_element_type=F32))
        S_scr[...] = S * jnp.exp(tb * p1_ref[...]) + diag_blocks(_mm_tn(ke, v_ref[...]))

    for h in range(N_HEADS):
        sl = slice(h * DV, (h + 1) * DV)
        o_ref[:, sl] = _head_rms_gate(o[:, sl], g_ref[:, sl], nw_ref[...]).astype(o_ref.dtype)

    @pl.when(tblk == pl.num_programs(1) - 1)
    def _():
        sout_ref[...] = S_scr[...]


def decay_mixer(mode, cols, batch, seq, blk_qk, blk_v, blk_g, aux, aux_is_cols, p1, p2, norm_w, s0):
    tb = min(SEQ_BLOCK, seq)
    C = min(CHUNK, seq) if mode == "gla" else tb
    nt = seq // tb
    HK = N_HEADS * DK
    row_map = lambda bl: (lambda b, t: (b * nt + t, bl))
    if aux_is_cols:
        aux_spec = pl.BlockSpec((tb, GROUP_W), row_map(BLK_MISC))
    else:
        aux_spec = pl.BlockSpec((tb, aux.shape[1]), lambda b, t: (t, 0))
    full = lambda a: pl.BlockSpec(a.shape, lambda b, t: (0,) * a.ndim)
    return pl.pallas_call(
        functools.partial(_decay_kernel, mode, C, tb),
        grid=(batch, nt),
        in_specs=[pl.BlockSpec((tb, GROUP_W), row_map(blk_qk)),
                  pl.BlockSpec((tb, GROUP_W), row_map(blk_v)),
                  pl.BlockSpec((tb, GROUP_W), row_map(blk_g)),
                  aux_spec, full(p1), full(p2), full(norm_w),
                  pl.BlockSpec((None, HK, DV), lambda b, t: (b, 0, 0))],
        out_specs=[pl.BlockSpec((tb, GROUP_W), lambda b, t: (b * nt + t, 0)),
                   pl.BlockSpec((None, HK, DV), lambda b, t: (b, 0, 0))],
        out_shape=[jax.ShapeDtypeStruct((batch * seq, GROUP_W), BF16),
                   jax.ShapeDtypeStruct((batch, HK, DV), F32)],
        scratch_shapes=[pltpu.VMEM((HK, DV), F32)] + [pltpu.VMEM((tb, HK), F32)] * 2,
        compiler_params=_params(("parallel", "arbitrary")),
        name="decay_" + mode,
    )(cols, cols, cols, aux, p1, p2, norm_w, s0)


def pad_in_cols(w):
    z = jnp.zeros(w.shape[:-1] + (COLS_W - 7312,), w.dtype)
    return jnp.concatenate([w[..., 0:1024], w[..., 1040:1552], w[..., 1552:5136], w[..., 5136:6672],
                            w[..., 6800:7312], w[..., 6672:6800], w[..., 1024:1040], z], axis=-1)


IN_W = 7312
ALPHA_LO = 1024
RWKV_LORA_LO, RWKV_G_LO = 6672, 6800


def _wprep_kernel(w_ref, o_ref):
    lanes = w_ref.shape[1]
    n_main = RWKV_LORA_LO - ALPHA_LO - GLA_RANK
    g_dst = BLK_RWKV_G * GROUP_W
    misc = BLK_MISC * GROUP_W
    o_ref[:ALPHA_LO, :] = w_ref[:ALPHA_LO, :].astype(BF16)
    o_ref[ALPHA_LO:ALPHA_LO + n_main, :] = w_ref[ALPHA_LO + GLA_RANK:RWKV_LORA_LO, :].astype(BF16)
    o_ref[g_dst:g_dst + GROUP_W, :] = w_ref[RWKV_G_LO:IN_W, :].astype(BF16)
    o_ref[misc:misc + LANE, :] = w_ref[RWKV_LORA_LO:RWKV_G_LO, :].astype(BF16)
    o_ref[misc + LANE:misc + LANE + GLA_RANK, :] = w_ref[ALPHA_LO:ALPHA_LO + GLA_RANK, :].astype(BF16)
    o_ref[misc + LANE + GLA_RANK:, :] = jnp.zeros((GROUP_W - LANE - GLA_RANK, lanes), BF16)


def prep_in_weights(w_in_t, lanes=256):
    depth, n, d = w_in_t.shape
    assert n == IN_W
    return pl.pallas_call(
        _wprep_kernel,
        grid=(depth, d // lanes),
        in_specs=[pl.BlockSpec((None, n, lanes), lambda l, i: (l, 0, i))],
        out_specs=pl.BlockSpec((None, COLS_W, lanes), lambda l, i: (l, 0, i)),
        out_shape=jax.ShapeDtypeStruct((depth, COLS_W, d), BF16),
        compiler_params=_params(("parallel", "parallel")),
        name="prep_in_weights",
    )(w_in_t)


def gla_mixer(cols, batch, seq, lp, s0):
    HK = N_HEADS * DK
    w_a2 = jnp.zeros((LANE, HK), F32).at[:GLA_RANK].set(lp['gla_w_a2']).astype(BF16)
    b_a = lp['gla_b_a'].reshape(1, HK)
    return decay_mixer("gla", cols, batch, seq, BLK_GLA_QK, BLK_GLA_V, BLK_GLA_G, cols, True,
                       w_a2, b_a, lp['gla_norm'].reshape(1, DV), s0)


def ret_mixer(cols, batch, seq, lp, s0, pos):
    HK = N_HEADS * DK
    half = DK // 2
    inv = ROPE_BASE ** (-jnp.arange(half, dtype=F32) / half)
    ang = pos.astype(F32)[:, None] * inv[None, :]
    cos = jnp.tile(jnp.cos(ang), (1, 2 * N_HEADS))
    sin = jnp.tile(jnp.concatenate([-jnp.sin(ang), jnp.sin(ang)], axis=1), (1, N_HEADS))
    tables = jnp.concatenate([cos, sin], axis=1)
    log_gamma = jnp.log(1.0 - 2.0 ** (-5.0 - jnp.arange(N_HEADS, dtype=F32)))
    la = jnp.repeat(log_gamma, DK)
    return decay_mixer("ret", cols, batch, seq, BLK_RET_QK, BLK_RET_V, BLK_RET_G, tables, False,
                       jnp.broadcast_to(la[:, None], (HK, DV)), la.reshape(1, HK), lp['ret_norm'].reshape(1, DV), s0)


def _t5_bucket(rel):
    n = jnp.maximum(rel, 0)
    max_exact = REL_BUCKETS // 2
    nf = jnp.maximum(n, 1).astype(F32)
    large = max_exact + (jnp.log(nf / max_exact) / math.log(REL_MAX_DIST / max_exact)
                         * (REL_BUCKETS - max_exact)).astype(jnp.int32)
    large = jnp.minimum(large, REL_BUCKETS - 1)
    return jnp.where(n < max_exact, n, large)


def _rel_bias_tile(rel_bias, rel):
    onehot = (_t5_bucket(rel)[..., None] == jnp.arange(REL_BUCKETS, dtype=jnp.int32)).astype(F32)
    b = jnp.einsum('qkb,bh->hqk', onehot, rel_bias.astype(F32), precision=HIGHEST)
    return jnp.where(rel[None] >= 0, b, NEG_BIG)


def _lambda(lam_ref, lam_init):
    s1 = jnp.sum(lam_ref[0:1, :] * lam_ref[1:2, :], axis=-1, keepdims=True)
    s2 = jnp.sum(lam_ref[2:3, :] * lam_ref[3:4, :], axis=-1, keepdims=True)
    return jnp.exp(s1) - jnp.exp(s2) + lam_init


def _diff_finish(o1, o2, lam, lam_init, g, sw):
    o = o1 - lam * o2
    y = o * lax.rsqrt(jnp.mean(o * o, axis=-1, keepdims=True) + EPS) * sw * (1.0 - lam_init)
    return y * (g * _sigmoid(g))


def _diffattn_kernel(lam_init, tq, tk, q_ref, k_ref, v_ref, g_ref, bias_ref, lam_ref, sw_ref,
                     o_ref, kb_scr, vt_scr):
    qi = pl.program_id(2)
    nh = q_ref.shape[1] // DV
    heads = range(nh)
    hs = lambda h: slice(h * DV, (h + 1) * DV)

    @pl.when(qi == 0)
    def _():
        for h in heads:
            kb_scr[h] = k_ref[:, hs(h)].astype(BF16)
            for j in range(vt_scr.shape[1]):
                vt_scr[h, j] = v_ref[j * tk:(j + 1) * tk, hs(h)].T.astype(BF16)

    first_map = _iota((tq, DV), 1) < DK
    Q = []
    for h in heads:
        q = q_ref[:, hs(h)] * (DK ** -0.5 * LOG2E)
        Q.append(jnp.concatenate([jnp.where(first_map, q, 0.0), jnp.where(first_map, 0.0, q)], axis=0).astype(BF16))

    def step(j, carry, bias_idx):
        m, l, acc = carry
        r0 = pl.multiple_of(j * tk, tk)
        st = [lax.dot_general(kb_scr[h, pl.ds(r0, tk), :], Q[h], (((1,), (1,)), ((), ())),
                              preferred_element_type=F32) for h in heads]
        if bias_idx is not None:
            st = [st[h] + jnp.concatenate([bias_ref[h, bias_idx]] * 2, axis=1) for h in heads]
        m_new = [jnp.maximum(m[h], jnp.max(st[h], axis=0, keepdims=True)) for h in heads]
        p = [jnp.exp2(st[h] - m_new[h]) for h in heads]
        alpha = [jnp.exp2(m[h] - m_new[h]) for h in heads]
        l = [alpha[h] * l[h] + jnp.sum(p[h], axis=0, keepdims=True) for h in heads]
        acc = [alpha[h] * acc[h] + jnp.dot(vt_scr[h, j], p[h].astype(BF16), preferred_element_type=F32)
               for h in heads]
        return m_new, l, acc

    carry = ([jnp.full((1, 2 * tq), NEG_BIG, F32)] * nh, [jnp.zeros((1, 2 * tq), F32)] * nh,
             [jnp.zeros((DV, 2 * tq), F32)] * nh)
    first_near = jnp.maximum(qi - 1, 0)
    carry = lax.fori_loop(0, first_near, lambda j, c: step(j, c, None), carry)
    _, l, acc = lax.fori_loop(first_near, qi + 1, lambda j, c: step(j, c, j - qi + 1), carry)

    lam = _lambda(lam_ref, lam_init)
    for h in heads:
        ot = acc[h][:, :tq] / l[h][:, :tq] - lam * (acc[h][:, tq:] / l[h][:, tq:])
        yt = ot * lax.rsqrt(jnp.mean(ot * ot, axis=0, keepdims=True) + EPS) * sw_ref[...] * (1.0 - lam_init)
        g = g_ref[:, hs(h)]
        o_ref[:, hs(h)] = (yt.T * (g * _sigmoid(g))).astype(o_ref.dtype)


def prompt_bias_tiles(rel_bias, seq):
    tq = tk = min(ATTN_BLOCK, seq)
    assert tk >= REL_MAX_DIST
    d = _iota((tk, tq), 1) - _iota((tk, tq), 0)
    far = rel_bias.astype(F32)[REL_BUCKETS - 1][:, None, None]
    return jnp.stack([_rel_bias_tile(rel_bias, d + tk) - far, _rel_bias_tile(rel_bias, d) - far], axis=1) * LOG2E


def diff_attn_prompt(cols, batch, seq, lp, bias_tiles, lam_init):
    tq = tk = min(ATTN_BLOCK, seq)
    nq = seq // tq
    lam4 = jnp.stack([lp['diff_lam_q1'], lp['diff_lam_k1'], lp['diff_lam_q2'], lp['diff_lam_k2']]).astype(F32)
    const = lambda a: pl.BlockSpec(a.shape, lambda b, hg, i: (0,) * a.ndim)
    sw = lp['diff_subln'].reshape(DV, 1)
    nh = ATTN_HEADS
    wg = nh * DV
    per = GROUP_W // wg
    return pl.pallas_call(
        functools.partial(_diffattn_kernel, lam_init, tq, tk),
        grid=(batch, N_HEADS // nh, nq),
        in_specs=[pl.BlockSpec((tq, wg), lambda b, hg, i: (b * nq + i, BLK_DIFF_Q * per + hg)),
                  pl.BlockSpec((seq, wg), lambda b, hg, i: (b, BLK_DIFF_K * per + hg)),
                  pl.BlockSpec((seq, wg), lambda b, hg, i: (b, BLK_DIFF_V * per + hg)),
                  pl.BlockSpec((tq, wg), lambda b, hg, i: (b * nq + i, BLK_DIFF_G * per + hg)),
                  pl.BlockSpec((nh, 2, tk, tq), lambda b, hg, i: (hg, 0, 0, 0)), const(lam4), const(sw)],
        out_specs=pl.BlockSpec((tq, wg), lambda b, hg, i: (b * nq + i, hg)),
        out_shape=jax.ShapeDtypeStruct((batch * seq, GROUP_W), BF16),
        scratch_shapes=[pltpu.VMEM((nh, seq, DV), BF16), pltpu.VMEM((nh, seq // tk, DV, tk), BF16)],
        compiler_params=_params(("arbitrary", "arbitrary", "arbitrary"), BIG_VMEM_LIMIT),
        name="diff_attn_prompt",
    )(cols, cols, cols, cols, bias_tiles, lam4, sw)


def _diffdec_kernel(lam_init, PP, ts, page, pt_ref, q_ref, kn_ref, vn_ref, g_ref, *rest):
    k_refs = rest[:PP]
    v_refs = rest[PP:2 * PP]
    bias_ref, biasn_ref, lam_ref, sw_ref, o_ref, q_scr = rest[2 * PP:2 * PP + 6]
    state = rest[2 * PP + 6:]
    m_scr, l_scr, acc_scr = state[0::3], state[1::3], state[2::3]
    NCH = len(m_scr)
    per = PP // NCH
    nkc = per * page
    step_id = pl.program_id(1)
    W = N_HEADS * DV

    @pl.when(step_id == 0)
    def _():
        q = q_ref[...] * (DK ** -0.5)
        lane = _iota(q.shape, 1)
        pieces = []
        for h in range(N_HEADS):
            for mp in range(2):
                lo = h * DV + mp * DK
                pieces.append(jnp.where((lane >= lo) & (lane < lo + DK), q, 0.0))
        q_scr[...] = jnp.concatenate(pieces, axis=0).astype(BF16)
        for c in range(NCH):
            m_scr[c][...] = jnp.full(m_scr[c].shape, NEG_BIG, F32)
            l_scr[c][...] = jnp.zeros(l_scr[c].shape, F32)
            acc_scr[c][...] = jnp.zeros(acc_scr[c].shape, F32)

    def update(chains, kb, vb, bias):
        qb = q_scr[...]
        n = range(len(chains))
        s = [lax.dot_general(qb, kb[i], (((1,), (1,)), ((), ())), preferred_element_type=F32) + bias[i] for i in n]
        m_old = [m_scr[c][...] for c in chains]
        m_new = [jnp.maximum(m_old[i], jnp.max(s[i], axis=-1, keepdims=True)) for i in n]
        p = [jnp.exp(s[i] - m_new[i]) for i in n]
        alpha = [jnp.exp(m_old[i] - m_new[i]) for i in n]
        for i, c in enumerate(chains):
            l_scr[c][...] = alpha[i] * l_scr[c][...] + jnp.sum(p[i], axis=-1, keepdims=True)
            acc_scr[c][...] = alpha[i] * acc_scr[c][...] + jnp.dot(p[i].astype(BF16), vb[i], preferred_element_type=F32)
            m_scr[c][...] = m_new[i]

    def page_rows(ref):
        return jnp.concatenate([ref[pl.ds(h, page, stride=N_HEADS), :] for h in range(N_HEADS)], axis=1).astype(BF16)

    update(list(range(NCH)),
           [jnp.concatenate([page_rows(r) for r in k_refs[c * per:(c + 1) * per]], axis=0) for c in range(NCH)],
           [jnp.concatenate([page_rows(r) for r in v_refs[c * per:(c + 1) * per]], axis=0) for c in range(NCH)],
           [bias_ref[:, c * nkc:(c + 1) * nkc] for c in range(NCH)])

    @pl.when(step_id == pl.num_programs(1) - 1)
    def _():
        pad = jnp.zeros((page - ts, W), BF16)
        update([0], [jnp.concatenate([kn_ref[...].astype(BF16), pad], axis=0)],
               [jnp.concatenate([vn_ref[...].astype(BF16), pad], axis=0)], [biasn_ref[...]])
        m = m_scr[0][...]
        for c in range(1, NCH):
            m = jnp.maximum(m, m_scr[c][...])
        w = [jnp.exp(m_scr[c][...] - m) for c in range(NCH)]
        ln_all = sum(w[c] * l_scr[c][...] for c in range(NCH))
        acc_all = sum(w[c] * acc_scr[c][...] for c in range(NCH))
        lam = _lambda(lam_ref, lam_init)
        rows = 2 * ts
        for h in range(N_HEADS):
            a = acc_all[h * rows:(h + 1) * rows, h * DV:(h + 1) * DV]
            ln = ln_all[h * rows:(h + 1) * rows, :]
            o_ref[:, h * DV:(h + 1) * DV] = _diff_finish(
                a[:ts] / ln[:ts], a[ts:] / ln[ts:], lam, lam_init,
                g_ref[:, h * DV:(h + 1) * DV], sw_ref[...]).astype(o_ref.dtype)


def sample_bias_tiles(rel_bias, ts, n_pages, page):
    n_steps = n_pages // DECODE_PAGES
    past = n_pages * page
    nk = DECODE_PAGES * page
    assert nk >= REL_MAX_DIST
    t_of_row = jnp.tile(jnp.arange(ts, dtype=jnp.int32), N_HEADS * 2)
    h_of_row = jnp.repeat(jnp.arange(N_HEADS, dtype=jnp.int32), 2 * ts)
    head_sel = h_of_row[None, :, None] == jnp.arange(N_HEADS, dtype=jnp.int32)[:, None, None]

    def rows_bias(k_pos):
        rel = past + t_of_row[:, None] - k_pos[None, :]
        return jnp.sum(jnp.where(head_sel, _rel_bias_tile(rel_bias, rel), 0.0), axis=0)

    kk = jnp.arange(nk, dtype=jnp.int32)
    bias_steps = jnp.stack([rows_bias(kk), rows_bias((n_steps - 1) * nk + kk)])
    kn = jnp.arange(page, dtype=jnp.int32)
    bias_new = jnp.where(kn[None, :] < ts, rows_bias(past + kn), NEG_BIG)
    return bias_steps, bias_new


def diff_attn_sample(cols, batch, ts, cache_k, cache_v, layer, page_table, lp, bias_tiles, lam_init):
    page = cache_k.shape[2] // N_HEADS
    W = N_HEADS * DV
    n_pages = page_table.shape[1]
    PP = DECODE_PAGES
    n_steps = n_pages // PP
    nk = PP * page
    rows = N_HEADS * 2 * ts
    bias_steps, bias_new = bias_tiles
    lam4 = jnp.stack([lp['diff_lam_q1'], lp['diff_lam_k1'], lp['diff_lam_q2'], lp['diff_lam_k2']]).astype(F32)

    def page_spec(i):
        return pl.BlockSpec((None, None, page * N_HEADS, DV),
                            lambda b, s, pt: (layer, pt[b * n_pages + s * PP + i], 0, 0))

    col_spec = lambda blk: pl.BlockSpec((ts, GROUP_W), lambda b, s, pt: (b, blk))
    const = lambda a: pl.BlockSpec(a.shape, lambda b, s, pt: (0,) * a.ndim)
    grid_spec = pltpu.PrefetchScalarGridSpec(
        num_scalar_prefetch=1,
        grid=(batch, n_steps),
        in_specs=[col_spec(BLK_DIFF_Q), col_spec(BLK_DIFF_K), col_spec(BLK_DIFF_V), col_spec(BLK_DIFF_G)]
                 + [page_spec(i) for i in range(PP)] + [page_spec(i) for i in range(PP)]
                 + [pl.BlockSpec((None, rows, nk), lambda b, s, pt: (jnp.where(s == n_steps - 1, 1, 0), 0, 0)),
                    const(bias_new), const(lam4), pl.BlockSpec((1, DV), lambda b, s, pt: (0, 0))],
        out_specs=pl.BlockSpec((ts, GROUP_W), lambda b, s, pt: (b, 0)),
        scratch_shapes=[pltpu.VMEM((rows, W), BF16)]
                       + [pltpu.VMEM((rows, 1), F32), pltpu.VMEM((rows, 1), F32), pltpu.VMEM((rows, W), F32)] * DECODE_CHAINS)
    return pl.pallas_call(
        functools.partial(_diffdec_kernel, lam_init, PP, ts, page),
        grid_spec=grid_spec,
        out_shape=jax.ShapeDtypeStruct((batch * ts, GROUP_W), BF16),
        compiler_params=_params(("parallel", "arbitrary")),
        name="diff_attn_sample",
    )(page_table.reshape(-1), cols, cols, cols, cols, *([cache_k] * PP), *([cache_v] * PP),
      bias_steps, bias_new, lam4, lp['diff_subln'].reshape(1, DV))


RWKV_PAIRS = RWKV_HEADS // 2
SHIFT_PAD = 4 * GROUP_W


def _rwkv_kernel(C, tb, r_ref, k_ref, v_ref, misc_ref, g_ref, shift_ref, s0_ref,
                 mu_ref, w0_ref, ww2_ref, a0_ref, aw2_ref, kk_ref, ka_ref, rk_ref, nw_ref,
                 o_ref, sout_ref, S_scr, carry_scr):
    G = GROUP_W
    tblk = pl.program_id(1)

    @pl.when(tblk == 0)
    def _():
        S_scr[...] = s0_ref[...]
        carry_scr[...] = shift_ref[...]

    first_row = _iota((tb, 1), 0) == 0

    def token_shift(p, lo, width):
        prev = jnp.where(first_row, carry_scr[:, lo:lo + width], pltpu.roll(p, 1, axis=0))
        carry_scr[:, lo:lo + width] = p[tb - 1:tb, :]
        return p + mu_ref[:, lo:lo + width] * (prev - p)

    r = token_shift(r_ref[...], 0, G)
    k = token_shift(k_ref[...], G, G)
    v = token_shift(v_ref[...], 2 * G, G)
    lora = token_shift(misc_ref[:, :LANE], 3 * G, LANE)

    w_pre = w0_ref[...] + jnp.dot(jnp.tanh(lora).astype(BF16), ww2_ref[...], preferred_element_type=F32)
    lw = -jnp.exp(-_softplus(-w_pre) - 0.5)
    a = _sigmoid(a0_ref[...] + jnp.dot(lora.astype(BF16), aw2_ref[...], preferred_element_type=F32))

    kk = k * kk_ref[...]
    kk = kk / jnp.maximum(jnp.sqrt(_head_sums(kk * kk)), 1e-12)
    k2 = k * (1.0 + (a - 1.0) * ka_ref[...])
    bb = kk * a
    bonus = _head_sums(r * k2 * rk_ref[...]) * v

    row = _iota((tb, tb), 0)
    col = _iota((tb, tb), 1)
    same = (row // C) == (col // C)
    cum = _mm_split(jnp.where(same & (col <= row), 1.0, 0.0).astype(BF16), lw)
    tot = _mm_split(jnp.where(same, 1.0, 0.0).astype(BF16), lw)
    inv = jnp.exp(-cum)
    rest = jnp.exp(tot - cum)
    rt = r * jnp.exp(cum)
    kt = kk * jnp.exp(cum - lw)
    kd = k2 * inv
    bd = bb * inv
    kh = k2 * rest
    bh = bb * rest

    C2 = 2 * C
    nc = tb // C
    lane0 = _iota((C, LANE), 1) < RWKV_N
    rr = _iota((C2, C2), 0)
    cc = _iota((C2, C2), 1)
    strict = (rr % C) > (cc % C)
    incl = (rr % C) >= (cc % C)
    eye = jnp.where(rr == cc, 1.0, 0.0)
    eye_lane = _iota((LANE, LANE), 0) == _iota((LANE, LANE), 1)
    n_double = max(int(math.log2(C)) - 1, 0)
    probs = [(c, p) for c in range(nc) for p in range(RWKV_PAIRS)]

    def split(x, c, p):
        xs = x[c * C:(c + 1) * C, p * LANE:(p + 1) * LANE]
        return jnp.concatenate([jnp.where(lane0, xs, 0.0), jnp.where(lane0, 0.0, xs)], axis=0).astype(BF16)

    Lk = [split(kt, c, p) for c, p in probs]
    Lr = [split(rt, c, p) for c, p in probs]
    Rk = [split(kd, c, p) for c, p in probs]
    Rb = [split(bd, c, p) for c, p in probs]
    Vb = [split(v, c, p) for c, p in probs]
    Kh = [split(kh, c, p) for c, p in probs]
    Bh = [split(bh, c, p) for c, p in probs]
    if C2 % LANE == 0:
        G = [_mm_nt(jnp.concatenate([x, y], axis=0), jnp.concatenate([z, u], axis=0))
             for x, y, z, u in zip(Lk, Lr, Rk, Rb)]
        g_kk, g_kb = [g[:C2, :C2] for g in G], [g[:C2, C2:] for g in G]
        g_rk, g_rb = [g[C2:, :C2] for g in G], [g[C2:, C2:] for g in G]
    else:
        g_kk, g_kb = [_mm_nt(x, y) for x, y in zip(Lk, Rk)], [_mm_nt(x, y) for x, y in zip(Lk, Rb)]
        g_rk, g_rb = [_mm_nt(x, y) for x, y in zip(Lr, Rk)], [_mm_nt(x, y) for x, y in zip(Lr, Rb)]
    A_kr = [jnp.concatenate([jnp.where(strict, x, 0.0), jnp.where(incl, y, 0.0)], axis=0).astype(BF16)
            for x, y in zip(g_kk, g_rk)]
    A_rb = [jnp.where(incl, x, 0.0).astype(BF16) for x in g_rb]
    P = [jnp.where(strict, -x, 0.0) for x in g_kb]
    T = [eye + x for x in P]
    if n_double > 0:
        P = [_mm(x, x) for x in P]
    for it in range(n_double):
        if it < n_double - 1:
            TP = [_mm(jnp.concatenate([t, x], axis=0), x) for t, x in zip(T, P)]
            T = [t + y[:C2] for t, y in zip(T, TP)]
            P = [y[C2:] for y in TP]
        else:
            T = [t + _mm(t, x) for t, x in zip(T, P)]
    T = [t.astype(BF16) for t in T]
    AV = [_mm(x, y) for x, y in zip(A_kr, Vb)]
    TW = [_mm(t, jnp.concatenate([x, y[:C2].astype(BF16)], axis=1)).astype(BF16)
          for t, x, y in zip(T, Lk, AV)]
    AW = [_mm(x, y) for x, y in zip(A_rb, TW)]
    Wr = [x.astype(F32) - y[:, :LANE] for x, y in zip(Lr, AW)]
    O0 = [x[C2:] - y[:, LANE:] for x, y in zip(AV, AW)]
    BW = [_mm_tn(x, y) for x, y in zip(Bh, TW)]
    D0 = [_mm_tn(x, y) - z[:, LANE:] for x, y, z in zip(Kh, Vb, BW)]
    WN = [jnp.concatenate([x, y[:, :LANE]], axis=0).astype(BF16) for x, y in zip(Wr, BW)]
    gcol = [jnp.exp(jnp.sum(jnp.where(eye_lane, tot[c * C:c * C + 1, p * LANE:(p + 1) * LANE], 0.0),
                            axis=1, keepdims=True)) for c, p in probs]

    S = [S_scr[p] for p in range(RWKV_PAIRS)]
    o_rows = []
    for c in range(nc):
        o_lanes = []
        for p in range(RWKV_PAIRS):
            i = c * RWKV_PAIRS + p
            Rm = jnp.dot(WN[i], S[p].astype(BF16), preferred_element_type=F32)
            O = Rm[:C2] + O0[i]
            o_lanes.append(O[:C] + O[C:])
            S[p] = S[p] * gcol[i] - Rm[C2:] + D0[i]
        o_rows.append(jnp.concatenate(o_lanes, axis=1))
    for p in range(RWKV_PAIRS):
        S_scr[p] = S[p]

    o = jnp.concatenate(o_rows, axis=0)
    ms = _head_sums(o * o) * (1.0 / RWKV_N)
    y = o * lax.rsqrt(ms + EPS) * nw_ref[...] + bonus
    g = g_ref[...]
    o_ref[...] = (y * (g * _sigmoid(g))).astype(o_ref.dtype)

    @pl.when(tblk == pl.num_programs(1) - 1)
    def _():
        sout_ref[...] = S_scr[...]


def rwkv_mixer(cols, batch, seq, lp, s0_pairs, shift_pad):
    C = min(64, seq)
    tb = min(SEQ_BLOCK, seq)
    nt = seq // tb
    G = GROUP_W
    mu = lp['rwkv_mu']
    mu_pad = jnp.concatenate([mu, jnp.zeros((SHIFT_PAD - mu.shape[0],), F32)]).reshape(1, SHIFT_PAD)
    ww2 = jnp.zeros((LANE, G), F32).at[:RWKV_LORA].set(lp['rwkv_w_w2']).astype(BF16)
    aw2 = jnp.zeros((LANE, G), F32).at[RWKV_LORA:].set(lp['rwkv_a_w2']).astype(BF16)
    vec = lambda a: a.reshape(1, G).astype(F32)
    params = [mu_pad, vec(lp['rwkv_w0']), ww2, vec(lp['rwkv_a0']), aw2, vec(lp['rwkv_k_k']), vec(lp['rwkv_k_a']),
              vec(lp['rwkv_r_k']), jnp.tile(lp['rwkv_norm'], RWKV_HEADS).reshape(1, G)]
    row_map = lambda bl: (lambda b, t: (b * nt + t, bl))
    full = lambda a: pl.BlockSpec(a.shape, lambda b, t: (0,) * a.ndim)
    return pl.pallas_call(
        functools.partial(_rwkv_kernel, C, tb),
        grid=(batch, nt),
        in_specs=[pl.BlockSpec((tb, G), row_map(BLK_RWKV_R)), pl.BlockSpec((tb, G), row_map(BLK_RWKV_K)),
                  pl.BlockSpec((tb, G), row_map(BLK_RWKV_V)), pl.BlockSpec((tb, G), row_map(BLK_MISC)),
                  pl.BlockSpec((tb, G), row_map(BLK_RWKV_G)),
                  pl.BlockSpec((None, 1, SHIFT_PAD), lambda b, t: (b, 0, 0)),
                  pl.BlockSpec((None, RWKV_PAIRS, LANE, LANE), lambda b, t: (b, 0, 0, 0))]
                 + [full(a) for a in params],
        out_specs=[pl.BlockSpec((tb, G), lambda b, t: (b * nt + t, 0)),
                   pl.BlockSpec((None, RWKV_PAIRS, LANE, LANE), lambda b, t: (b, 0, 0, 0))],
        out_shape=[jax.ShapeDtypeStruct((batch * seq, G), BF16),
                   jax.ShapeDtypeStruct((batch, RWKV_PAIRS, LANE, LANE), F32)],
        scratch_shapes=[pltpu.VMEM((RWKV_PAIRS, LANE, LANE), F32), pltpu.VMEM((1, SHIFT_PAD), F32)],
        compiler_params=_params(("parallel", "arbitrary")),
        name="rwkv7",
    )(cols, cols, cols, cols, cols, shift_pad, s0_pairs, *params)


def rwkv_state_to_pairs(s):
    b = s.shape[0]
    st = jnp.swapaxes(s, -1, -2).reshape(b, RWKV_PAIRS, 2, RWKV_N, RWKV_N)
    z = jnp.zeros_like(st[:, :, 0])
    top = jnp.concatenate([st[:, :, 0], z], axis=-1)
    bot = jnp.concatenate([z, st[:, :, 1]], axis=-1)
    return jnp.concatenate([top, bot], axis=-2)


def rwkv_pairs_to_state(sp):
    b = sp.shape[0]
    h0 = sp[:, :, :RWKV_N, :RWKV_N]
    h1 = sp[:, :, RWKV_N:, RWKV_N:]
    st = jnp.stack([h0, h1], axis=2).reshape(b, RWKV_HEADS, RWKV_N, RWKV_N)
    return jnp.swapaxes(st, -1, -2)


def _group_layer(x, scale, shift, gate, rows_per_mod, batch, seq, lp, w_pad, w_o_bf, final_w, final,
                 s_gla, s_ret, s_rwkv, s_shift, pos, diff_fn, tm_in, tm_out, k_all, v_all, layer):
    cols, k_all, v_all = in_proj(x, scale, shift, lp['norm_w'], w_pad, k_all, v_all, layer, rows_per_mod, tm_in, IN_TN)
    oa, n_gla = gla_mixer(cols, batch, seq, lp, s_gla)
    ob = diff_fn(cols)
    oc, n_ret = ret_mixer(cols, batch, seq, lp, s_ret, pos)
    od, n_rwkv = rwkv_mixer(cols, batch, seq, lp, s_rwkv, s_shift)
    y = out_proj(x, gate, oa, ob, oc, od, w_o_bf, layer, final_w, rows_per_mod, tm_out, final)
    last = cols.reshape(batch, seq, COLS_W)[:, -1]
    n_shift = jnp.concatenate([last[:, BLK_RWKV_R * GROUP_W:(BLK_RWKV_V + 1) * GROUP_W],
                               last[:, BLK_MISC * GROUP_W:BLK_MISC * GROUP_W + 2 * RWKV_LORA]], axis=-1)
    return y, k_all, v_all, (n_gla.reshape(batch, N_HEADS, DK, DV), n_ret.reshape(batch, N_HEADS, DK, DV),
                             rwkv_pairs_to_state(n_rwkv), n_shift)


def kernel(x_prompt, x_sample, cache_k, cache_v, state_gla, state_ret, state_rwkv, state_shift, page_table,
           c_prompt, c_sample, rel_bias, w_ada, b_ada, norm_w, w_in, w_o, gla_w_a2, gla_b_a, gla_norm,
           diff_lam_q1, diff_lam_k1, diff_lam_q2, diff_lam_k2, diff_subln, ret_norm, rwkv_mu, rwkv_w0, rwkv_w_w2,
           rwkv_a0, rwkv_a_w2, rwkv_k_k, rwkv_k_a, rwkv_r_k, rwkv_norm, final_norm):
    Bp, Sp, D = x_prompt.shape
    Bs, Ts, _ = x_sample.shape
    depth = w_in.shape[0]
    n_pool, page = cache_k.shape[1], cache_k.shape[2]
    past_len = page_table.shape[1] * page
    ck = cache_k.reshape(depth, n_pool, page * N_HEADS, DV)
    cv = cache_v.reshape(depth, n_pool, page * N_HEADS, DV)
    pos_p = jnp.arange(Sp, dtype=jnp.int32)
    pos_s = past_len + jnp.arange(Ts, dtype=jnp.int32)
    c_all = jnp.concatenate([c_prompt, c_sample], axis=0)
    xp = x_prompt.reshape(Bp * Sp, D)
    xs = x_sample.reshape(Bs * Ts, D)
    zeros_dec = jnp.zeros((Bp, N_HEADS * DK, DV), F32)
    zeros_rwkv = jnp.zeros((Bp, RWKV_PAIRS, LANE, LANE), F32)
    zeros_shift = jnp.zeros((Bp, 1, SHIFT_PAD), F32)
    bias_p = prompt_bias_tiles(rel_bias, Sp)
    bias_s = sample_bias_tiles(rel_bias, Ts, page_table.shape[1], page)
    kp = jnp.zeros((depth, Bp * Sp * N_HEADS, DV), F32)
    vp = jnp.zeros((depth, Bp * Sp * N_HEADS, DV), F32)
    ks = jnp.zeros((depth, Bs * Ts * N_HEADS, DV), F32)
    vs = jnp.zeros((depth, Bs * Ts * N_HEADS, DV), F32)
    w_pad_all = prep_in_weights(jnp.swapaxes(w_in, 1, 2))
    w_o_all = w_o.astype(BF16)
    new_p, new_s = [], []
    for l in range(depth):
        lp = dict(norm_w=norm_w[l], gla_w_a2=gla_w_a2[l], gla_b_a=gla_b_a[l], gla_norm=gla_norm[l],
                  diff_lam_q1=diff_lam_q1[l], diff_lam_k1=diff_lam_k1[l], diff_lam_q2=diff_lam_q2[l],
                  diff_lam_k2=diff_lam_k2[l], diff_subln=diff_subln[l], ret_norm=ret_norm[l],
                  rwkv_mu=rwkv_mu[l], rwkv_w0=rwkv_w0[l], rwkv_w_w2=rwkv_w_w2[l], rwkv_a0=rwkv_a0[l],
                  rwkv_a_w2=rwkv_a_w2[l], rwkv_k_k=rwkv_k_k[l], rwkv_k_a=rwkv_k_a[l], rwkv_r_k=rwkv_r_k[l],
                  rwkv_norm=rwkv_norm[l])
        final = l == depth - 1
        lam_init = 0.8 - 0.6 * math.exp(-0.3 * l)
        mod = ada_mod(c_all, w_ada, b_ada, l)
        shift, scale, gate = mod[:, :D], mod[:, D:2 * D], mod[:, 2 * D:]
        w_pad, w_o_bf = w_pad_all, w_o_all

        per_batch = lambda a: a[:Bp].reshape(Bp, 1, D)
        xp, kp, vp, st_p = _group_layer(
            xp, per_batch(scale), per_batch(shift), per_batch(gate), Sp, Bp, Sp, lp, w_pad, w_o_bf, final_norm, final,
            zeros_dec, zeros_dec, zeros_rwkv, zeros_shift, pos_p,
            lambda cols: diff_attn_prompt(cols, Bp, Sp, lp, bias_p, lam_init), min(IN_TM, Sp), min(OUT_TM, Sp),
            kp, vp, l)

        per_row = lambda a: jnp.repeat(a[Bp:], Ts, axis=0)
        sh = state_shift[l]
        sh_pad = jnp.concatenate([sh, jnp.zeros((Bs, SHIFT_PAD - sh.shape[1]), F32)], axis=1).reshape(Bs, 1, SHIFT_PAD)
        xs, ks, vs, st_s = _group_layer(
            xs, per_row(scale), per_row(shift), per_row(gate), 1, Bs, Ts, lp, w_pad, w_o_bf, final_norm, final,
            state_gla[l].reshape(Bs, N_HEADS * DK, DV), state_ret[l].reshape(Bs, N_HEADS * DK, DV),
            rwkv_state_to_pairs(state_rwkv[l]), sh_pad, pos_s,
            lambda cols: diff_attn_sample(cols, Bs, Ts, ck, cv, l, page_table, lp, bias_s, lam_init),
            Bs * Ts, Bs * Ts, ks, vs, l)
        new_p.append(st_p)
        new_s.append(st_s)
    stack = lambda states, i: jnp.stack([s[i] for s in states])
    heads = lambda a, b, t: a.reshape(depth, b, t, N_HEADS, DV)
    return (xp.reshape(Bp, Sp, D), xs.reshape(Bs, Ts, D),
            heads(kp, Bp, Sp), heads(vp, Bp, Sp), heads(ks, Bs, Ts), heads(vs, Bs, Ts),
            stack(new_p, 0), stack(new_s, 0), stack(new_p, 1), stack(new_s, 1),
            stack(new_p, 2), stack(new_s, 2), stack(new_p, 3), stack(new_s, 3))
```

```python
import functools
import math

import jax
import jax.numpy as jnp
from jax import lax
from jax.experimental import pallas as pl
from jax.experimental.pallas import tpu as pltpu

F32 = jnp.float32
BF16 = jnp.bfloat16
HIGHEST = lax.Precision.HIGHEST

D_MODEL = 2048
GROUP_W = 512
N_HEADS = 4
DK = 64
DV = 128
GLA_RANK = 16
GLA_TAU = 16.0
RWKV_N = 64
RWKV_HEADS = 8
RWKV_LORA = 64
ROPE_BASE = 10000.0
REL_BUCKETS = 32
REL_MAX_DIST = 128
CHUNK = 16
GLA_DIAG = 8
EPS = 1e-6
NEG_BIG = -1e30
LOG2E = math.log2(math.e)

LANE = 128
VMEM_LIMIT = 48 * 1024 * 1024
ATTN_BLOCK = 512
ATTN_HEADS = 4
DECODE_PAGES = 32
DECODE_CHAINS = 4
SEQ_BLOCK = 256
IN_TM, IN_TN = 1024, 1280
BIG_VMEM_LIMIT = 56 * 1024 * 1024
NORM_ROWS = 128
OUT_TM = 512
ADA_TN = 1536

COLS_W = 15 * GROUP_W
BLK_GLA_QK, BLK_GLA_V, BLK_GLA_G = 0, 1, 2
BLK_DIFF_Q, BLK_DIFF_K, BLK_DIFF_V, BLK_DIFF_G = 3, 4, 5, 6
BLK_RET_QK, BLK_RET_V, BLK_RET_G = 7, 8, 9
BLK_RWKV_R, BLK_RWKV_K, BLK_RWKV_V, BLK_RWKV_G = 10, 11, 12, 13
BLK_MISC = 14


def _mm(a, b):
    return jnp.dot(a.astype(BF16), b.astype(BF16), preferred_element_type=F32)


def _mm_nt(a, b):
    return lax.dot_general(a.astype(BF16), b.astype(BF16), (((1,), (1,)), ((), ())), preferred_element_type=F32)


def _mm_tn(a, b):
    return lax.dot_general(a.astype(BF16), b.astype(BF16), (((0,), (0,)), ((), ())), preferred_element_type=F32)


def _split2(x):
    hi = x.astype(BF16)
    return hi, (x - hi.astype(F32)).astype(BF16)


def _mm_split(a, x):
    hi, lo = _split2(x)
    return jnp.dot(a, hi, preferred_element_type=F32) + jnp.dot(a, lo, preferred_element_type=F32)


def _head_sums(x):
    ind = jnp.where((_iota((LANE, LANE), 0) // RWKV_N) == (_iota((LANE, LANE), 1) // RWKV_N), 1.0, 0.0).astype(BF16)
    hi, lo = _split2(x)
    parts = []
    for i in range(x.shape[1] // LANE):
        sl = slice(i * LANE, (i + 1) * LANE)
        parts.append(jnp.dot(hi[:, sl], ind, preferred_element_type=F32) + jnp.dot(lo[:, sl], ind, preferred_element_type=F32))
    return jnp.concatenate(parts, axis=1)


def _sigmoid(x):
    return 1.0 / (1.0 + jnp.exp(-x))


def _softplus(x):
    return jnp.maximum(x, 0.0) + jnp.log1p(jnp.exp(-jnp.abs(x)))


def _iota(shape, dim):
    return lax.broadcasted_iota(jnp.int32, shape, dim)


def _params(sem, vmem_limit=VMEM_LIMIT):
    return pltpu.CompilerParams(dimension_semantics=sem, vmem_limit_bytes=vmem_limit)


def _ada_kernel(c_ref, w_ref, b_ref, o_ref):
    c = c_ref[...]
    o_ref[...] = _mm(c * _sigmoid(c), w_ref[...]) + b_ref[...]


def ada_mod(c, w_ada, b_ada, layer, tn=ADA_TN):
    rows, d = c.shape
    depth, _, n = w_ada.shape
    return pl.pallas_call(
        _ada_kernel,
        grid=(n // tn,),
        in_specs=[pl.BlockSpec((rows, d), lambda j: (0, 0)),
                  pl.BlockSpec((None, d, tn), lambda j: (layer, 0, j)),
                  pl.BlockSpec((None, 1, tn), lambda j: (layer, 0, j))],
        out_specs=pl.BlockSpec((rows, tn), lambda j: (0, j)),
        out_shape=jax.ShapeDtypeStruct((rows, n), F32),
        compiler_params=_params(("arbitrary",)),
        name="ada_mod",
    )(c, w_ada, b_ada.reshape(depth, 1, n))


def _inproj_kernel(k_tile, k_lo, v_tile, v_lo, x_ref, scale_ref, shift_ref, nw_ref, w_ref, kin_ref, vin_ref,
                   o_ref, k_ref, v_ref, h_ref):
    del kin_ref, vin_ref
    j = pl.program_id(1)

    tm = o_ref.shape[0]
    rows = min(NORM_ROWS, tm)
    per_row_mod = scale_ref.shape[0] == tm

    @pl.when(j == 0)
    def _():
        gain = None if per_row_mod else nw_ref[...] * (1.0 + scale_ref[...])

        def norm_rows(i, carry):
            sl = pl.ds(pl.multiple_of(i * rows, rows), rows)
            x = x_ref[sl, :]
            y = x * lax.rsqrt(jnp.mean(x * x, axis=-1, keepdims=True) + EPS)
            if per_row_mod:
                h = y * nw_ref[...] * (1.0 + scale_ref[sl, :]) + shift_ref[sl, :]
            else:
                h = y * gain + shift_ref[...]
            h_ref[sl, :] = h.astype(BF16)
            return carry

        lax.fori_loop(0, tm // rows, norm_rows, 0)

    o_ref[...] = lax.dot_general(h_ref[...], w_ref[...], (((1,), (1,)), ((), ())), preferred_element_type=F32)

    def heads_to_rows(dst_ref, lo):
        for h in range(N_HEADS):
            dst_ref[pl.ds(h, tm, stride=N_HEADS), :] = o_ref[:, lo + h * DV:lo + (h + 1) * DV]

    @pl.when(j == k_tile)
    def _():
        heads_to_rows(k_ref, k_lo)

    @pl.when(j == v_tile)
    def _():
        heads_to_rows(v_ref, v_lo)


def in_proj(x, scale, shift, norm_w, w_pad, k_all, v_all, layer, rows_per_mod, tm, tn):
    m, d = x.shape
    n = w_pad.shape[1]
    k_lo, v_lo = BLK_DIFF_K * GROUP_W, BLK_DIFF_V * GROUP_W
    k_tile, v_tile = k_lo // tn, v_lo // tn
    assert (k_lo + GROUP_W - 1) // tn == k_tile and (v_lo + GROUP_W - 1) // tn == v_tile
    if rows_per_mod == 1:
        mod_spec = pl.BlockSpec((tm, d), lambda i, j: (i, 0))
    else:
        per = rows_per_mod // tm
        mod_spec = pl.BlockSpec((None, 1, d), lambda i, j: (i // per, 0, 0))
    kv_spec = pl.BlockSpec((None, tm * N_HEADS, DV), lambda i, j: (layer, i, 0), pipeline_mode=pl.Buffered(1))
    return pl.pallas_call(
        functools.partial(_inproj_kernel, k_tile, k_lo - k_tile * tn, v_tile, v_lo - v_tile * tn),
        grid=(m // tm, n // tn),
        in_specs=[pl.BlockSpec((tm, d), lambda i, j: (i, 0)),
                  mod_spec, mod_spec,
                  pl.BlockSpec((1, d), lambda i, j: (0, 0)),
                  pl.BlockSpec((None, tn, d), lambda i, j: (layer, j, 0)),
                  pl.BlockSpec(memory_space=pl.ANY), pl.BlockSpec(memory_space=pl.ANY)],
        out_specs=[pl.BlockSpec((tm, tn), lambda i, j: (i, j)), kv_spec, kv_spec],
        out_shape=[jax.ShapeDtypeStruct((m, n), F32), jax.ShapeDtypeStruct(k_all.shape, F32),
                   jax.ShapeDtypeStruct(v_all.shape, F32)],
        input_output_aliases={5: 1, 6: 2},
        scratch_shapes=[pltpu.VMEM((tm, d), BF16)],
        compiler_params=_params(("parallel", "arbitrary"), BIG_VMEM_LIMIT),
        name="in_proj",
    )(x, scale, shift, norm_w.reshape(1, d), w_pad, k_all, v_all)


def _outproj_kernel(final, x_ref, gate_ref, oa_ref, ob_ref, oc_ref, od_ref, w_ref, fw_ref, y_ref):
    acc = jnp.dot(oa_ref[...], w_ref[0 * GROUP_W:1 * GROUP_W, :], preferred_element_type=F32)
    acc += jnp.dot(ob_ref[...], w_ref[1 * GROUP_W:2 * GROUP_W, :], preferred_element_type=F32)
    acc += jnp.dot(oc_ref[...], w_ref[2 * GROUP_W:3 * GROUP_W, :], preferred_element_type=F32)
    acc += jnp.dot(od_ref[...], w_ref[3 * GROUP_W:4 * GROUP_W, :], preferred_element_type=F32)
    y = x_ref[...] + gate_ref[...] * acc
    if final:
        y = y * lax.rsqrt(jnp.mean(y * y, axis=-1, keepdims=True) + EPS) * fw_ref[...]
    y_ref[...] = y


def out_proj(x, gate, oa, ob, oc, od, w_o, layer, final_w, rows_per_mod, tm, final):
    m, d = x.shape
    if rows_per_mod == 1:
        mod_spec = pl.BlockSpec((tm, d), lambda i: (i, 0))
    else:
        per = rows_per_mod // tm
        mod_spec = pl.BlockSpec((None, 1, d), lambda i: (i // per, 0, 0))
    o_spec = pl.BlockSpec((tm, GROUP_W), lambda i: (i, 0))
    return pl.pallas_call(
        functools.partial(_outproj_kernel, final),
        grid=(m // tm,),
        in_specs=[pl.BlockSpec((tm, d), lambda i: (i, 0)), mod_spec,
                  o_spec, o_spec, o_spec, o_spec,
                  pl.BlockSpec((None, d, d), lambda i: (layer, 0, 0)),
                  pl.BlockSpec((1, d), lambda i: (0, 0))],
        out_specs=pl.BlockSpec((tm, d), lambda i: (i, 0)),
        out_shape=jax.ShapeDtypeStruct((m, d), F32),
        compiler_params=_params(("parallel",)),
        name="out_proj",
    )(x, gate, oa, ob, oc, od, w_o, final_w.reshape(1, d))


def _head_rms_gate(o, g, nw):
    y = o * lax.rsqrt(jnp.mean(o * o, axis=-1, keepdims=True) + EPS) * nw
    return y * (g * _sigmoid(g))


def _rope128(x, cos, sin_signed):
    half = DK // 2
    up = pltpu.roll(x, LANE - half, axis=1)
    down = pltpu.roll(x, half, axis=1)
    first = (_iota(x.shape, 1) % DK) < half
    return x * cos + jnp.where(first, up, down) * sin_signed


def _decay_kernel(mode, C, tb, qk_ref, v_ref, g_ref, aux_ref, p1_ref, p2_ref, nw_ref, s0_ref,
                  o_ref, sout_ref, S_scr, k_scr, b_scr):
    HK = N_HEADS * DK
    tblk = pl.program_id(1)

    @pl.when(tblk == 0)
    def _():
        S_scr[...] = s0_ref[...]

    q = qk_ref[:, :HK]
    k = qk_ref[:, HK:]
    if mode == "gla":
        x = jnp.dot(aux_ref[:, LANE:2 * LANE].astype(BF16), p1_ref[...], preferred_element_type=F32) + p2_ref[...]
        la = -_softplus(-x) * (1.0 / GLA_TAU)
        q = q * (DK ** -0.5)
    else:
        cos = aux_ref[:, :HK]
        sin = aux_ref[:, HK:]
        q = jnp.concatenate([_rope128(q[:, i * LANE:(i + 1) * LANE], cos[:, i * LANE:(i + 1) * LANE],
                                      sin[:, i * LANE:(i + 1) * LANE]) for i in range(HK // LANE)], axis=1)
        k = jnp.concatenate([_rope128(k[:, i * LANE:(i + 1) * LANE], cos[:, i * LANE:(i + 1) * LANE],
                                      sin[:, i * LANE:(i + 1) * LANE]) for i in range(HK // LANE)], axis=1)
        k = k * (DK ** -0.5)

    nc = tb // C
    lane_head = _iota((C, HK), 1) // DK
    heads_rows = lambda x: jnp.concatenate([jnp.where(lane_head == h, x, 0.0) for h in range(N_HEADS)],
                                           axis=0).astype(BF16)
    diag_blocks = lambda d: jnp.concatenate([d[h * DK:(h + 1) * DK, h * DV:(h + 1) * DV] for h in range(N_HEADS)], axis=0)
    heads_lanes = lambda x: jnp.concatenate([x[h * C:(h + 1) * C, :] for h in range(N_HEADS)], axis=1)
    row = _iota((tb, tb), 0)
    col = _iota((tb, tb), 1)

    if mode == "gla":
        same = (row // C) == (col // C)
        b = _mm_split(jnp.where(same & (col <= row), 1.0, 0.0).astype(BF16), la)
        btot = _mm_split(jnp.where(same, 1.0, 0.0).astype(BF16), la)
        k_scr[...] = k
        b_scr[...] = b
        qe = q * jnp.exp(b)
        ke = k * jnp.exp(btot - b)
        sel = jnp.where(_iota((tb, nc * DV), 0) // C == _iota((tb, nc * DV), 1) // DV, 1.0, 0.0).astype(BF16)
        la_hi, la_lo = _split2(la)
        tn = (((0,), (0,)), ((), ()))
        gcols = jnp.exp(lax.dot_general(la_hi, sel, tn, preferred_element_type=F32)
                        + lax.dot_general(la_lo, sel, tn, preferred_element_type=F32))
        ind = jnp.where((_iota((HK, N_HEADS * DV), 0) // DK) == (_iota((HK, N_HEADS * DV), 1) // DV), 1.0, 0.0).astype(BF16)
        D = min(GLA_DIAG, C)
        halves = C // D
        trow = _iota((D, HK), 0)
        chunks = range(nc)
        X = []
        for c in chunks:
            xs = []
            for a in range(halves):
                r0 = c * C + a * D
                qd, bd = q[r0:r0 + D], b[r0:r0 + D]
                for s in range(D):
                    r = r0 + s
                    e = jnp.where(trow >= s, jnp.exp(jnp.minimum(bd - b_scr[r:r + 1, :], 0.0)), 0.0)
                    xs.append((qd * k_scr[r:r + 1, :] * e).astype(BF16))
            X.append(jnp.concatenate(xs, axis=0))
        R = [jnp.dot(x, ind, preferred_element_type=F32) for x in X]
        o_blocks = []
        for c in chunks:
            blocks = []
            for a in range(halves):
                r0 = c * C + a * D
                base = a * D * D
                o_d = R[c][base:base + D, :] * v_ref[r0:r0 + 1, :]
                for s in range(1, D):
                    o_d = o_d + R[c][base + s * D:base + (s + 1) * D, :] * v_ref[r0 + s:r0 + s + 1, :]
                blocks.append(o_d)
            o_blocks.append(blocks)
        if halves == 2:
            lane_head_d = _iota((D, HK), 1) // DK
            qa, ka = [], []
            for c in chunks:
                r0 = c * C
                anchor = b_scr[r0 + D - 1:r0 + D, :]
                qs = q[r0 + D:r0 + C] * jnp.exp(b[r0 + D:r0 + C] - anchor)
                qa.append(jnp.concatenate([jnp.where(lane_head_d == h, qs, 0.0) for h in range(N_HEADS)], axis=0))
                ka.append(k[r0:r0 + D] * jnp.exp(anchor - b[r0:r0 + D]))
            rows_q = nc * N_HEADS * D
            p_off = _mm_nt(jnp.concatenate(qa, axis=0), jnp.concatenate(ka, axis=0))
            own = (_iota((rows_q, nc * D), 0) // (N_HEADS * D)) == (_iota((rows_q, nc * D), 1) // D)
            v_first = jnp.concatenate([v_ref[c * C:c * C + D, :] for c in chunks], axis=0)
            o_off = _mm(jnp.where(own, p_off, 0.0), v_first)
            for c in chunks:
                base = c * N_HEADS * D
                o_blocks[c][1] = o_blocks[c][1] + jnp.concatenate(
                    [o_off[base + h * D:base + (h + 1) * D, h * DV:(h + 1) * DV] for h in range(N_HEADS)], axis=1)
        o_chunks = [jnp.concatenate(blk, axis=0) for blk in o_blocks]
        v_rows = lambda c: jnp.concatenate([v_ref[c * C:(c + 1) * C, h * DV:(h + 1) * DV] for h in range(N_HEADS)], axis=0)
        dS = [_mm_tn(heads_rows(ke[c * C:(c + 1) * C]), v_rows(c)) for c in chunks]
        L = [heads_rows(qe[c * C:(c + 1) * C]) for c in chunks]
        S = S_scr[...]
        for c in chunks:
            o_chunks[c] = o_chunks[c] + heads_lanes(jnp.dot(L[c], S.astype(BF16), preferred_element_type=F32))
            S = S * gcols[:, c * DV:(c + 1) * DV] + dS[c]
        S_scr[...] = S
        o = jnp.concatenate(o_chunks, axis=0)
    else:
        lg = p2_ref[...]
        tcol = _iota((tb, 1), 0).astype(F32)
        qe = q * jnp.exp((tcol + 1.0) * lg)
        ke = k * jnp.exp((tb - 1.0 - tcol) * lg)
        dist = (row - col).astype(F32)
        kb = k.astype(BF16)
        Q = heads_rows(q)
        o_heads = []
        for h in range(N_HEADS):
            s = lax.dot_general(Q[h * tb:(h + 1) * tb], kb, (((1,), (1,)), ((), ())), preferred_element_type=F32)
            dec = jnp.where(row >= col, jnp.exp(jnp.minimum(dist * lg[:, h * DK:h * DK + 1], 0.0)), 0.0)
            o_heads.append(_mm(s * dec, v_ref[:, h * DV:(h + 1) * DV]))
        S = S_scr[...]
        o = jnp.concatenate(o_heads, axis=1) + heads_lanes(jnp.dot(heads_rows(qe), S.astype(BF16),
                                                                   preferred_element_type=F32))
        S_scr[...] = S * jnp.exp(tb * p1_ref[...]) + diag_blocks(_mm_tn(ke, v_ref[...]))

    for h in range(N_HEADS):
        sl = slice(h * DV, (h + 1) * DV)
        o_ref[:, sl] = _head_rms_gate(o[:, sl], g_ref[:, sl], nw_ref[...]).astype(o_ref.dtype)

    @pl.when(tblk == pl.num_programs(1) - 1)
    def _():
        sout_ref[...] = S_scr[...]


def decay_mixer(mode, cols, batch, seq, blk_qk, blk_v, blk_g, aux, aux_is_cols, p1, p2, norm_w, s0):
    tb = min(SEQ_BLOCK, seq)
    C = min(CHUNK, seq) if mode == "gla" else tb
    nt = seq // tb
    HK = N_HEADS * DK
    row_map = lambda bl: (lambda b, t: (b * nt + t, bl))
    if aux_is_cols:
        aux_spec = pl.BlockSpec((tb, GROUP_W), row_map(BLK_MISC))
    else:
        aux_spec = pl.BlockSpec((tb, aux.shape[1]), lambda b, t: (t, 0))
    full = lambda a: pl.BlockSpec(a.shape, lambda b, t: (0,) * a.ndim)
    return pl.pallas_call(
        functools.partial(_decay_kernel, mode, C, tb),
        grid=(batch, nt),
        in_specs=[pl.BlockSpec((tb, GROUP_W), row_map(blk_qk)),
                  pl.BlockSpec((tb, GROUP_W), row_map(blk_v)),
                  pl.BlockSpec((tb, GROUP_W), row_map(blk_g)),
                  aux_spec, full(p1), full(p2), full(norm_w),
                  pl.BlockSpec((None, HK, DV), lambda b, t: (b, 0, 0))],
        out_specs=[pl.BlockSpec((tb, GROUP_W), lambda b, t: (b * nt + t, 0)),
                   pl.BlockSpec((None, HK, DV), lambda b, t: (b, 0, 0))],
        out_shape=[jax.ShapeDtypeStruct((batch * seq, GROUP_W), BF16),
                   jax.ShapeDtypeStruct((batch, HK, DV), F32)],
        scratch_shapes=[pltpu.VMEM((HK, DV), F32)] + [pltpu.VMEM((tb, HK), F32)] * 2,
        compiler_params=_params(("parallel", "arbitrary")),
        name="decay_" + mode,
    )(cols, cols, cols, aux, p1, p2, norm_w, s0)


IN_W = 7312
ALPHA_LO = 1024
RWKV_LORA_LO, RWKV_G_LO = 6672, 6800


def _wprep_kernel(w_ref, o_ref):
    lanes = w_ref.shape[1]
    n_main = RWKV_LORA_LO - ALPHA_LO - GLA_RANK
    g_dst = BLK_RWKV_G * GROUP_W
    misc = BLK_MISC * GROUP_W
    o_ref[:ALPHA_LO, :] = w_ref[:ALPHA_LO, :].astype(BF16)
    o_ref[ALPHA_LO:ALPHA_LO + n_main, :] = w_ref[ALPHA_LO + GLA_RANK:RWKV_LORA_LO, :].astype(BF16)
    o_ref[g_dst:g_dst + GROUP_W, :] = w_ref[RWKV_G_LO:IN_W, :].astype(BF16)
    o_ref[misc:misc + LANE, :] = w_ref[RWKV_LORA_LO:RWKV_G_LO, :].astype(BF16)
    o_ref[misc + LANE:misc + LANE + GLA_RANK, :] = w_ref[ALPHA_LO:ALPHA_LO + GLA_RANK, :].astype(BF16)
    o_ref[misc + LANE + GLA_RANK:, :] = jnp.zeros((GROUP_W - LANE - GLA_RANK, lanes), BF16)


def prep_in_weights(w_in_t, lanes=256):
    depth, n, d = w_in_t.shape
    assert n == IN_W
    return pl.pallas_call(
        _wprep_kernel,
        grid=(depth, d // lanes),
        in_specs=[pl.BlockSpec((None, n, lanes), lambda l, i: (l, 0, i))],
        out_specs=pl.BlockSpec((None, COLS_W, lanes), lambda l, i: (l, 0, i)),
        out_shape=jax.ShapeDtypeStruct((depth, COLS_W, d), BF16),
        compiler_params=_params(("parallel", "parallel")),
        name="prep_in_weights",
    )(w_in_t)


def gla_mixer(cols, batch, seq, lp, s0):
    HK = N_HEADS * DK
    w_a2 = jnp.zeros((LANE, HK), F32).at[:GLA_RANK].set(lp['gla_w_a2']).astype(BF16)
    b_a = lp['gla_b_a'].reshape(1, HK)
    return decay_mixer("gla", cols, batch, seq, BLK_GLA_QK, BLK_GLA_V, BLK_GLA_G, cols, True,
                       w_a2, b_a, lp['gla_norm'].reshape(1, DV), s0)


def ret_mixer(cols, batch, seq, lp, s0, pos):
    HK = N_HEADS * DK
    half = DK // 2
    inv = ROPE_BASE ** (-jnp.arange(half, dtype=F32) / half)
    ang = pos.astype(F32)[:, None] * inv[None, :]
    cos = jnp.tile(jnp.cos(ang), (1, 2 * N_HEADS))
    sin = jnp.tile(jnp.concatenate([-jnp.sin(ang), jnp.sin(ang)], axis=1), (1, N_HEADS))
    tables = jnp.concatenate([cos, sin], axis=1)
    log_gamma = jnp.log(1.0 - 2.0 ** (-5.0 - jnp.arange(N_HEADS, dtype=F32)))
    la = jnp.repeat(log_gamma, DK)
    return decay_mixer("ret", cols, batch, seq, BLK_RET_QK, BLK_RET_V, BLK_RET_G, tables, False,
                       jnp.broadcast_to(la[:, None], (HK, DV)), la.reshape(1, HK), lp['ret_norm'].reshape(1, DV), s0)


def _t5_bucket(rel):
    n = jnp.maximum(rel, 0)
    max_exact = REL_BUCKETS // 2
    nf = jnp.maximum(n, 1).astype(F32)
    large = max_exact + (jnp.log(nf / max_exact) / math.log(REL_MAX_DIST / max_exact)
                         * (REL_BUCKETS - max_exact)).astype(jnp.int32)
    large = jnp.minimum(large, REL_BUCKETS - 1)
    return jnp.where(n < max_exact, n, large)


def _rel_bias_tile(rel_bias, rel):
    onehot = (_t5_bucket(rel)[..., None] == jnp.arange(REL_BUCKETS, dtype=jnp.int32)).astype(F32)
    b = jnp.einsum('qkb,bh->hqk', onehot, rel_bias.astype(F32), precision=HIGHEST)
    return jnp.where(rel[None] >= 0, b, NEG_BIG)


def _lambda(lam_ref, lam_init):
    s1 = jnp.sum(lam_ref[0:1, :] * lam_ref[1:2, :], axis=-1, keepdims=True)
    s2 = jnp.sum(lam_ref[2:3, :] * lam_ref[3:4, :], axis=-1, keepdims=True)
    return jnp.exp(s1) - jnp.exp(s2) + lam_init


def _diff_finish(o1, o2, lam, lam_init, g, sw):
    o = o1 - lam * o2
    y = o * lax.rsqrt(jnp.mean(o * o, axis=-1, keepdims=True) + EPS) * sw * (1.0 - lam_init)
    return y * (g * _sigmoid(g))


def _diffattn_kernel(lam_init, tq, tk, q_ref, k_ref, v_ref, g_ref, bias_ref, lam_ref, sw_ref,
                     o_ref, kb_scr, vt_scr):
    qi = pl.program_id(2)
    nh = q_ref.shape[1] // DV
    heads = range(nh)
    hs = lambda h: slice(h * DV, (h + 1) * DV)

    @pl.when(qi == 0)
    def _():
        for h in heads:
            kb_scr[h] = k_ref[:, hs(h)].astype(BF16)
            for j in range(vt_scr.shape[1]):
                vt_scr[h, j] = v_ref[j * tk:(j + 1) * tk, hs(h)].T.astype(BF16)

    first_map = _iota((tq, DV), 1) < DK
    Q = []
    for h in heads:
        q = q_ref[:, hs(h)] * (DK ** -0.5 * LOG2E)
        Q.append(jnp.concatenate([jnp.where(first_map, q, 0.0), jnp.where(first_map, 0.0, q)], axis=0).astype(BF16))

    def step(j, carry, bias_idx):
        m, l, acc = carry
        r0 = pl.multiple_of(j * tk, tk)
        st = [lax.dot_general(kb_scr[h, pl.ds(r0, tk), :], Q[h], (((1,), (1,)), ((), ())),
                              preferred_element_type=F32) for h in heads]
        if bias_idx is not None:
            st = [st[h] + jnp.concatenate([bias_ref[h, bias_idx]] * 2, axis=1) for h in heads]
        m_new = [jnp.maximum(m[h], jnp.max(st[h], axis=0, keepdims=True)) for h in heads]
        p = [jnp.exp2(st[h] - m_new[h]) for h in heads]
        alpha = [jnp.exp2(m[h] - m_new[h]) for h in heads]
        l = [alpha[h] * l[h] + jnp.sum(p[h], axis=0, keepdims=True) for h in heads]
        acc = [alpha[h] * acc[h] + jnp.dot(vt_scr[h, j], p[h].astype(BF16), preferred_element_type=F32)
               for h in heads]
        return m_new, l, acc

    carry = ([jnp.full((1, 2 * tq), NEG_BIG, F32)] * nh, [jnp.zeros((1, 2 * tq), F32)] * nh,
             [jnp.zeros((DV, 2 * tq), F32)] * nh)
    first_near = jnp.maximum(qi - 1, 0)
    carry = lax.fori_loop(0, first_near, lambda j, c: step(j, c, None), carry)
    _, l, acc = lax.fori_loop(first_near, qi + 1, lambda j, c: step(j, c, j - qi + 1), carry)

    lam = _lambda(lam_ref, lam_init)
    for h in heads:
        ot = acc[h][:, :tq] / l[h][:, :tq] - lam * (acc[h][:, tq:] / l[h][:, tq:])
        yt = ot * lax.rsqrt(jnp.mean(ot * ot, axis=0, keepdims=True) + EPS) * sw_ref[...] * (1.0 - lam_init)
        g = g_ref[:, hs(h)]
        o_ref[:, hs(h)] = (yt.T * (g * _sigmoid(g))).astype(o_ref.dtype)


def prompt_bias_tiles(rel_bias, seq):
    tq = tk = min(ATTN_BLOCK, seq)
    assert tk >= REL_MAX_DIST
    d = _iota((tk, tq), 1) - _iota((tk, tq), 0)
    far = rel_bias.astype(F32)[REL_BUCKETS - 1][:, None, None]
    return jnp.stack([_rel_bias_tile(rel_bias, d + tk) - far, _rel_bias_tile(rel_bias, d) - far], axis=1) * LOG2E


def diff_attn_prompt(cols, batch, seq, lp, bias_tiles, lam_init):
    tq = tk = min(ATTN_BLOCK, seq)
    nq = seq // tq
    lam4 = jnp.stack([lp['diff_lam_q1'], lp['diff_lam_k1'], lp['diff_lam_q2'], lp['diff_lam_k2']]).astype(F32)
    const = lambda a: pl.BlockSpec(a.shape, lambda b, hg, i: (0,) * a.ndim)
    sw = lp['diff_subln'].reshape(DV, 1)
    nh = ATTN_HEADS
    wg = nh * DV
    per = GROUP_W // wg
    return pl.pallas_call(
        functools.partial(_diffattn_kernel, lam_init, tq, tk),
        grid=(batch, N_HEADS // nh, nq),
        in_specs=[pl.BlockSpec((tq, wg), lambda b, hg, i: (b * nq + i, BLK_DIFF_Q * per + hg)),
                  pl.BlockSpec((seq, wg), lambda b, hg, i: (b, BLK_DIFF_K * per + hg)),
                  pl.BlockSpec((seq, wg), lambda b, hg, i: (b, BLK_DIFF_V * per + hg)),
                  pl.BlockSpec((tq, wg), lambda b, hg, i: (b * nq + i, BLK_DIFF_G * per + hg)),
                  pl.BlockSpec((nh, 2, tk, tq), lambda b, hg, i: (hg, 0, 0, 0)), const(lam4), const(sw)],
        out_specs=pl.BlockSpec((tq, wg), lambda b, hg, i: (b * nq + i, hg)),
        out_shape=jax.ShapeDtypeStruct((batch * seq, GROUP_W), BF16),
        scratch_shapes=[pltpu.VMEM((nh, seq, DV), BF16), pltpu.VMEM((nh, seq // tk, DV, tk), BF16)],
        compiler_params=_params(("arbitrary", "arbitrary", "arbitrary"), BIG_VMEM_LIMIT),
        name="diff_attn_prompt",
    )(cols, cols, cols, cols, bias_tiles, lam4, sw)


def _diffdec_kernel(lam_init, PP, ts, page, pt_ref, q_ref, kn_ref, vn_ref, g_ref, *rest):
    k_refs = rest[:PP]
    v_refs = rest[PP:2 * PP]
    bias_ref, biasn_ref, lam_ref, sw_ref, o_ref, q_scr = rest[2 * PP:2 * PP + 6]
    state = rest[2 * PP + 6:]
    m_scr, l_scr, acc_scr = state[0::3], state[1::3], state[2::3]
    NCH = len(m_scr)
    per = PP // NCH
    nkc = per * page
    step_id = pl.program_id(1)
    W = N_HEADS * DV

    @pl.when(step_id == 0)
    def _():
        q = q_ref[...] * (DK ** -0.5)
        lane = _iota(q.shape, 1)
        pieces = []
        for h in range(N_HEADS):
            for mp in range(2):
                lo = h * DV + mp * DK
                pieces.append(jnp.where((lane >= lo) & (lane < lo + DK), q, 0.0))
        q_scr[...] = jnp.concatenate(pieces, axis=0).astype(BF16)
        for c in range(NCH):
            m_scr[c][...] = jnp.full(m_scr[c].shape, NEG_BIG, F32)
            l_scr[c][...] = jnp.zeros(l_scr[c].shape, F32)
            acc_scr[c][...] = jnp.zeros(acc_scr[c].shape, F32)

    def update(chains, kb, vb, bias):
        qb = q_scr[...]
        n = range(len(chains))
        s = [lax.dot_general(qb, kb[i], (((1,), (1,)), ((), ())), preferred_element_type=F32) + bias[i] for i in n]
        m_old = [m_scr[c][...] for c in chains]
        m_new = [jnp.maximum(m_old[i], jnp.max(s[i], axis=-1, keepdims=True)) for i in n]
        p = [jnp.exp(s[i] - m_new[i]) for i in n]
        alpha = [jnp.exp(m_old[i] - m_new[i]) for i in n]
        for i, c in enumerate(chains):
            l_scr[c][...] = alpha[i] * l_scr[c][...] + jnp.sum(p[i], axis=-1, keepdims=True)
            acc_scr[c][...] = alpha[i] * acc_scr[c][...] + jnp.dot(p[i].astype(BF16), vb[i], preferred_element_type=F32)
            m_scr[c][...] = m_new[i]

    def page_rows(ref):
        return jnp.concatenate([ref[pl.ds(h, page, stride=N_HEADS), :] for h in range(N_HEADS)], axis=1).astype(BF16)

    update(list(range(NCH)),
           [jnp.concatenate([page_rows(r) for r in k_refs[c * per:(c + 1) * per]], axis=0) for c in range(NCH)],
           [jnp.concatenate([page_rows(r) for r in v_refs[c * per:(c + 1) * per]], axis=0) for c in range(NCH)],
           [bias_ref[:, c * nkc:(c + 1) * nkc] for c in range(NCH)])

    @pl.when(step_id == pl.num_programs(1) - 1)
    def _():
        pad = jnp.zeros((page - ts, W), BF16)
        update([0], [jnp.concatenate([kn_ref[...].astype(BF16), pad], axis=0)],
               [jnp.concatenate([vn_ref[...].astype(BF16), pad], axis=0)], [biasn_ref[...]])
        m = m_scr[0][...]
        for c in range(1, NCH):
            m = jnp.maximum(m, m_scr[c][...])
        w = [jnp.exp(m_scr[c][...] - m) for c in range(NCH)]
        ln_all = sum(w[c] * l_scr[c][...] for c in range(NCH))
        acc_all = sum(w[c] * acc_scr[c][...] for c in range(NCH))
        lam = _lambda(lam_ref, lam_init)
        rows = 2 * ts
        for h in range(N_HEADS):
            a = acc_all[h * rows:(h + 1) * rows, h * DV:(h + 1) * DV]
            ln = ln_all[h * rows:(h + 1) * rows, :]
            o_ref[:, h * DV:(h + 1) * DV] = _diff_finish(
                a[:ts] / ln[:ts], a[ts:] / ln[ts:], lam, lam_init,
                g_ref[:, h * DV:(h + 1) * DV], sw_ref[...]).astype(o_ref.dtype)


def sample_bias_tiles(rel_bias, ts, n_pages, page):
    n_steps = n_pages // DECODE_PAGES
    past = n_pages * page
    nk = DECODE_PAGES * page
    assert nk >= REL_MAX_DIST
    t_of_row = jnp.tile(jnp.arange(ts, dtype=jnp.int32), N_HEADS * 2)
    h_of_row = jnp.repeat(jnp.arange(N_HEADS, dtype=jnp.int32), 2 * ts)
    head_sel = h_of_row[None, :, None] == jnp.arange(N_HEADS, dtype=jnp.int32)[:, None, None]

    def rows_bias(k_pos):
        rel = past + t_of_row[:, None] - k_pos[None, :]
        return jnp.sum(jnp.where(head_sel, _rel_bias_tile(rel_bias, rel), 0.0), axis=0)

    kk = jnp.arange(nk, dtype=jnp.int32)
    bias_steps = jnp.stack([rows_bias(kk), rows_bias((n_steps - 1) * nk + kk)])
    kn = jnp.arange(page, dtype=jnp.int32)
    bias_new = jnp.where(kn[None, :] < ts, rows_bias(past + kn), NEG_BIG)
    return bias_steps, bias_new


def diff_attn_sample(cols, batch, ts, cache_k, cache_v, layer, page_table, lp, bias_tiles, lam_init):
    page = cache_k.shape[2] // N_HEADS
    W = N_HEADS * DV
    n_pages = page_table.shape[1]
    PP = DECODE_PAGES
    n_steps = n_pages // PP
    nk = PP * page
    rows = N_HEADS * 2 * ts
    bias_steps, bias_new = bias_tiles
    lam4 = jnp.stack([lp['diff_lam_q1'], lp['diff_lam_k1'], lp['diff_lam_q2'], lp['diff_lam_k2']]).astype(F32)

    def page_spec(i):
        return pl.BlockSpec((None, None, page * N_HEADS, DV),
                            lambda b, s, pt: (layer, pt[b * n_pages + s * PP + i], 0, 0))

    col_spec = lambda blk: pl.BlockSpec((ts, GROUP_W), lambda b, s, pt: (b, blk))
    const = lambda a: pl.BlockSpec(a.shape, lambda b, s, pt: (0,) * a.ndim)
    grid_spec = pltpu.PrefetchScalarGridSpec(
        num_scalar_prefetch=1,
        grid=(batch, n_steps),
        in_specs=[col_spec(BLK_DIFF_Q), col_spec(BLK_DIFF_K), col_spec(BLK_DIFF_V), col_spec(BLK_DIFF_G)]
                 + [page_spec(i) for i in range(PP)] + [page_spec(i) for i in range(PP)]
                 + [pl.BlockSpec((None, rows, nk), lambda b, s, pt: (jnp.where(s == n_steps - 1, 1, 0), 0, 0)),
                    const(bias_new), const(lam4), pl.BlockSpec((1, DV), lambda b, s, pt: (0, 0))],
        out_specs=pl.BlockSpec((ts, GROUP_W), lambda b, s, pt: (b, 0)),
        scratch_shapes=[pltpu.VMEM((rows, W), BF16)]
                       + [pltpu.VMEM((rows, 1), F32), pltpu.VMEM((rows, 1), F32), pltpu.VMEM((rows, W), F32)] * DECODE_CHAINS)
    return pl.pallas_call(
        functools.partial(_diffdec_kernel, lam_init, PP, ts, page),
        grid_spec=grid_spec,
        out_shape=jax.ShapeDtypeStruct((batch * ts, GROUP_W), BF16),
        compiler_params=_params(("parallel", "arbitrary")),
        name="diff_attn_sample",
    )(page_table.reshape(-1), cols, cols, cols, cols, *([cache_k] * PP), *([cache_v] * PP),
      bias_steps, bias_new, lam4, lp['diff_subln'].reshape(1, DV))


RWKV_PAIRS = RWKV_HEADS // 2
SHIFT_PAD = 4 * GROUP_W


def _rwkv_kernel(C, tb, r_ref, k_ref, v_ref, misc_ref, g_ref, shift_ref, s0_ref,
                 mu_ref, w0_ref, ww2_ref, a0_ref, aw2_ref, kk_ref, ka_ref, rk_ref, nw_ref,
                 o_ref, sout_ref, S_scr, carry_scr):
    G = GROUP_W
    tblk = pl.program_id(1)

    @pl.when(tblk == 0)
    def _():
        S_scr[...] = s0_ref[...]
        carry_scr[...] = shift_ref[...]

    first_row = _iota((tb, 1), 0) == 0

    def token_shift(p, lo, width):
        prev = jnp.where(first_row, carry_scr[:, lo:lo + width], pltpu.roll(p, 1, axis=0))
        carry_scr[:, lo:lo + width] = p[tb - 1:tb, :]
        return p + mu_ref[:, lo:lo + width] * (prev - p)

    r = token_shift(r_ref[...], 0, G)
    k = token_shift(k_ref[...], G, G)
    v = token_shift(v_ref[...], 2 * G, G)
    lora = token_shift(misc_ref[:, :LANE], 3 * G, LANE)

    w_pre = w0_ref[...] + jnp.dot(jnp.tanh(lora).astype(BF16), ww2_ref[...], preferred_element_type=F32)
    lw = -jnp.exp(-_softplus(-w_pre) - 0.5)
    a = _sigmoid(a0_ref[...] + jnp.dot(lora.astype(BF16), aw2_ref[...], preferred_element_type=F32))

    kk = k * kk_ref[...]
    kk = kk / jnp.maximum(jnp.sqrt(_head_sums(kk * kk)), 1e-12)
    k2 = k * (1.0 + (a - 1.0) * ka_ref[...])
    bb = kk * a
    bonus = _head_sums(r * k2 * rk_ref[...]) * v

    row = _iota((tb, tb), 0)
    col = _iota((tb, tb), 1)
    same = (row // C) == (col // C)
    cum = _mm_split(jnp.where(same & (col <= row), 1.0, 0.0).astype(BF16), lw)
    tot = _mm_split(jnp.where(same, 1.0, 0.0).astype(BF16), lw)
    inv = jnp.exp(-cum)
    rest = jnp.exp(tot - cum)
    rt = r * jnp.exp(cum)
    kt = kk * jnp.exp(cum - lw)
    kd = k2 * inv
    bd = bb * inv
    kh = k2 * rest
    bh = bb * rest

    C2 = 2 * C
    nc = tb // C
    lane0 = _iota((C, LANE), 1) < RWKV_N
    rr = _iota((C2, C2), 0)
    cc = _iota((C2, C2), 1)
    strict = (rr % C) > (cc % C)
    incl = (rr % C) >= (cc % C)
    eye = jnp.where(rr == cc, 1.0, 0.0)
    eye_lane = _iota((LANE, LANE), 0) == _iota((LANE, LANE), 1)
    n_double = max(int(math.log2(C)) - 1, 0)
    probs = [(c, p) for c in range(nc) for p in range(RWKV_PAIRS)]

    def split(x, c, p):
        xs = x[c * C:(c + 1) * C, p * LANE:(p + 1) * LANE]
        return jnp.concatenate([jnp.where(lane0, xs, 0.0), jnp.where(lane0, 0.0, xs)], axis=0).astype(BF16)

    Lk = [split(kt, c, p) for c, p in probs]
    Lr = [split(rt, c, p) for c, p in probs]
    Rk = [split(kd, c, p) for c, p in probs]
    Rb = [split(bd, c, p) for c, p in probs]
    Vb = [split(v, c, p) for c, p in probs]
    Kh = [split(kh, c, p) for c, p in probs]
    Bh = [split(bh, c, p) for c, p in probs]
    if C2 % LANE == 0:
        G = [_mm_nt(jnp.concatenate([x, y], axis=0), jnp.concatenate([z, u], axis=0))
             for x, y, z, u in zip(Lk, Lr, Rk, Rb)]
        g_kk, g_kb = [g[:C2, :C2] for g in G], [g[:C2, C2:] for g in G]
        g_rk, g_rb = [g[C2:, :C2] for g in G], [g[C2:, C2:] for g in G]
    else:
        g_kk, g_kb = [_mm_nt(x, y) for x, y in zip(Lk, Rk)], [_mm_nt(x, y) for x, y in zip(Lk, Rb)]
        g_rk, g_rb = [_mm_nt(x, y) for x, y in zip(Lr, Rk)], [_mm_nt(x, y) for x, y in zip(Lr, Rb)]
    A_kr = [jnp.concatenate([jnp.where(strict, x, 0.0), jnp.where(incl, y, 0.0)], axis=0).astype(BF16)
            for x, y in zip(g_kk, g_rk)]
    A_rb = [jnp.where(incl, x, 0.0).astype(BF16) for x in g_rb]
    P = [jnp.where(strict, -x, 0.0) for x in g_kb]
    T = [eye + x for x in P]
    if n_double > 0:
        P = [_mm(x, x) for x in P]
    for it in range(n_double):
        if it < n_double - 1:
            TP = [_mm(jnp.concatenate([t, x], axis=0), x) for t, x in zip(T, P)]
            T = [t + y[:C2] for t, y in zip(T, TP)]
            P = [y[C2:] for y in TP]
        else:
            T = [t + _mm(t, x) for t, x in zip(T, P)]
    T = [t.astype(BF16) for t in T]
    AV = [_mm(x, y) for x, y in zip(A_kr, Vb)]
    TW = [_mm(t, jnp.concatenate([x, y[:C2].astype(BF16)], axis=1)).astype(BF16)
          for t, x, y in zip(T, Lk, AV)]
    AW = [_mm(x, y) for x, y in zip(A_rb, TW)]
    Wr = [x.astype(F32) - y[:, :LANE] for x, y in zip(Lr, AW)]
    O0 = [x[C2:] - y[:, LANE:] for x, y in zip(AV, AW)]
    BW = [_mm_tn(x, y) for x, y in zip(Bh, TW)]
    D0 = [_mm_tn(x, y) - z[:, LANE:] for x, y, z in zip(Kh, Vb, BW)]
    WN = [jnp.concatenate([x, y[:, :LANE]], axis=0).astype(BF16) for x, y in zip(Wr, BW)]
    gcol = [jnp.exp(jnp.sum(jnp.where(eye_lane, tot[c * C:c * C + 1, p * LANE:(p + 1) * LANE], 0.0),
                            axis=1, keepdims=True)) for c, p in probs]

    S = [S_scr[p] for p in range(RWKV_PAIRS)]
    o_rows = []
    for c in range(nc):
        o_lanes = []
        for p in range(RWKV_PAIRS):
            i = c * RWKV_PAIRS + p
            Rm = jnp.dot(WN[i], S[p].astype(BF16), preferred_element_type=F32)
            O = Rm[:C2] + O0[i]
            o_lanes.append(O[:C] + O[C:])
            S[p] = S[p] * gcol[i] - Rm[C2:] + D0[i]
        o_rows.append(jnp.concatenate(o_lanes, axis=1))
    for p in range(RWKV_PAIRS):
        S_scr[p] = S[p]

    o = jnp.concatenate(o_rows, axis=0)
    ms = _head_sums(o * o) * (1.0 / RWKV_N)
    y = o * lax.rsqrt(ms + EPS) * nw_ref[...] + bonus
    g = g_ref[...]
    o_ref[...] = (y * (g * _sigmoid(g))).astype(o_ref.dtype)

    @pl.when(tblk == pl.num_programs(1) - 1)
    def _():
        sout_ref[...] = S_scr[...]


def rwkv_mixer(cols, batch, seq, lp, s0_pairs, shift_pad):
    C = min(64, seq)
    tb = min(SEQ_BLOCK, seq)
    nt = seq // tb
    G = GROUP_W
    mu = lp['rwkv_mu']
    mu_pad = jnp.concatenate([mu, jnp.zeros((SHIFT_PAD - mu.shape[0],), F32)]).reshape(1, SHIFT_PAD)
    ww2 = jnp.zeros((LANE, G), F32).at[:RWKV_LORA].set(lp['rwkv_w_w2']).astype(BF16)
    aw2 = jnp.zeros((LANE, G), F32).at[RWKV_LORA:].set(lp['rwkv_a_w2']).astype(BF16)
    vec = lambda a: a.reshape(1, G).astype(F32)
    params = [mu_pad, vec(lp['rwkv_w0']), ww2, vec(lp['rwkv_a0']), aw2, vec(lp['rwkv_k_k']), vec(lp['rwkv_k_a']),
              vec(lp['rwkv_r_k']), jnp.tile(lp['rwkv_norm'], RWKV_HEADS).reshape(1, G)]
    row_map = lambda bl: (lambda b, t: (b * nt + t, bl))
    full = lambda a: pl.BlockSpec(a.shape, lambda b, t: (0,) * a.ndim)
    return pl.pallas_call(
        functools.partial(_rwkv_kernel, C, tb),
        grid=(batch, nt),
        in_specs=[pl.BlockSpec((tb, G), row_map(BLK_RWKV_R)), pl.BlockSpec((tb, G), row_map(BLK_RWKV_K)),
                  pl.BlockSpec((tb, G), row_map(BLK_RWKV_V)), pl.BlockSpec((tb, G), row_map(BLK_MISC)),
                  pl.BlockSpec((tb, G), row_map(BLK_RWKV_G)),
                  pl.BlockSpec((None, 1, SHIFT_PAD), lambda b, t: (b, 0, 0)),
                  pl.BlockSpec((None, RWKV_PAIRS, LANE, LANE), lambda b, t: (b, 0, 0, 0))]
                 + [full(a) for a in params],
        out_specs=[pl.BlockSpec((tb, G), lambda b, t: (b * nt + t, 0)),
                   pl.BlockSpec((None, RWKV_PAIRS, LANE, LANE), lambda b, t: (b, 0, 0, 0))],
        out_shape=[jax.ShapeDtypeStruct((batch * seq, G), BF16),
                   jax.ShapeDtypeStruct((batch, RWKV_PAIRS, LANE, LANE), F32)],
        scratch_shapes=[pltpu.VMEM((RWKV_PAIRS, LANE, LANE), F32), pltpu.VMEM((1, SHIFT_PAD), F32)],
        compiler_params=_params(("parallel", "arbitrary")),
        name="rwkv7",
    )(cols, cols, cols, cols, cols, shift_pad, s0_pairs, *params)


def rwkv_state_to_pairs(s):
    b = s.shape[0]
    st = jnp.swapaxes(s, -1, -2).reshape(b, RWKV_PAIRS, 2, RWKV_N, RWKV_N)
    z = jnp.zeros_like(st[:, :, 0])
    top = jnp.concatenate([st[:, :, 0], z], axis=-1)
    bot = jnp.concatenate([z, st[:, :, 1]], axis=-1)
    return jnp.concatenate([top, bot], axis=-2)


def rwkv_pairs_to_state(sp):
    b = sp.shape[0]
    h0 = sp[:, :, :RWKV_N, :RWKV_N]
    h1 = sp[:, :, RWKV_N:, RWKV_N:]
    st = jnp.stack([h0, h1], axis=2).reshape(b, RWKV_HEADS, RWKV_N, RWKV_N)
    return jnp.swapaxes(st, -1, -2)


def _group_layer(x, scale, shift, gate, rows_per_mod, batch, seq, lp, w_pad, w_o_bf, final_w, final,
                 s_gla, s_ret, s_rwkv, s_shift, pos, diff_fn, tm_in, tm_out, k_all, v_all, layer):
    cols, k_all, v_all = in_proj(x, scale, shift, lp['norm_w'], w_pad, k_all, v_all, layer, rows_per_mod, tm_in, IN_TN)
    oa, n_gla = gla_mixer(cols, batch, seq, lp, s_gla)
    ob = diff_fn(cols)
    oc, n_ret = ret_mixer(cols, batch, seq, lp, s_ret, pos)
    od, n_rwkv = rwkv_mixer(cols, batch, seq, lp, s_rwkv, s_shift)
    y = out_proj(x, gate, oa, ob, oc, od, w_o_bf, layer, final_w, rows_per_mod, tm_out, final)
    last = cols.reshape(batch, seq, COLS_W)[:, -1]
    n_shift = jnp.concatenate([last[:, BLK_RWKV_R * GROUP_W:(BLK_RWKV_V + 1) * GROUP_W],
                               last[:, BLK_MISC * GROUP_W:BLK_MISC * GROUP_W + 2 * RWKV_LORA]], axis=-1)
    return y, k_all, v_all, (n_gla.reshape(batch, N_HEADS, DK, DV), n_ret.reshape(batch, N_HEADS, DK, DV),
                             rwkv_pairs_to_state(n_rwkv), n_shift)


def kernel(x_prompt, x_sample, cache_k, cache_v, state_gla, state_ret, state_rwkv, state_shift, page_table,
           c_prompt, c_sample, rel_bias, w_ada, b_ada, norm_w, w_in, w_o, gla_w_a2, gla_b_a, gla_norm,
           diff_lam_q1, diff_lam_k1, diff_lam_q2, diff_lam_k2, diff_subln, ret_norm, rwkv_mu, rwkv_w0, rwkv_w_w2,
           rwkv_a0, rwkv_a_w2, rwkv_k_k, rwkv_k_a, rwkv_r_k, rwkv_norm, final_norm):
    Bp, Sp, D = x_prompt.shape
    Bs, Ts, _ = x_sample.shape
    depth = w_in.shape[0]
    n_pool, page = cache_k.shape[1], cache_k.shape[2]
    past_len = page_table.shape[1] * page
    ck = cache_k.reshape(depth, n_pool, page * N_HEADS, DV)
    cv = cache_v.reshape(depth, n_pool, page * N_HEADS, DV)
    pos_p = jnp.arange(Sp, dtype=jnp.int32)
    pos_s = past_len + jnp.arange(Ts, dtype=jnp.int32)
    c_all = jnp.concatenate([c_prompt, c_sample], axis=0)
    xp = x_prompt.reshape(Bp * Sp, D)
    xs = x_sample.reshape(Bs * Ts, D)
    zeros_dec = jnp.zeros((Bp, N_HEADS * DK, DV), F32)
    zeros_rwkv = jnp.zeros((Bp, RWKV_PAIRS, LANE, LANE), F32)
    zeros_shift = jnp.zeros((Bp, 1, SHIFT_PAD), F32)
    bias_p = prompt_bias_tiles(rel_bias, Sp)
    bias_s = sample_bias_tiles(rel_bias, Ts, page_table.shape[1], page)
    kp = jnp.zeros((depth, Bp * Sp * N_HEADS, DV), F32)
    vp = jnp.zeros((depth, Bp * Sp * N_HEADS, DV), F32)
    ks = jnp.zeros((depth, Bs * Ts * N_HEADS, DV), F32)
    vs = jnp.zeros((depth, Bs * Ts * N_HEADS, DV), F32)
    w_pad_all = prep_in_weights(jnp.swapaxes(w_in, 1, 2))
    w_o_all = w_o.astype(BF16)
    new_p, new_s = [], []
    for l in range(depth):
        lp = dict(norm_w=norm_w[l], gla_w_a2=gla_w_a2[l], gla_b_a=gla_b_a[l], gla_norm=gla_norm[l],
                  diff_lam_q1=diff_lam_q1[l], diff_lam_k1=diff_lam_k1[l], diff_lam_q2=diff_lam_q2[l],
                  diff_lam_k2=diff_lam_k2[l], diff_subln=diff_subln[l], ret_norm=ret_norm[l],
                  rwkv_mu=rwkv_mu[l], rwkv_w0=rwkv_w0[l], rwkv_w_w2=rwkv_w_w2[l], rwkv_a0=rwkv_a0[l],
                  rwkv_a_w2=rwkv_a_w2[l], rwkv_k_k=rwkv_k_k[l], rwkv_k_a=rwkv_k_a[l], rwkv_r_k=rwkv_r_k[l],
                  rwkv_norm=rwkv_norm[l])
        final = l == depth - 1
        lam_init = 0.8 - 0.6 * math.exp(-0.3 * l)
        mod = ada_mod(c_all, w_ada, b_ada, l)
        shift, scale, gate = mod[:, :D], mod[:, D:2 * D], mod[:, 2 * D:]
        w_pad, w_o_bf = w_pad_all, w_o_all

        per_batch = lambda a: a[:Bp].reshape(Bp, 1, D)
        xp, kp, vp, st_p = _group_layer(
            xp, per_batch(scale), per_batch(shift), per_batch(gate), Sp, Bp, Sp, lp, w_pad, w_o_bf, final_norm, final,
            zeros_dec, zeros_dec, zeros_rwkv, zeros_shift, pos_p,
            lambda cols: diff_attn_prompt(cols, Bp, Sp, lp, bias_p, lam_init), min(IN_TM, Sp), min(OUT_TM, Sp),
            kp, vp, l)

        per_row = lambda a: jnp.repeat(a[Bp:], Ts, axis=0)
        sh = state_shift[l]
        sh_pad = jnp.concatenate([sh, jnp.zeros((Bs, SHIFT_PAD - sh.shape[1]), F32)], axis=1).reshape(Bs, 1, SHIFT_PAD)
        xs, ks, vs, st_s = _group_layer(
            xs, per_row(scale), per_row(shift), per_row(gate), 1, Bs, Ts, lp, w_pad, w_o_bf, final_norm, final,
            state_gla[l].reshape(Bs, N_HEADS * DK, DV), state_ret[l].reshape(Bs, N_HEADS * DK, DV),
            rwkv_state_to_pairs(state_rwkv[l]), sh_pad, pos_s,
            lambda cols: diff_attn_sample(cols, Bs, Ts, ck, cv, l, page_table, lp, bias_s, lam_init),
            Bs * Ts, Bs * Ts, ks, vs, l)
        new_p.append(st_p)
        new_s.append(st_s)
    stack = lambda states, i: jnp.stack([s[i] for s in states])
    heads = lambda a, b, t: a.reshape(depth, b, t, N_HEADS, DV)
    return (xp.reshape(Bp, Sp, D), xs.reshape(Bs, Ts, D),
            heads(kp, Bp, Sp), heads(vp, Bp, Sp), heads(ks, Bs, Ts), heads(vs, Bs, Ts),
            stack(new_p, 0), stack(new_s, 0), stack(new_p, 1), stack(new_s, 1),
            stack(new_p, 2), stack(new_s, 2), stack(new_p, 3), stack(new_s, 3))
```
